```python
import math
import jax
import jax.numpy as jnp
from jax import lax
import numpy as np

D_MODEL = 2048
BATCH = 8
SEQ = 4096
DEPTH = 4

GRID_W = 64
CTX_LEN = 256
N_MIXERS = 2
N_HYENA_LAYERS = (DEPTH + 1) // 2
N_ATTN_LAYERS = DEPTH // 2
HY_ORDER = 2
HY_EMB = 33
HY_BANDS = (HY_EMB - 1) // 2
HY_FILTER_W = 64
HY_FAST_DECAY_PCT = 0.3
HY_SLOW_DECAY_PCT = 1.5
HY_DECAY_TARGET = 1e-2
HEAD_DIM = 64
N_Q_HEADS = D_MODEL // HEAD_DIM
N_KV_HEADS = 4
Q_PER_KV = N_Q_HEADS // N_KV_HEADS
D_Q = N_Q_HEADS * HEAD_DIM
D_KV = N_KV_HEADS * HEAD_DIM
WINDOW = 128
Q_BLOCK = 128
ROPE_PAIRS = HEAD_DIM // 4
ROPE_BASE = 10000.0
N_GROUPS = 8
EXPERTS_PER_GROUP = 8
N_EXPERTS = N_GROUPS * EXPERTS_PER_GROUP
TOP_K = 2
D_EXPERT = 3 * D_MODEL // 16
MOE_BLOCK = 256
EPS = 1e-6

kernel_name = 'hybrid_hyena_swa_hmoe_dit'


def rms_norm(x, gain=None):
    xf = x.astype(jnp.float32)
    y = xf * lax.rsqrt(jnp.mean(xf * xf, axis=-1, keepdims=True) + EPS)
    if gain is not None:
        y = y * gain.astype(jnp.float32)
    return y.astype(x.dtype)


def modulate(x, shift, scale):
    return rms_norm(x) * (1 + scale) + shift


def ada_modulation(cond, w, b):
    m = jax.nn.silu(cond) @ w + b
    return jnp.split(m, 6, axis=-1)


def short_conv(u, w, b):
    L = u.shape[1]
    up = jnp.pad(u, ((0, 0), (1, 1), (0, 0)))
    return up[:, :L] * w[0] + up[:, 1:L + 1] * w[1] + up[:, 2:L + 2] * w[2] + b


def hyena_filters(L, w1, b1, freq1, w2, b2, freq2, w3):
    f32 = jnp.float32
    t = jnp.linspace(0.0, 1.0, L, dtype=f32)[:, None]
    omega = (2.0 * math.pi / L) * jnp.arange(L, dtype=f32)[:, None]
    bands = jnp.linspace(1e-4, HY_BANDS - 1, HY_BANDS, dtype=f32)[None, :]
    feats = jnp.concatenate([t, jnp.cos(bands * omega), -jnp.sin(bands * omega)], axis=-1)
    h = jnp.sin(freq1.astype(f32) * (feats @ w1.astype(f32) + b1.astype(f32)))
    h = jnp.sin(freq2.astype(f32) * (h @ w2.astype(f32) + b2.astype(f32)))
    h = (h @ w3.astype(f32)).reshape(L, HY_ORDER, 2, D_MODEL)
    max_decay = math.log(HY_DECAY_TARGET) / HY_FAST_DECAY_PCT
    min_decay = math.log(HY_DECAY_TARGET) / HY_SLOW_DECAY_PCT
    deltas = jnp.linspace(min_decay, max_decay, D_MODEL, dtype=f32)
    decay = jnp.exp(-t * jnp.abs(deltas))
    h = h * decay[:, None, None, :]
    h_fwd, h_bwd = h[:, :, 0], h[:, :, 1]
    k = jnp.concatenate([h_fwd, jnp.zeros_like(h_fwd[:1]), h_bwd[:0:-1]], axis=0)
    k = k * lax.rsqrt(jnp.sum(k * k, axis=0, keepdims=True) + EPS)
    return jnp.fft.rfft(k, axis=0)


def long_conv(u, k_f, bias):
    L = u.shape[1]
    uf = u.astype(jnp.float32)
    y = jnp.fft.irfft(jnp.fft.rfft(uf, n=2 * L, axis=1) * k_f, n=2 * L, axis=1)[:, :L]
    return (y + uf * bias.astype(jnp.float32)).astype(u.dtype)


def hyena_mixer(h, w_in, conv_w, conv_b, filt, bias, w_out):
    L = h.shape[1]
    u = short_conv(h @ w_in, conv_w, conv_b)
    v, x1, x2 = jnp.split(u, 3, axis=-1)
    k_f = hyena_filters(L, *filt)
    z = x1 * long_conv(v, k_f[:, 0], bias[0])
    z = x2 * long_conv(z, k_f[:, 1], bias[1])
    return z @ w_out


def axial_rope_tables(L):
    rows = L // GRID_W
    r = jnp.repeat(jnp.arange(rows, dtype=jnp.float32), GRID_W)
    col = jnp.tile(jnp.arange(GRID_W, dtype=jnp.float32), rows)
    inv = ROPE_BASE ** (-(2.0 * jnp.arange(ROPE_PAIRS, dtype=jnp.float32)) / (2 * ROPE_PAIRS))
    ang = jnp.stack([r[:, None] * inv, col[:, None] * inv], axis=1)
    return jnp.cos(ang), jnp.sin(ang)


def apply_axial_rope(x, cos, sin):
    shp = (cos.shape[0],) + (1,) * (x.ndim - 3) + cos.shape[1:]
    cos = cos.reshape(shp)
    sin = sin.reshape(shp)
    xr = x.astype(jnp.float32).reshape(x.shape[:-1] + (2, 2, ROPE_PAIRS))
    x1, x2 = xr[..., 0, :], xr[..., 1, :]
    out = jnp.stack([x1 * cos - x2 * sin, x2 * cos + x1 * sin], axis=-2)
    return out.reshape(x.shape).astype(x.dtype)


def split_qkv(qkv):
    B, L = qkv.shape[:2]
    q, k, v = jnp.split(qkv, [D_Q, D_Q + D_KV], axis=-1)
    return (q.reshape(B, L, N_KV_HEADS, Q_PER_KV, HEAD_DIM),
            k.reshape(B, L, N_KV_HEADS, HEAD_DIM),
            v.reshape(B, L, N_KV_HEADS, HEAD_DIM))


def sink_attention(q, k, v, sink, mask):
    s = jnp.einsum('bqkgd,bskd->bkgqs', q, k).astype(jnp.float32) * (HEAD_DIM ** -0.5)
    s = jnp.where(mask, s, -jnp.inf)
    sink_col = jnp.broadcast_to(sink.astype(jnp.float32)[None, :, :, None, None], s.shape[:-1] + (1,))
    p = jax.nn.softmax(jnp.concatenate([s, sink_col], axis=-1), axis=-1)[..., :-1]
    return jnp.einsum('bkgqs,bskd->bqkgd', p.astype(v.dtype), v)


def attention_mixer(h_lat, h_ctx, w_qkv, sink, w_o, with_ctx_out):
    B, L, _ = h_lat.shape
    sink = sink.reshape(N_KV_HEADS, Q_PER_KV)
    q, k, v = split_qkv(h_lat @ w_qkv)
    q_c, k_c, v_c = split_qkv(h_ctx @ w_qkv)
    ctx_len = k_c.shape[1]
    cos, sin = axial_rope_tables(L)
    q = apply_axial_rope(q, cos, sin)
    k = apply_axial_rope(k, cos, sin)
    kp = jnp.pad(k, ((0, 0), (WINDOW, WINDOW), (0, 0), (0, 0)))
    vp = jnp.pad(v, ((0, 0), (WINDOW, WINDOW), (0, 0), (0, 0)))
    span = Q_BLOCK + 2 * WINDOW
    ctx_mask = jnp.ones((Q_BLOCK, ctx_len), bool)

    def block(bi):
        start = bi * Q_BLOCK
        qb = lax.dynamic_slice_in_dim(q, start, Q_BLOCK, axis=1)
        kb = jnp.concatenate([lax.dynamic_slice_in_dim(kp, start, span, axis=1), k_c], axis=1)
        vb = jnp.concatenate([lax.dynamic_slice_in_dim(vp, start, span, axis=1), v_c], axis=1)
        qpos = start + jnp.arange(Q_BLOCK)
        kpos = start - WINDOW + jnp.arange(span)
        band = (jnp.abs(qpos[:, None] - kpos[None, :]) <= WINDOW) & (kpos >= 0) & (kpos < L)
        return sink_attention(qb, kb, vb, sink, jnp.concatenate([band, ctx_mask], axis=1))

    o = lax.map(block, jnp.arange(L // Q_BLOCK))
    y_lat = jnp.moveaxis(o, 0, 1).reshape(B, L, D_Q) @ w_o
    if not with_ctx_out:
        return y_lat, None
    o_c = sink_attention(q_c, k_c, v_c, sink, jnp.ones((ctx_len, ctx_len), bool))
    return y_lat, o_c.reshape(B, ctx_len, D_Q) @ w_o


def expert_dispatch(h, expert_id, gates, w1, w3, w2):
    n_tok, d = h.shape
    n_assign = n_tok * TOP_K
    flat_e = expert_id.reshape(n_assign).astype(jnp.int32)
    flat_t = jnp.repeat(jnp.arange(n_tok, dtype=jnp.int32), TOP_K)
    flat_g = gates.reshape(n_assign)
    order = jnp.argsort(flat_e)
    se = flat_e[order]
    counts = jnp.bincount(flat_e, length=N_EXPERTS)
    padded = (counts + MOE_BLOCK - 1) // MOE_BLOCK * MOE_BLOCK
    pad_end = jnp.cumsum(padded)
    pad_start = pad_end - padded
    start = jnp.cumsum(counts) - counts
    dest = pad_start[se] + jnp.arange(n_assign, dtype=jnp.int32) - start[se]
    n_blocks = -(-n_assign // MOE_BLOCK) + N_EXPERTS
    slot_tok = jnp.full((n_blocks * MOE_BLOCK,), n_tok, jnp.int32).at[dest].set(flat_t[order])
    slot_gate = jnp.zeros((n_blocks * MOE_BLOCK,), h.dtype).at[dest].set(flat_g[order])
    block_start = jnp.arange(n_blocks, dtype=jnp.int32) * MOE_BLOCK
    block_exp = jnp.clip(jnp.searchsorted(pad_end, block_start, side='right'), 0, N_EXPERTS - 1)
    h_pad = jnp.concatenate([h, jnp.zeros((1, d), h.dtype)], axis=0)

    def run(args):
        tok, e = args
        xb = h_pad[tok]
        return (jax.nn.silu(xb @ w1[e]) * (xb @ w3[e])) @ w2[e]

    y = lax.map(run, (slot_tok.reshape(n_blocks, MOE_BLOCK), block_exp))
    y = y.reshape(n_blocks * MOE_BLOCK, d) * slot_gate[:, None]
    return jax.ops.segment_sum(y, slot_tok, num_segments=n_tok + 1)[:n_tok]


def hier_moe(h, wg, bg, we, be, w1, w3, w2):
    n_tok = h.shape[0]
    f32 = jnp.float32
    hf = h.astype(f32)
    p_group = jax.nn.softmax(hf @ wg.astype(f32) + bg.astype(f32), axis=-1)
    p_g, g_sel = lax.top_k(p_group, 1)
    fine = (hf @ we.astype(f32) + be.astype(f32)).reshape(n_tok, N_GROUPS, EXPERTS_PER_GROUP)
    fine = fine[jnp.arange(n_tok), g_sel[:, 0]]
    p_e = jax.nn.softmax(fine, axis=-1)
    top_p, top_i = lax.top_k(p_e, TOP_K)
    gates = p_g * top_p / jnp.sum(top_p, axis=-1, keepdims=True)
    expert_id = g_sel * EXPERTS_PER_GROUP + top_i
    return expert_dispatch(h, expert_id, gates.astype(h.dtype), w1, w3, w2)


def setup_inputs(seed: int = 0) -> dict:
    key = jax.random.key(seed)
    ks = jax.random.split(key, 32)
    f32 = jnp.float32
    D = D_MODEL

    def nrm(k, shape, std):
        return jax.random.normal(k, shape, f32) * std

    return {
        'x': nrm(ks[0], (BATCH, SEQ, D), 1.0),
        'c': nrm(ks[1], (BATCH, D), 1.0),
        'ctx': nrm(ks[2], (BATCH, CTX_LEN, D), 1.0),
        'c_ctx': nrm(ks[3], (D,), 1.0),
        'mod_w': nrm(ks[4], (DEPTH, D, 6 * D), D ** -0.5),
        'mod_b': nrm(ks[5], (DEPTH, 6 * D), 0.02),
        'hy_w_in': nrm(ks[6], (N_HYENA_LAYERS, D, 3 * D), D ** -0.5),
        'hy_conv_w': nrm(ks[7], (N_HYENA_LAYERS, 3, 3 * D), 3 ** -0.5),
        'hy_conv_b': nrm(ks[8], (N_HYENA_LAYERS, 3 * D), 0.02),
        'hy_f_w1': nrm(ks[9], (N_HYENA_LAYERS, HY_EMB, HY_FILTER_W), HY_EMB ** -0.5),
        'hy_f_b1': nrm(ks[10], (N_HYENA_LAYERS, HY_FILTER_W), 0.02),
        'hy_f_freq1': 1.0 + nrm(ks[11], (N_HYENA_LAYERS, HY_FILTER_W), 0.1),
        'hy_f_w2': nrm(ks[12], (N_HYENA_LAYERS, HY_FILTER_W, HY_FILTER_W), HY_FILTER_W ** -0.5),
        'hy_f_b2': nrm(ks[13], (N_HYENA_LAYERS, HY_FILTER_W), 0.02),
        'hy_f_freq2': 1.0 + nrm(ks[14], (N_HYENA_LAYERS, HY_FILTER_W), 0.1),
        'hy_f_w3': nrm(ks[15], (N_HYENA_LAYERS, HY_FILTER_W, HY_ORDER * 2 * D), HY_FILTER_W ** -0.5),
        'hy_bias': nrm(ks[16], (N_HYENA_LAYERS, HY_ORDER, D), 0.1),
        'hy_w_out': nrm(ks[17], (N_HYENA_LAYERS, D, D), D ** -0.5),
        'at_w_qkv': nrm(ks[18], (N_ATTN_LAYERS, D, D_Q + 2 * D_KV), D ** -0.5),
        'at_sink': nrm(ks[19], (N_ATTN_LAYERS, N_Q_HEADS), 0.5),
        'at_w_o': nrm(ks[20], (N_ATTN_LAYERS, D_Q, D), D_Q ** -0.5),
        'moe_wg': nrm(ks[21], (DEPTH, D, N_GROUPS), D ** -0.5),
        'moe_bg': nrm(ks[22], (DEPTH, N_GROUPS), 0.01),
        'moe_we': nrm(ks[23], (DEPTH, D, N_EXPERTS), D ** -0.5),
        'moe_be': nrm(ks[24], (DEPTH, N_EXPERTS), 0.01),
        'moe_w1': nrm(ks[25], (DEPTH, N_EXPERTS, D, D_EXPERT), D ** -0.5),
        'moe_w3': nrm(ks[26], (DEPTH, N_EXPERTS, D, D_EXPERT), D ** -0.5),
        'moe_w2': nrm(ks[27], (DEPTH, N_EXPERTS, D_EXPERT, D), D_EXPERT ** -0.5),
        'final_gain': 1.0 + nrm(ks[28], (D,), 0.02),
    }


def reference(x, c, ctx, c_ctx, mod_w, mod_b, hy_w_in, hy_conv_w, hy_conv_b, hy_f_w1, hy_f_b1,
              hy_f_freq1, hy_f_w2, hy_f_b2, hy_f_freq2, hy_f_w3, hy_bias, hy_w_out,
              at_w_qkv, at_sink, at_w_o, moe_wg, moe_bg, moe_we, moe_be, moe_w1, moe_w3, moe_w2,
              final_gain):
    B, L, D = x.shape
    xc = ctx
    for i in range(DEPTH):
        last = i == DEPTH - 1
        j = i // N_MIXERS
        sh1, sc1, g1, sh2, sc2, g2 = ada_modulation(c[:, None, :], mod_w[i], mod_b[i])
        sh1c, sc1c, g1c, sh2c, sc2c, g2c = ada_modulation(c_ctx, mod_w[i], mod_b[i])
        h_lat = modulate(x, sh1, sc1)
        y_ctx = None
        if i % N_MIXERS == 0:
            filt = (hy_f_w1[j], hy_f_b1[j], hy_f_freq1[j], hy_f_w2[j], hy_f_b2[j], hy_f_freq2[j], hy_f_w3[j])
            y_lat = hyena_mixer(h_lat, hy_w_in[j], hy_conv_w[j], hy_conv_b[j], filt, hy_bias[j], hy_w_out[j])
            if not last:
                y_ctx = hyena_mixer(modulate(xc, sh1c, sc1c), hy_w_in[j], hy_conv_w[j], hy_conv_b[j],
                                    filt, hy_bias[j], hy_w_out[j])
        else:
            y_lat, y_ctx = attention_mixer(h_lat, modulate(xc, sh1c, sc1c), at_w_qkv[j], at_sink[j],
                                           at_w_o[j], not last)
        x = x + g1 * y_lat
        moe_args = (moe_wg[i], moe_bg[i], moe_we[i], moe_be[i], moe_w1[i], moe_w3[i], moe_w2[i])
        if last:
            y = hier_moe(modulate(x, sh2, sc2).reshape(B * L, D), *moe_args)
            x = x + g2 * y.reshape(B, L, D)
        else:
            xc = xc + g1c * y_ctx
            tokens = jnp.concatenate([modulate(x, sh2, sc2).reshape(B * L, D),
                                      modulate(xc, sh2c, sc2c).reshape(-1, D)], axis=0)
            y = hier_moe(tokens, *moe_args)
            x = x + g2 * y[:B * L].reshape(B, L, D)
            xc = xc + g2c * y[B * L:].reshape(xc.shape)
    return rms_norm(x, final_gain)
```

```python
import functools
import math

import jax
import jax.numpy as jnp
from jax import lax
from jax.experimental import pallas as pl
from jax.experimental.pallas import tpu as pltpu

F32 = jnp.float32
BF16 = jnp.bfloat16
HIGHEST = lax.Precision.HIGHEST

EPS = 1e-6
GRID_W = 64
HEAD_DIM = 64
N_KV_HEADS = 4
WINDOW = 128
Q_BLOCK = 128
ROPE_PAIRS = HEAD_DIM // 4
ROPE_BASE = 10000.0
N_GROUPS = 8
EXPERTS_PER_GROUP = 8
N_EXPERTS = N_GROUPS * EXPERTS_PER_GROUP
MOE_BLOCK = 256
HY_EMB = 33
HY_BANDS = (HY_EMB - 1) // 2
HY_FAST_DECAY_PCT = 0.3
HY_SLOW_DECAY_PCT = 1.5
HY_DECAY_TARGET = 1e-2

LANES = 128
ROUTE_W = LANES
VMEM_LIMIT = 56 * 1024 * 1024


def _tile(n, pref):
    if n <= pref:
        return n
    t = pref
    while n % t:
        t -= 1
    return t


def _params(*sem):
    return pltpu.CompilerParams(dimension_semantics=sem, vmem_limit_bytes=VMEM_LIMIT)


def _modulated_norm(x, sh, sc):
    y = x * lax.rsqrt(jnp.mean(x * x, axis=-1, keepdims=True) + EPS)
    return y * (1.0 + sc) + sh


def _adaln_kernel(c_ref, w_ref, b_ref, o_ref):
    c = c_ref[...]
    a = c * jax.nn.sigmoid(c)
    o_ref[0] = jnp.dot(a, w_ref[0], precision=HIGHEST, preferred_element_type=F32) + b_ref[0]


def adaln(cond, mod_w, mod_b):
    depth, d, n = mod_w.shape
    r = cond.shape[0]
    tn = _tile(n, 1024)
    return pl.pallas_call(
        _adaln_kernel,
        out_shape=jax.ShapeDtypeStruct((depth, r, n), F32),
        grid=(depth, n // tn),
        in_specs=[pl.BlockSpec((r, d), lambda l, j: (0, 0)),
                  pl.BlockSpec((1, d, tn), lambda l, j: (l, 0, j)),
                  pl.BlockSpec((1, 1, tn), lambda l, j: (l, 0, j))],
        out_specs=pl.BlockSpec((1, r, tn), lambda l, j: (l, 0, j)),
        compiler_params=_params("arbitrary", "arbitrary"),
        name="adaln",
    )(cond, mod_w, mod_b.reshape(depth, 1, n))


def _modmm_kernel(x_ref, sh_ref, sc_ref, w_ref, o_ref, h_ref):
    @pl.when(pl.program_id(1) == 0)
    def _():
        h_ref[...] = _modulated_norm(x_ref[...], sh_ref[0], sc_ref[0]).astype(BF16)

    o_ref[...] = jnp.dot(h_ref[...], w_ref[...], preferred_element_type=F32).astype(o_ref.dtype)


def modmm(x, sh, sc, w, rows_per_seg):
    t, d = x.shape
    n = w.shape[1]
    tm = _tile(rows_per_seg, 1024)
    tn = _tile(n, 512)
    seg = lambda i, j: ((i * tm) // rows_per_seg, 0, 0)
    return pl.pallas_call(
        _modmm_kernel,
        out_shape=jax.ShapeDtypeStruct((t, n), BF16),
        grid=(t // tm, n // tn),
        in_specs=[pl.BlockSpec((tm, d), lambda i, j: (i, 0)),
                  pl.BlockSpec((1, 1, d), seg),
                  pl.BlockSpec((1, 1, d), seg),
                  pl.BlockSpec((d, tn), lambda i, j: (0, j))],
        out_specs=pl.BlockSpec((tm, tn), lambda i, j: (i, j)),
        scratch_shapes=[pltpu.VMEM((tm, d), BF16)],
        compiler_params=_params("arbitrary", "arbitrary"),
        name="modmm",
    )(x, sh, sc, w)


def _mmres_kernel(a_ref, w_ref, r_ref, g_ref, o_ref):
    acc = jnp.dot(a_ref[...], w_ref[...], preferred_element_type=F32)
    o_ref[...] = r_ref[...] + g_ref[0] * acc


def mm_res(a, w, res, gate, rows_per_seg):
    t, k = a.shape
    n = w.shape[1]
    tm = _tile(rows_per_seg, 1024)
    tn = _tile(n, 512)
    return pl.pallas_call(
        _mmres_kernel,
        out_shape=jax.ShapeDtypeStruct((t, n), F32),
        grid=(t // tm, n // tn),
        in_specs=[pl.BlockSpec((tm, k), lambda i, j: (i, 0)),
                  pl.BlockSpec((k, tn), lambda i, j: (0, j)),
                  pl.BlockSpec((tm, tn), lambda i, j: (i, j)),
                  pl.BlockSpec((1, 1, tn), lambda i, j: ((i * tm) // rows_per_seg, 0, j))],
        out_specs=pl.BlockSpec((tm, tn), lambda i, j: (i, j)),
        compiler_params=_params("arbitrary", "arbitrary"),
        name="mm_res",
    )(a, w, res, gate)


def _sconv_kernel(u_ref, w_ref, b_ref, o_ref):
    x = u_ref[...].astype(F32)
    n = x.shape[0]
    row = lax.broadcasted_iota(jnp.int32, x.shape, 0)
    prev = jnp.where(row == 0, 0.0, pltpu.roll(x, 1, 0))
    nxt = jnp.where(row == n - 1, 0.0, pltpu.roll(x, n - 1, 0))
    y = prev * w_ref[0:1, :] + x * w_ref[1:2, :] + nxt * w_ref[2:3, :] + b_ref[...]
    o_ref[...] = y.astype(o_ref.dtype)


def short_conv(u, w, b, seq_len):
    t, c = u.shape
    tc = _tile(c, 256)
    return pl.pallas_call(
        _sconv_kernel,
        out_shape=jax.ShapeDtypeStruct((t, c), BF16),
        grid=(t // seq_len, c // tc),
        in_specs=[pl.BlockSpec((seq_len, tc), lambda s, j: (s, j)),
                  pl.BlockSpec((3, tc), lambda s, j: (0, j)),
                  pl.BlockSpec((1, tc), lambda s, j: (0, j))],
        out_specs=pl.BlockSpec((seq_len, tc), lambda s, j: (s, j)),
        compiler_params=_params("arbitrary", "arbitrary"),
        name="short_conv",
    )(u, w, b.reshape(1, c))


def _filter_kernel(feat_ref, w1_ref, b1_ref, f1_ref, w2_ref, b2_ref, f2_ref, w3f_ref, w3b_ref,
                   dl_ref, e_ref, o_ref):
    feats = feat_ref[...]
    n = feats.shape[0]
    h = jnp.sin(f1_ref[...] * (jnp.dot(feats, w1_ref[...], precision=HIGHEST,
                                       preferred_element_type=F32) + b1_ref[...]))
    h = jnp.sin(f2_ref[...] * (jnp.dot(h, w2_ref[...], precision=HIGHEST,
                                       preferred_element_type=F32) + b2_ref[...]))
    t = feats[:, 0:1]
    decay = jnp.exp(-t * jnp.abs(dl_ref[...]))
    hf = jnp.dot(h, w3f_ref[...], precision=HIGHEST, preferred_element_type=F32) * decay
    hb = jnp.dot(h, w3b_ref[...], precision=HIGHEST, preferred_element_type=F32) * decay
    row = lax.broadcasted_iota(jnp.int32, hf.shape, 0)
    hb = jnp.where(row == 0, 0.0, hb)
    s = lax.rsqrt(jnp.sum(hf * hf + hb * hb, axis=0, keepdims=True) + EPS)
    e_ref[0] = ((hf + hb) * s).astype(e_ref.dtype)
    o_ref[0] = ((hf - hb) * s).astype(o_ref.dtype)


def hyena_filter_taps(seq_len, w1, b1, f1, w2, b2, f2, w3, d):
    emb_pad = LANES
    t = jnp.linspace(0.0, 1.0, seq_len, dtype=F32)[:, None]
    omega = (2.0 * math.pi / seq_len) * jnp.arange(seq_len, dtype=F32)[:, None]
    bands = jnp.linspace(1e-4, HY_BANDS - 1, HY_BANDS, dtype=F32)[None, :]
    feats = jnp.concatenate([t, jnp.cos(bands * omega), -jnp.sin(bands * omega),
                             jnp.zeros((seq_len, emb_pad - HY_EMB), F32)], axis=-1)
    w1p = jnp.concatenate([w1, jnp.zeros((emb_pad - HY_EMB, w1.shape[1]), F32)], axis=0)
    max_decay = math.log(HY_DECAY_TARGET) / HY_FAST_DECAY_PCT
    min_decay = math.log(HY_DECAY_TARGET) / HY_SLOW_DECAY_PCT
    deltas = jnp.linspace(min_decay, max_decay, d, dtype=F32).reshape(1, d)
    fw = w1.shape[1]
    tn = _tile(d, 256)
    nb = d // tn
    full = lambda shape: pl.BlockSpec(shape, lambda o, j: (0,) * len(shape))
    out = jax.ShapeDtypeStruct((2, seq_len, d), BF16)
    return pl.pallas_call(
        _filter_kernel,
        out_shape=(out, out),
        grid=(2, nb),
        in_specs=[full((seq_len, emb_pad)), full((emb_pad, fw)), full((1, fw)), full((1, fw)),
                  full((fw, fw)), full((1, fw)), full((1, fw)),
                  pl.BlockSpec((fw, tn), lambda o, j: (0, (2 * o) * nb + j)),
                  pl.BlockSpec((fw, tn), lambda o, j: (0, (2 * o + 1) * nb + j)),
                  pl.BlockSpec((1, tn), lambda o, j: (0, j))],
        out_specs=(pl.BlockSpec((1, seq_len, tn), lambda o, j: (o, 0, j)),
                   pl.BlockSpec((1, seq_len, tn), lambda o, j: (o, 0, j))),
        compiler_params=_params("arbitrary", "arbitrary"),
        name="hyena_filter",
    )(feats, w1p, b1.reshape(1, fw), f1.reshape(1, fw), w2, b2.reshape(1, fw), f2.reshape(1, fw),
      w3, w3, deltas)


def _alt_sum(x):
    row = lax.broadcasted_iota(jnp.int32, x.shape, 0)
    return jnp.sum(jnp.where((row & 1) == 1, -x, x), axis=0, keepdims=True)


def _spectrum_kernel(c_ref, s_ref, e_ref, o_ref, kr_ref, km_ref):
    kr_ref[0] = jnp.dot(c_ref[...], e_ref[0], preferred_element_type=F32)
    km = jnp.dot(s_ref[...], o_ref[0], preferred_element_type=F32)

    @pl.when(pl.program_id(2) != 0)
    def _():
        km_ref[0] = km

    @pl.when(pl.program_id(2) == 0)
    def _():
        row = lax.broadcasted_iota(jnp.int32, km.shape, 0)
        km_ref[0] = jnp.where(row == 0, _alt_sum(e_ref[0].astype(F32)), km)


def filter_spectrum(cmat, smat, e_taps, o_taps):
    _, n, d = e_taps.shape
    tk = _tile(n, 512)
    tn = _tile(d, 512)
    out = jax.ShapeDtypeStruct((2, n, d), F32)
    return pl.pallas_call(
        _spectrum_kernel,
        out_shape=(out, out),
        grid=(2, d // tn, n // tk),
        in_specs=[pl.BlockSpec((tk, n), lambda o, j, k: (k, 0)),
                  pl.BlockSpec((tk, n), lambda o, j, k: (k, 0)),
                  pl.BlockSpec((1, n, tn), lambda o, j, k: (o, 0, j)),
                  pl.BlockSpec((1, n, tn), lambda o, j, k: (o, 0, j))],
        out_specs=(pl.BlockSpec((1, tk, tn), lambda o, j, k: (o, k, j)),
                   pl.BlockSpec((1, tk, tn), lambda o, j, k: (o, k, j))),
        compiler_params=_params("arbitrary", "arbitrary", "arbitrary"),
        name="filter_spectrum",
    )(cmat, smat, e_taps, o_taps)


def _dft_fwd_kernel(c_ref, s_ref, u_ref, kr_ref, km_ref, yr_ref, q_ref):
    u = u_ref[...]
    a = jnp.dot(c_ref[...], u, preferred_element_type=F32)
    b = jnp.dot(s_ref[...], u, preferred_element_type=F32)
    kr = kr_ref[0]
    km = km_ref[0]
    yr = a * kr - b * km
    q = b * kr + a * km

    @pl.when(pl.program_id(2) != 0)
    def _():
        yr_ref[...] = yr.astype(yr_ref.dtype)
        q_ref[...] = q.astype(q_ref.dtype)

    @pl.when(pl.program_id(2) == 0)
    def _():
        row = lax.broadcasted_iota(jnp.int32, yr.shape, 0)
        ynyq = _alt_sum(u.astype(F32)) * km[0:1, :]
        yr_ref[...] = jnp.where(row == 0, 0.5 * yr, yr).astype(yr_ref.dtype)
        q_ref[...] = jnp.where(row == 0, 0.5 * ynyq, q).astype(q_ref.dtype)


def dft_fwd(cmat, smat, u, col0, kr, km, order, seq_len, d):
    t = u.shape[0]
    tk = _tile(seq_len, 512)
    tn = _tile(d, 512)
    cb = col0 // tn
    out = jax.ShapeDtypeStruct((t, d), BF16)
    nk = seq_len // tk
    return pl.pallas_call(
        _dft_fwd_kernel,
        out_shape=(out, out),
        grid=(t // seq_len, d // tn, nk),
        in_specs=[pl.BlockSpec((tk, seq_len), lambda b, j, k: (k, 0)),
                  pl.BlockSpec((tk, seq_len), lambda b, j, k: (k, 0)),
                  pl.BlockSpec((seq_len, tn), lambda b, j, k: (b, cb + j)),
                  pl.BlockSpec((1, tk, tn), lambda b, j, k: (order, k, j)),
                  pl.BlockSpec((1, tk, tn), lambda b, j, k: (order, k, j))],
        out_specs=(pl.BlockSpec((tk, tn), lambda b, j, k: (b * nk + k, j)),
                   pl.BlockSpec((tk, tn), lambda b, j, k: (b * nk + k, j))),
        compiler_params=_params("arbitrary", "arbitrary", "arbitrary"),
        name="dft_fwd",
    )(cmat, smat, u, kr, km)


def _dft_inv_kernel(c_ref, s_ref, yr_ref, q_ref, u_ref, g_ref, bias_ref, o_ref, *, seq_len):
    acc = jnp.dot(c_ref[...], yr_ref[...], preferred_element_type=F32)
    acc += jnp.dot(s_ref[...], q_ref[...], preferred_element_type=F32)
    tt = acc.shape[0]
    t_idx = pl.program_id(2) * tt + lax.broadcasted_iota(jnp.int32, acc.shape, 0)
    half_nyq = q_ref[0:1, :].astype(F32)
    acc += jnp.where((t_idx & 1) == 1, -half_nyq, half_nyq)
    u = u_ref[...].astype(F32)
    y = acc * (1.0 / seq_len) + u * bias_ref[0]
    o_ref[...] = (g_ref[...].astype(F32) * y).astype(o_ref.dtype)


def dft_inv(cmat, smat, yr, q, u, ucol0, gsrc, gcol0, bias, order, seq_len, d):
    t = yr.shape[0]
    tt = _tile(seq_len, 512)
    tn = _tile(d, 512)
    ub, gb = ucol0 // tn, gcol0 // tn
    nt = seq_len // tt
    return pl.pallas_call(
        functools.partial(_dft_inv_kernel, seq_len=seq_len),
        out_shape=jax.ShapeDtypeStruct((t, d), BF16),
        grid=(t // seq_len, d // tn, nt),
        in_specs=[pl.BlockSpec((tt, seq_len), lambda b, j, k: (k, 0)),
                  pl.BlockSpec((tt, seq_len), lambda b, j, k: (k, 0)),
                  pl.BlockSpec((seq_len, tn), lambda b, j, k: (b, j)),
                  pl.BlockSpec((seq_len, tn), lambda b, j, k: (b, j)),
                  pl.BlockSpec((tt, tn), lambda b, j, k: (b * nt + k, ub + j)),
                  pl.BlockSpec((tt, tn), lambda b, j, k: (b * nt + k, gb + j)),
                  pl.BlockSpec((1, 1, tn), lambda b, j, k: (order, 0, j))],
        out_specs=pl.BlockSpec((tt, tn), lambda b, j, k: (b * nt + k, j)),
        compiler_params=_params("arbitrary", "arbitrary", "arbitrary"),
        name="dft_inv",
    )(cmat, smat, yr, q, u, gsrc, bias)


def dft_tables(seq_len):
    n = jnp.arange(seq_len, dtype=jnp.int32)
    ang = ((n[:, None] * n[None, :]) % (2 * seq_len)).astype(F32) * (math.pi / seq_len)
    return jnp.cos(ang).astype(BF16), jnp.sin(ang).astype(BF16)


def hyena_long_convs(u3, tables, spectrum, bias, seq_len, d):
    cmat, smat = tables
    kr, km = spectrum
    yr, q = dft_fwd(cmat, smat, u3, 0, kr, km, 0, seq_len, d)
    z = dft_inv(cmat, smat, yr, q, u3, 0, u3, d, bias, 0, seq_len, d)
    yr, q = dft_fwd(cmat, smat, z, 0, kr, km, 1, seq_len, d)
    return dft_inv(cmat, smat, yr, q, z, 0, u3, 2 * d, bias, 1, seq_len, d)


def _rope_kernel(x_ref, cos_ref, sa_ref, sb_ref, o_ref, *, n_q_blocks, scale):
    j = pl.program_id(1)
    c, sa, sb = cos_ref[...], sa_ref[...], sb_ref[...]
    sc = jnp.where(j < n_q_blocks, scale, 1.0).astype(F32)
    for h in range(x_ref.shape[1] // LANES):
        x = x_ref[:, h * LANES:(h + 1) * LANES].astype(F32)
        y = x * c + pltpu.roll(x, LANES - ROPE_PAIRS, 1) * sa + pltpu.roll(x, ROPE_PAIRS, 1) * sb
        o_ref[:, h * LANES:(h + 1) * LANES] = (y * sc).astype(o_ref.dtype)


def rope_tables(seq_len):
    pos = jnp.arange(seq_len, dtype=jnp.int32)
    r = (pos // GRID_W).astype(F32)[:, None]
    col = (pos % GRID_W).astype(F32)[:, None]
    inv = ROPE_BASE ** (-(2.0 * jnp.arange(ROPE_PAIRS, dtype=F32)) / (2 * ROPE_PAIRS))
    ar, ac = r * inv, col * inv
    zeros = jnp.zeros_like(ar)
    cos = jnp.concatenate([jnp.cos(ar), jnp.cos(ar), jnp.cos(ac), jnp.cos(ac)], axis=1)
    sa = jnp.concatenate([-jnp.sin(ar), zeros, -jnp.sin(ac), zeros], axis=1)
    sb = jnp.concatenate([zeros, jnp.sin(ar), zeros, jnp.sin(ac)], axis=1)
    rep = LANES // HEAD_DIM
    return tuple(jnp.tile(a, (1, rep)) for a in (cos, sa, sb))


def rope_qk(qkv, tables, seq_len, d_q, d_kv):
    t, n = qkv.shape
    tr = _tile(seq_len, 512)
    tc = 2 * LANES
    nb = seq_len // tr
    tab = pl.BlockSpec((tr, LANES), lambda i, j: (i % nb, 0))
    return pl.pallas_call(
        functools.partial(_rope_kernel, n_q_blocks=d_q // tc, scale=HEAD_DIM ** -0.5),
        out_shape=jax.ShapeDtypeStruct((t, n), qkv.dtype),
        grid=(t // tr, (d_q + d_kv) // tc),
        in_specs=[pl.BlockSpec((tr, tc), lambda i, j: (i, j)), tab, tab, tab],
        out_specs=pl.BlockSpec((tr, tc), lambda i, j: (i, j)),
        input_output_aliases={0: 0},
        compiler_params=_params("arbitrary", "arbitrary"),
        name="rope",
    )(qkv, *tables)


def _attend(q_ref, sink_ref, keys, vals, biases, o_ref, n_heads, q_per_kv, q_scale=None):
    outs = []
    for h in range(n_heads):
        g = h // q_per_kv
        q = q_ref[:, h * HEAD_DIM:(h + 1) * HEAD_DIM]
        if q_scale is not None:
            q = q * q_scale
        sink = sink_ref[h]
        scores = []
        for ks, bias in zip(keys, biases):
            s = lax.dot_general(q, ks[:, g * HEAD_DIM:(g + 1) * HEAD_DIM],
                                (((1,), (1,)), ((), ())), preferred_element_type=F32)
            scores.append(s if bias is None else s + bias)
        m = functools.reduce(jnp.maximum, [jnp.max(s, axis=-1, keepdims=True) for s in scores])
        m = jnp.maximum(m, sink)
        denom = jnp.exp(sink - m)
        acc = None
        for s, vs in zip(scores, vals):
            p = jnp.exp(s - m)
            denom = denom + jnp.sum(p, axis=-1, keepdims=True)
            pv = jnp.dot(p.astype(BF16), vs[:, g * HEAD_DIM:(g + 1) * HEAD_DIM],
                         preferred_element_type=F32)
            acc = pv if acc is None else acc + pv
        outs.append(acc / denom)
        if len(outs) == LANES // HEAD_DIM:
            lo = (h + 1) * HEAD_DIM - LANES
            o_ref[:, lo:lo + LANES] = jnp.concatenate(outs, axis=1).astype(o_ref.dtype)
            outs = []


def _attn_kernel(sink_ref, q_ref, k_ref, v_ref, kc_ref, vc_ref, o_ref, *, seq_len, n_heads, q_per_kv):
    span = Q_BLOCK + 2 * WINDOW
    start = pl.program_id(1) * Q_BLOCK
    kstart = pl.multiple_of(jnp.clip(start - WINDOW, 0, seq_len - span), Q_BLOCK)
    qpos = start + lax.broadcasted_iota(jnp.int32, (Q_BLOCK, span), 0)
    kpos = kstart + lax.broadcasted_iota(jnp.int32, (Q_BLOCK, span), 1)
    bias = jnp.where(jnp.abs(qpos - kpos) <= WINDOW, 0.0, -1e30).astype(F32)
    kw = k_ref[pl.ds(kstart, span), :]
    vw = v_ref[pl.ds(kstart, span), :]
    _attend(q_ref, sink_ref, [kw, kc_ref[...]], [vw, vc_ref[...]], [bias, None], o_ref,
            n_heads, q_per_kv)


def window_attention(qkv, qkv_ctx, sink, batch, seq_len, ctx_len, d_q, d_kv):
    n_heads = d_q // HEAD_DIM
    nqb = seq_len // Q_BLOCK
    kcol, vcol = d_q // d_kv, d_q // d_kv + 1
    return pl.pallas_call(
        functools.partial(_attn_kernel, seq_len=seq_len, n_heads=n_heads,
                          q_per_kv=n_heads // N_KV_HEADS),
        out_shape=jax.ShapeDtypeStruct((batch * seq_len, d_q), BF16),
        grid=(batch, nqb),
        in_specs=[pl.BlockSpec(memory_space=pltpu.SMEM),
                  pl.BlockSpec((Q_BLOCK, d_q), lambda b, i: (b * nqb + i, 0)),
                  pl.BlockSpec((seq_len, d_kv), lambda b, i: (b, kcol)),
                  pl.BlockSpec((seq_len, d_kv), lambda b, i: (b, vcol)),
                  pl.BlockSpec((ctx_len, d_kv), lambda b, i: (b, kcol)),
                  pl.BlockSpec((ctx_len, d_kv), lambda b, i: (b, vcol))],
        out_specs=pl.BlockSpec((Q_BLOCK, d_q), lambda b, i: (b * nqb + i, 0)),
        compiler_params=_params("arbitrary", "arbitrary"),
        name="window_attention",
    )(sink, qkv, qkv, qkv, qkv_ctx, qkv_ctx)


def _ctx_attn_kernel(sink_ref, q_ref, kc_ref, vc_ref, o_ref, *, n_heads, q_per_kv, scale):
    _attend(q_ref, sink_ref, [kc_ref[...]], [vc_ref[...]], [None], o_ref, n_heads, q_per_kv,
            q_scale=scale)


def ctx_attention(qkv_ctx, sink, batch, ctx_len, d_q, d_kv):
    n_heads = d_q // HEAD_DIM
    kcol, vcol = d_q // d_kv, d_q // d_kv + 1
    return pl.pallas_call(
        functools.partial(_ctx_attn_kernel, n_heads=n_heads, q_per_kv=n_heads // N_KV_HEADS,
                          scale=HEAD_DIM ** -0.5),
        out_shape=jax.ShapeDtypeStruct((batch * ctx_len, d_q), BF16),
        grid=(batch,),
        in_specs=[pl.BlockSpec(memory_space=pltpu.SMEM),
                  pl.BlockSpec((ctx_len, d_q), lambda b: (b, 0)),
                  pl.BlockSpec((ctx_len, d_kv), lambda b: (b, kcol)),
                  pl.BlockSpec((ctx_len, d_kv), lambda b: (b, vcol))],
        out_specs=pl.BlockSpec((ctx_len, d_q), lambda b: (b, 0)),
        compiler_params=_params("arbitrary"),
        name="ctx_attention",
    )(sink, qkv_ctx, qkv_ctx, qkv_ctx)


def _router_kernel(x_ref, sh_ref, sc_ref, wr_ref, br_ref, cnt0_ref, route_ref, cnt_ref, carry_ref):
    @pl.when(pl.program_id(0) == 0)
    def _():
        carry_ref[...] = cnt0_ref[...]

    h = _modulated_norm(x_ref[...], sh_ref[0], sc_ref[0])
    logits = jnp.dot(h, wr_ref[...], precision=HIGHEST, preferred_element_type=F32) + br_ref[...]
    tm = logits.shape[0]
    lane_i = lax.broadcasted_iota(jnp.int32, logits.shape, 1)
    lane = lane_i.astype(F32)
    neg = -jnp.inf
    lg = jnp.where(lane_i < N_GROUPS, logits, neg)
    mg = jnp.max(lg, axis=-1, keepdims=True)
    p_g = 1.0 / jnp.sum(jnp.exp(lg - mg), axis=-1, keepdims=True)
    g_sel = jnp.min(jnp.where(lg == mg, lane, float(ROUTE_W)), axis=-1, keepdims=True)
    lo = N_GROUPS + g_sel * EXPERTS_PER_GROUP
    lf = jnp.where((lane >= lo) & (lane < lo + EXPERTS_PER_GROUP), logits, neg)
    m1 = jnp.max(lf, axis=-1, keepdims=True)
    i1 = jnp.min(jnp.where(lf == m1, lane, float(ROUTE_W)), axis=-1, keepdims=True)
    lf2 = jnp.where(lane == i1, neg, lf)
    m2 = jnp.max(lf2, axis=-1, keepdims=True)
    i2 = jnp.min(jnp.where(lf2 == m2, lane, float(ROUTE_W)), axis=-1, keepdims=True)
    a2 = jnp.exp(m2 - m1)
    gate1 = p_g / (1.0 + a2)
    gate2 = gate1 * a2
    sel1 = lane == i1
    sel2 = lane == i2
    onehot = jnp.where(sel1 | sel2, 1.0, 0.0)
    r_i = lax.broadcasted_iota(jnp.int32, (tm, tm), 0)
    c_i = lax.broadcasted_iota(jnp.int32, (tm, tm), 1)
    tri = jnp.where(r_i > c_i, 1.0, 0.0).astype(BF16)
    before = jnp.dot(tri, onehot.astype(BF16), preferred_element_type=F32) + carry_ref[...]
    rank1 = jnp.sum(jnp.where(sel1, before, 0.0), axis=-1, keepdims=True)
    rank2 = jnp.sum(jnp.where(sel2, before, 0.0), axis=-1, keepdims=True)
    carry_ref[...] += jnp.sum(onehot, axis=0, keepdims=True)
    cnt_ref[...] = carry_ref[...]
    vals = (i1 - N_GROUPS, i2 - N_GROUPS, rank1, rank2, gate1, gate2)
    route = jnp.zeros_like(logits)
    for idx, v in enumerate(vals):
        route = jnp.where(lane_i == idx, v, route)
    route_ref[...] = route


def router(x, sh, sc, wr, br, cnt0, rows_per_seg):
    t, d = x.shape
    tm = _tile(rows_per_seg, 512)
    seg = lambda i: ((i * tm) // rows_per_seg, 0, 0)
    return pl.pallas_call(
        _router_kernel,
        out_shape=(jax.ShapeDtypeStruct((t, ROUTE_W), F32), jax.ShapeDtypeStruct((1, ROUTE_W), F32)),
        grid=(t // tm,),
        in_specs=[pl.BlockSpec((tm, d), lambda i: (i, 0)),
                  pl.BlockSpec((1, 1, d), seg), pl.BlockSpec((1, 1, d), seg),
                  pl.BlockSpec((d, ROUTE_W), lambda i: (0, 0)),
                  pl.BlockSpec((1, ROUTE_W), lambda i: (0, 0)),
                  pl.BlockSpec((1, ROUTE_W), lambda i: (0, 0))],
        out_specs=(pl.BlockSpec((tm, ROUTE_W), lambda i: (i, 0)),
                   pl.BlockSpec((1, ROUTE_W), lambda i: (0, 0))),
        scratch_shapes=[pltpu.VMEM((1, ROUTE_W), F32)],
        compiler_params=_params("arbitrary"),
        name="router",
    )(x, sh, sc, wr, br, cnt0)


def _row_copy(src, src_row, dst, dst_row, sem):
    return pltpu.make_async_copy(src.at[pl.ds(src_row, 1)], dst.at[pl.ds(dst_row, 1)], sem)


def _dispatch_kernel(dest_ref, x_ref, sh_ref, sc_ref, xs_in_ref, xs_ref, h_ref, sem):
    del xs_in_ref
    tm = x_ref.shape[0]
    base = pl.program_id(0) * tm
    h_ref[...] = _modulated_norm(x_ref[...], sh_ref[0], sc_ref[0])

    def issue(r, carry):
        _row_copy(h_ref, r, xs_ref, dest_ref[2 * (base + r)], sem).start()
        _row_copy(h_ref, r, xs_ref, dest_ref[2 * (base + r) + 1], sem).start()
        return carry

    def drain(r, carry):
        _row_copy(h_ref, r, xs_ref, dest_ref[2 * (base + r)], sem).wait()
        _row_copy(h_ref, r, xs_ref, dest_ref[2 * (base + r) + 1], sem).wait()
        return carry

    lax.fori_loop(0, tm, issue, 0)
    lax.fori_loop(0, tm, drain, 0)


def dispatch(dest, x, sh, sc, xs, rows_per_seg):
    t, d = x.shape
    tm = _tile(rows_per_seg, 256)
    seg = lambda i, dest_ref: ((i * tm) // rows_per_seg, 0, 0)
    return pl.pallas_call(
        _dispatch_kernel,
        out_shape=jax.ShapeDtypeStruct(xs.shape, xs.dtype),
        grid_spec=pltpu.PrefetchScalarGridSpec(
            num_scalar_prefetch=1,
            grid=(t // tm,),
            in_specs=[pl.BlockSpec((tm, d), lambda i, dest_ref: (i, 0)),
                      pl.BlockSpec((1, 1, d), seg), pl.BlockSpec((1, 1, d), seg),
                      pl.BlockSpec(memory_space=pl.ANY)],
            out_specs=pl.BlockSpec(memory_space=pl.ANY),
            scratch_shapes=[pltpu.VMEM((tm, d), F32), pltpu.SemaphoreType.DMA]),
        input_output_aliases={4: 0},
        compiler_params=_params("arbitrary"),
        name="moe_dispatch",
    )(dest, x, sh, sc, xs)


def _expert_kernel(bexp_ref, nused_ref, xs_ref, w1_ref, w3_ref, w2_ref, ys_ref, w1b, w3b, w2b):
    i = pl.program_id(0)
    e = bexp_ref[i]
    prev = bexp_ref[jnp.maximum(i - 1, 0)]

    @pl.when((i == 0) | (e != prev))
    def _():
        w1b[...] = w1_ref[0, 0].astype(BF16)
        w3b[...] = w3_ref[0, 0].astype(BF16)
        w2b[...] = w2_ref[0, 0].astype(BF16)

    @pl.when(i < nused_ref[0])
    def _():
        xb = xs_ref[...].astype(BF16)
        h1 = jnp.dot(xb, w1b[...], preferred_element_type=F32)
        h3 = jnp.dot(xb, w3b[...], preferred_element_type=F32)
        a = (h1 * jax.nn.sigmoid(h1) * h3).astype(BF16)
        ys_ref[...] = jnp.dot(a, w2b[...], preferred_element_type=F32)

    @pl.when(i >= nused_ref[0])
    def _():
        ys_ref[...] = jnp.zeros_like(ys_ref)


def experts(block_exp, nused, xs, w1, w3, w2, layer):
    n_slots, d = xs.shape
    de = w1.shape[-1]
    blk = lambda i, bexp, nu: (jnp.minimum(i, nu[0] - 1), 0)
    return pl.pallas_call(
        _expert_kernel,
        out_shape=jax.ShapeDtypeStruct((n_slots, d), F32),
        grid_spec=pltpu.PrefetchScalarGridSpec(
            num_scalar_prefetch=2,
            grid=(n_slots // MOE_BLOCK,),
            in_specs=[pl.BlockSpec((MOE_BLOCK, d), blk),
                      pl.BlockSpec((1, 1, d, de), lambda i, bexp, nu: (layer, bexp[i], 0, 0)),
                      pl.BlockSpec((1, 1, d, de), lambda i, bexp, nu: (layer, bexp[i], 0, 0)),
                      pl.BlockSpec((1, 1, de, d), lambda i, bexp, nu: (layer, bexp[i], 0, 0))],
            out_specs=pl.BlockSpec((MOE_BLOCK, d), lambda i, bexp, nu: (i, 0)),
            scratch_shapes=[pltpu.VMEM((d, de), BF16), pltpu.VMEM((d, de), BF16),
                            pltpu.VMEM((de, d), BF16)]),
        compiler_params=_params("arbitrary"),
        name="moe_experts",
    )(block_exp, nused, xs, w1, w3, w2)


def _combine_kernel(dest_ref, x_ref, route_ref, g_ref, gain_ref, ys_ref, o_ref, y0_ref, y1_ref, sem,
                    *, final):
    tm = x_ref.shape[0]
    base = pl.program_id(0) * tm

    def issue(r, carry):
        _row_copy(ys_ref, dest_ref[2 * (base + r)], y0_ref, r, sem).start()
        _row_copy(ys_ref, dest_ref[2 * (base + r) + 1], y1_ref, r, sem).start()
        return carry

    def drain(r, carry):
        _row_copy(ys_ref, dest_ref[2 * (base + r)], y0_ref, r, sem).wait()
        _row_copy(ys_ref, dest_ref[2 * (base + r) + 1], y1_ref, r, sem).wait()
        return carry

    lax.fori_loop(0, tm, issue, 0)
    lax.fori_loop(0, tm, drain, 0)
    route = route_ref[...]
    y = route[:, 4:5] * y0_ref[...] + route[:, 5:6] * y1_ref[...]
    out = x_ref[...] + g_ref[0] * y
    if final:
        out = out * lax.rsqrt(jnp.mean(out * out, axis=-1, keepdims=True) + EPS) * gain_ref[...]
    o_ref[...] = out


def combine(dest, x, route, gate, gain, ys, rows_per_seg, final):
    t, d = x.shape
    tm = _tile(rows_per_seg, 256)
    return pl.pallas_call(
        functools.partial(_combine_kernel, final=final),
        out_shape=jax.ShapeDtypeStruct((t, d), F32),
        grid_spec=pltpu.PrefetchScalarGridSpec(
            num_scalar_prefetch=1,
            grid=(t // tm,),
            in_specs=[pl.BlockSpec((tm, d), lambda i, dest_ref: (i, 0)),
                      pl.BlockSpec((tm, ROUTE_W), lambda i, dest_ref: (i, 0)),
                      pl.BlockSpec((1, 1, d), lambda i, dest_ref: ((i * tm) // rows_per_seg, 0, 0)),
                      pl.BlockSpec((1, d), lambda i, dest_ref: (0, 0)),
                      pl.BlockSpec(memory_space=pl.ANY)],
            out_specs=pl.BlockSpec((tm, d), lambda i, dest_ref: (i, 0)),
            scratch_shapes=[pltpu.VMEM((tm, d), F32), pltpu.VMEM((tm, d), F32),
                            pltpu.SemaphoreType.DMA]),
        compiler_params=_params("arbitrary"),
        name="moe_combine",
    )(dest, x, route, gate, gain, ys)


def _slot_plan(route_list, counts):
    cnt = counts[0, N_GROUPS:N_GROUPS + N_EXPERTS].astype(jnp.int32)
    padded = (cnt + MOE_BLOCK - 1) // MOE_BLOCK * MOE_BLOCK
    pad_end = jnp.cumsum(padded)
    pad_start = pad_end - padded
    n_assign = 2 * sum(r.shape[0] for r in route_list)
    n_blocks = -(-n_assign // MOE_BLOCK) + N_EXPERTS
    block_start = jnp.arange(n_blocks, dtype=jnp.int32) * MOE_BLOCK
    block_exp = jnp.clip(jnp.searchsorted(pad_end, block_start, side='right'), 0,
                         N_EXPERTS - 1).astype(jnp.int32)
    nused = (pad_end[-1:] // MOE_BLOCK).astype(jnp.int32)
    dests = []
    for r in route_list:
        e = r[:, 0:2].astype(jnp.int32)
        dests.append((pad_start[e] + r[:, 2:4].astype(jnp.int32)).reshape(-1))
    return dests, block_exp, nused, n_blocks


def hier_moe_layer(streams, layer, wr, br, w1, w3, w2, gain, final):
    counts = jnp.zeros((1, ROUTE_W), F32)
    routes = []
    for x, sh, sc, _, rps in streams:
        route, counts = router(x, sh, sc, wr, br, counts, rps)
        routes.append(route)
    dests, block_exp, nused, n_blocks = _slot_plan(routes, counts)
    d = streams[0][0].shape[1]
    xs = jnp.zeros((n_blocks * MOE_BLOCK, d), F32)
    for (x, sh, sc, _, rps), dest in zip(streams, dests):
        xs = dispatch(dest, x, sh, sc, xs, rps)
    ys = experts(block_exp, nused, xs, w1, w3, w2, layer)
    return [combine(dest, x, route, g, gain, ys, rps, final)
            for (x, _, _, g, rps), dest, route in zip(streams, dests, routes)]


def kernel(x, c, ctx, c_ctx, mod_w, mod_b, hy_w_in, hy_conv_w, hy_conv_b, hy_f_w1, hy_f_b1,
           hy_f_freq1, hy_f_w2, hy_f_b2, hy_f_freq2, hy_f_w3, hy_bias, hy_w_out,
           at_w_qkv, at_sink, at_w_o, moe_wg, moe_bg, moe_we, moe_be, moe_w1, moe_w3, moe_w2,
           final_gain):
    batch, seq_len, d = x.shape
    ctx_len = ctx.shape[1]
    depth = mod_w.shape[0]
    d_q = at_w_qkv.shape[2] - 2 * N_KV_HEADS * HEAD_DIM
    d_kv = N_KV_HEADS * HEAD_DIM
    n_ctx = batch * ctx_len

    xl = x.reshape(batch * seq_len, d)
    xc = ctx.reshape(n_ctx, d)

    n_cond = batch + 1
    cond = jnp.concatenate([c, c_ctx[None, :], jnp.zeros((-n_cond % 8, d), F32)], axis=0)
    mods = adaln(cond, mod_w, mod_b)

    def mod_slices(i):
        parts = [mods[i, :, j * d:(j + 1) * d] for j in range(6)]
        lat = [p[:batch].reshape(batch, 1, d) for p in parts]
        cx = [p[batch:batch + 1].reshape(1, 1, d) for p in parts]
        return lat, cx

    tables = {n: dft_tables(n) for n in (seq_len, ctx_len)}
    rtabs = rope_tables(seq_len)
    gain = final_gain.reshape(1, d)

    for i in range(depth):
        last = i == depth - 1
        j = i // 2
        (sh1, sc1, g1, sh2, sc2, g2), (sh1c, sc1c, g1c, sh2c, sc2c, g2c) = mod_slices(i)
        if i % 2 == 0:
            w_in = hy_w_in[j].astype(BF16)
            w_out = hy_w_out[j].astype(BF16)
            filt = (hy_f_w1[j], hy_f_b1[j], hy_f_freq1[j], hy_f_w2[j], hy_f_b2[j], hy_f_freq2[j],
                    hy_f_w3[j])
            bias = hy_bias[j].reshape(2, 1, d)

            def mixer(xs_, sh, sc, g, n_seq, rps):
                u = modmm(xs_, sh, sc, w_in, rps)
                u3 = short_conv(u, hy_conv_w[j], hy_conv_b[j], n_seq)
                e_taps, o_taps = hyena_filter_taps(n_seq, *filt, d)
                spec = filter_spectrum(*tables[n_seq], e_taps, o_taps)
                z = hyena_long_convs(u3, tables[n_seq], spec, bias, n_seq, d)
                return mm_res(z, w_out, xs_, g, rps)

            xl = mixer(xl, sh1, sc1, g1, seq_len, seq_len)
            if not last:
                xc = mixer(xc, sh1c, sc1c, g1c, ctx_len, n_ctx)
        else:
            w_qkv = at_w_qkv[j].astype(BF16)
            w_o = at_w_o[j].astype(BF16)
            sink = at_sink[j]
            qkv = rope_qk(modmm(xl, sh1, sc1, w_qkv, seq_len), rtabs, seq_len, d_q, d_kv)
            qkv_c = modmm(xc, sh1c, sc1c, w_qkv, n_ctx)
            o = window_attention(qkv, qkv_c, sink, batch, seq_len, ctx_len, d_q, d_kv)
            xl = mm_res(o, w_o, xl, g1, seq_len)
            if not last:
                o_c = ctx_attention(qkv_c, sink, batch, ctx_len, d_q, d_kv)
                xc = mm_res(o_c, w_o, xc, g1c, n_ctx)

        wr = jnp.concatenate([moe_wg[i], moe_we[i],
                              jnp.zeros((d, ROUTE_W - N_GROUPS - N_EXPERTS), F32)], axis=1)
        br = jnp.concatenate([moe_bg[i], moe_be[i],
                              jnp.zeros((ROUTE_W - N_GROUPS - N_EXPERTS,), F32)]).reshape(1, ROUTE_W)
        streams = [(xl, sh2, sc2, g2, seq_len)]
        if not last:
            streams.append((xc, sh2c, sc2c, g2c, n_ctx))
        outs = hier_moe_layer(streams, i, wr, br, moe_w1, moe_w3, moe_w2, gain, last)
        xl = outs[0]
        if not last:
            xc = outs[1]

    return xl.reshape(batch, seq_len, d)
```

```python
import functools
import math

import jax
import jax.numpy as jnp
from jax import lax
from jax.experimental import pallas as pl
from jax.experimental.pallas import tpu as pltpu

F32 = jnp.float32
BF16 = jnp.bfloat16
HIGHEST = lax.Precision.HIGHEST

EPS = 1e-6
GRID_W = 64
HEAD_DIM = 64
N_KV_HEADS = 4
WINDOW = 128
Q_BLOCK = 128
ROPE_PAIRS = HEAD_DIM // 4
ROPE_BASE = 10000.0
N_GROUPS = 8
EXPERTS_PER_GROUP = 8
N_EXPERTS = N_GROUPS * EXPERTS_PER_GROUP
MOE_BLOCK = 256
HY_EMB = 33
HY_BANDS = (HY_EMB - 1) // 2
HY_FAST_DECAY_PCT = 0.3
HY_SLOW_DECAY_PCT = 1.5
HY_DECAY_TARGET = 1e-2

LANES = 128
ROUTE_W = LANES
VMEM_LIMIT = 56 * 1024 * 1024


def _tile(n, pref):
    if n <= pref:
        return n
    t = pref
    while n % t:
        t -= 1
    return t


def _params(*sem):
    return pltpu.CompilerParams(dimension_semantics=sem, vmem_limit_bytes=VMEM_LIMIT)


def _modulated_norm(x, sh, sc):
    y = x * lax.rsqrt(jnp.mean(x * x, axis=-1, keepdims=True) + EPS)
    return y * (1.0 + sc) + sh


def _adaln_kernel(c_ref, w_ref, b_ref, o_ref):
    c = c_ref[...]
    a = c * jax.nn.sigmoid(c)
    o_ref[0] = jnp.dot(a, w_ref[0], precision=HIGHEST, preferred_element_type=F32) + b_ref[0]


def adaln(cond, mod_w, mod_b):
    depth, d, n = mod_w.shape
    r = cond.shape[0]
    tn = _tile(n, 1024)
    return pl.pallas_call(
        _adaln_kernel,
        out_shape=jax.ShapeDtypeStruct((depth, r, n), F32),
        grid=(depth, n // tn),
        in_specs=[pl.BlockSpec((r, d), lambda l, j: (0, 0)),
                  pl.BlockSpec((1, d, tn), lambda l, j: (l, 0, j)),
                  pl.BlockSpec((1, 1, tn), lambda l, j: (l, 0, j))],
        out_specs=pl.BlockSpec((1, r, tn), lambda l, j: (l, 0, j)),
        compiler_params=_params("arbitrary", "arbitrary"),
        name="adaln",
    )(cond, mod_w, mod_b.reshape(depth, 1, n))


def _modmm_kernel(x_ref, sh_ref, sc_ref, w_ref, o_ref, h_ref):
    @pl.when(pl.program_id(1) == 0)
    def _():
        h_ref[...] = _modulated_norm(x_ref[...], sh_ref[0], sc_ref[0]).astype(BF16)

    o_ref[...] = jnp.dot(h_ref[...], w_ref[...], preferred_element_type=F32).astype(o_ref.dtype)


def modmm(x, sh, sc, w, rows_per_seg):
    t, d = x.shape
    n = w.shape[1]
    tm = _tile(rows_per_seg, 1024)
    tn = _tile(n, 512)
    seg = lambda i, j: ((i * tm) // rows_per_seg, 0, 0)
    return pl.pallas_call(
        _modmm_kernel,
        out_shape=jax.ShapeDtypeStruct((t, n), BF16),
        grid=(t // tm, n // tn),
        in_specs=[pl.BlockSpec((tm, d), lambda i, j: (i, 0)),
                  pl.BlockSpec((1, 1, d), seg),
                  pl.BlockSpec((1, 1, d), seg),
                  pl.BlockSpec((d, tn), lambda i, j: (0, j))],
        out_specs=pl.BlockSpec((tm, tn), lambda i, j: (i, j)),
        scratch_shapes=[pltpu.VMEM((tm, d), BF16)],
        compiler_params=_params("arbitrary", "arbitrary"),
        name="modmm",
    )(x, sh, sc, w)


def _mmres_kernel(a_ref, w_ref, r_ref, g_ref, o_ref):
    acc = jnp.dot(a_ref[...], w_ref[...], preferred_element_type=F32)
    o_ref[...] = r_ref[...] + g_ref[0] * acc


def mm_res(a, w, res, gate, rows_per_seg):
    t, k = a.shape
    n = w.shape[1]
    tm = _tile(rows_per_seg, 1024)
    tn = _tile(n, 512)
    return pl.pallas_call(
        _mmres_kernel,
        out_shape=jax.ShapeDtypeStruct((t, n), F32),
        grid=(t // tm, n // tn),
        in_specs=[pl.BlockSpec((tm, k), lambda i, j: (i, 0)),
                  pl.BlockSpec((k, tn), lambda i, j: (0, j)),
                  pl.BlockSpec((tm, tn), lambda i, j: (i, j)),
                  pl.BlockSpec((1, 1, tn), lambda i, j: ((i * tm) // rows_per_seg, 0, j))],
        out_specs=pl.BlockSpec((tm, tn), lambda i, j: (i, j)),
        compiler_params=_params("arbitrary", "arbitrary"),
        name="mm_res",
    )(a, w, res, gate)


def _sconv_kernel(u_ref, w_ref, b_ref, o_ref):
    x = u_ref[...].astype(F32)
    n = x.shape[0]
    row = lax.broadcasted_iota(jnp.int32, x.shape, 0)
    prev = jnp.where(row == 0, 0.0, pltpu.roll(x, 1, 0))
    nxt = jnp.where(row == n - 1, 0.0, pltpu.roll(x, n - 1, 0))
    y = prev * w_ref[0:1, :] + x * w_ref[1:2, :] + nxt * w_ref[2:3, :] + b_ref[...]
    o_ref[...] = y.astype(o_ref.dtype)


def short_conv(u, w, b, seq_len):
    t, c = u.shape
    tc = _tile(c, 256)
    return pl.pallas_call(
        _sconv_kernel,
        out_shape=jax.ShapeDtypeStruct((t, c), BF16),
        grid=(t // seq_len, c // tc),
        in_specs=[pl.BlockSpec((seq_len, tc), lambda s, j: (s, j)),
                  pl.BlockSpec((3, tc), lambda s, j: (0, j)),
                  pl.BlockSpec((1, tc), lambda s, j: (0, j))],
        out_specs=pl.BlockSpec((seq_len, tc), lambda s, j: (s, j)),
        compiler_params=_params("arbitrary", "arbitrary"),
        name="short_conv",
    )(u, w, b.reshape(1, c))


def _filter_mlp_kernel(feat_ref, w1_ref, b1_ref, f1_ref, w2_ref, b2_ref, f2_ref, h_ref):
    h = jnp.sin(f1_ref[...] * (jnp.dot(feat_ref[...], w1_ref[...], precision=HIGHEST,
                                       preferred_element_type=F32) + b1_ref[...]))
    h_ref[...] = jnp.sin(f2_ref[...] * (jnp.dot(h, w2_ref[...], precision=HIGHEST,
                                                preferred_element_type=F32) + b2_ref[...]))


def _filter_kernel(hf_ref, hb_ref, tf_ref, tb_ref, w3f_ref, w3b_ref, dl_ref, p_ref, q_ref, *,
                   even_odd):
    dl = jnp.abs(dl_ref[...])
    hf = jnp.dot(hf_ref[...], w3f_ref[...], precision=HIGHEST,
                 preferred_element_type=F32) * jnp.exp(-tf_ref[...] * dl)
    hb = jnp.dot(hb_ref[...], w3b_ref[...], precision=HIGHEST,
                 preferred_element_type=F32) * jnp.exp(-tb_ref[...] * dl)
    row = lax.broadcasted_iota(jnp.int32, hf.shape, 0)
    hb = jnp.where(row == 0, 0.0, hb)
    s = lax.rsqrt(jnp.sum(hf * hf + hb * hb, axis=0, keepdims=True) + EPS)
    p, q = ((hf + hb) * s, (hf - hb) * s) if even_odd else (hf * s, hb * s)
    p_ref[0] = p.astype(p_ref.dtype).reshape(p_ref.shape[1:])
    q_ref[0] = q.astype(q_ref.dtype).reshape(q_ref.shape[1:])


def hyena_filter_taps(seq_len, w1, b1, f1, w2, b2, f2, w3, d, even_odd):
    t = jnp.linspace(0.0, 1.0, seq_len, dtype=F32)[:, None]
    omega = (2.0 * math.pi / seq_len) * jnp.arange(seq_len, dtype=F32)[:, None]
    bands = jnp.linspace(1e-4, HY_BANDS - 1, HY_BANDS, dtype=F32)[None, :]
    feats = jnp.concatenate([t, jnp.cos(bands * omega), -jnp.sin(bands * omega),
                             jnp.zeros((seq_len, LANES - HY_EMB), F32)], axis=-1)
    if not even_odd:
        feats = jnp.concatenate([feats, feats[:1], feats[:0:-1]], axis=0)
    w1p = jnp.concatenate([w1, jnp.zeros((LANES - HY_EMB, w1.shape[1]), F32)], axis=0)
    fw = w1.shape[1]
    rows = feats.shape[0]
    vec = lambda a: a.reshape(1, fw)
    h = pl.pallas_call(
        _filter_mlp_kernel,
        out_shape=jax.ShapeDtypeStruct((rows, fw), F32),
        compiler_params=_params(),
        name="hyena_filter_mlp",
    )(feats, w1p, vec(b1), vec(f1), w2, vec(b2), vec(f2))
    tcol = feats[:, 0:1]
    max_decay = math.log(HY_DECAY_TARGET) / HY_FAST_DECAY_PCT
    min_decay = math.log(HY_DECAY_TARGET) / HY_SLOW_DECAY_PCT
    deltas = jnp.linspace(min_decay, max_decay, d, dtype=F32).reshape(1, d)
    tn = _tile(d, 256)
    nb = d // tn
    bsel = 0 if even_odd else 1
    if even_odd:
        oshape, oblock = (2, seq_len, d), (1, seq_len, tn)
        omap = lambda o, j: (o, 0, j)
    else:
        n2 = 2 * seq_len // FFT_N1
        oshape, oblock = (2, seq_len // n2, n2, d), (1, seq_len // n2, n2, tn)
        omap = lambda o, j: (o, 0, 0, j)
    out = jax.ShapeDtypeStruct(oshape, BF16)
    return pl.pallas_call(
        functools.partial(_filter_kernel, even_odd=even_odd),
        out_shape=(out, out),
        grid=(2, nb),
        in_specs=[pl.BlockSpec((seq_len, fw), lambda o, j: (0, 0)),
                  pl.BlockSpec((seq_len, fw), lambda o, j: (bsel, 0)),
                  pl.BlockSpec((seq_len, 1), lambda o, j: (0, 0)),
                  pl.BlockSpec((seq_len, 1), lambda o, j: (bsel, 0)),
                  pl.BlockSpec((fw, tn), lambda o, j: (0, (2 * o) * nb + j)),
                  pl.BlockSpec((fw, tn), lambda o, j: (0, (2 * o + 1) * nb + j)),
                  pl.BlockSpec((1, tn), lambda o, j: (0, j))],
        out_specs=(pl.BlockSpec(oblock, omap), pl.BlockSpec(oblock, omap)),
        compiler_params=_params("arbitrary", "arbitrary"),
        name="hyena_filter",
    )(h, h, tcol, tcol, w3, w3, deltas)


def _alt_sum(x):
    row = lax.broadcasted_iota(jnp.int32, x.shape, 0)
    return jnp.sum(jnp.where((row & 1) == 1, -x, x), axis=0, keepdims=True)


def _spectrum_kernel(c_ref, s_ref, e_ref, o_ref, kr_ref, km_ref):
    kr_ref[0] = jnp.dot(c_ref[...], e_ref[0], preferred_element_type=F32)
    km = jnp.dot(s_ref[...], o_ref[0], preferred_element_type=F32)

    @pl.when(pl.program_id(2) != 0)
    def _():
        km_ref[0] = km

    @pl.when(pl.program_id(2) == 0)
    def _():
        row = lax.broadcasted_iota(jnp.int32, km.shape, 0)
        km_ref[0] = jnp.where(row == 0, _alt_sum(e_ref[0].astype(F32)), km)


def filter_spectrum(cmat, smat, e_taps, o_taps):
    _, n, d = e_taps.shape
    tk = _tile(n, 512)
    tn = _tile(d, 512)
    out = jax.ShapeDtypeStruct((2, n, d), F32)
    return pl.pallas_call(
        _spectrum_kernel,
        out_shape=(out, out),
        grid=(2, d // tn, n // tk),
        in_specs=[pl.BlockSpec((tk, n), lambda o, j, k: (k, 0)),
                  pl.BlockSpec((tk, n), lambda o, j, k: (k, 0)),
                  pl.BlockSpec((1, n, tn), lambda o, j, k: (o, 0, j)),
                  pl.BlockSpec((1, n, tn), lambda o, j, k: (o, 0, j))],
        out_specs=(pl.BlockSpec((1, tk, tn), lambda o, j, k: (o, k, j)),
                   pl.BlockSpec((1, tk, tn), lambda o, j, k: (o, k, j))),
        compiler_params=_params("arbitrary", "arbitrary", "arbitrary"),
        name="filter_spectrum",
    )(cmat, smat, e_taps, o_taps)


def _dft_fwd_kernel(c_ref, s_ref, u_ref, kr_ref, km_ref, yr_ref, q_ref):
    u = u_ref[...]
    a = jnp.dot(c_ref[...], u, preferred_element_type=F32)
    b = jnp.dot(s_ref[...], u, preferred_element_type=F32)
    kr = kr_ref[0]
    km = km_ref[0]
    yr = a * kr - b * km
    q = b * kr + a * km

    @pl.when(pl.program_id(2) != 0)
    def _():
        yr_ref[...] = yr.astype(yr_ref.dtype)
        q_ref[...] = q.astype(q_ref.dtype)

    @pl.when(pl.program_id(2) == 0)
    def _():
        row = lax.broadcasted_iota(jnp.int32, yr.shape, 0)
        ynyq = _alt_sum(u.astype(F32)) * km[0:1, :]
        yr_ref[...] = jnp.where(row == 0, 0.5 * yr, yr).astype(yr_ref.dtype)
        q_ref[...] = jnp.where(row == 0, 0.5 * ynyq, q).astype(q_ref.dtype)


def dft_fwd(cmat, smat, u, col0, kr, km, order, seq_len, d):
    t = u.shape[0]
    tk = _tile(seq_len, 512)
    tn = _tile(d, 512)
    cb = col0 // tn
    out = jax.ShapeDtypeStruct((t, d), BF16)
    nk = seq_len // tk
    return pl.pallas_call(
        _dft_fwd_kernel,
        out_shape=(out, out),
        grid=(t // seq_len, d // tn, nk),
        in_specs=[pl.BlockSpec((tk, seq_len), lambda b, j, k: (k, 0)),
                  pl.BlockSpec((tk, seq_len), lambda b, j, k: (k, 0)),
                  pl.BlockSpec((seq_len, tn), lambda b, j, k: (b, cb + j)),
                  pl.BlockSpec((1, tk, tn), lambda b, j, k: (order, k, j)),
                  pl.BlockSpec((1, tk, tn), lambda b, j, k: (order, k, j))],
        out_specs=(pl.BlockSpec((tk, tn), lambda b, j, k: (b * nk + k, j)),
                   pl.BlockSpec((tk, tn), lambda b, j, k: (b * nk + k, j))),
        compiler_params=_params("arbitrary", "arbitrary", "arbitrary"),
        name="dft_fwd",
    )(cmat, smat, u, kr, km)


def _dft_inv_kernel(c_ref, s_ref, yr_ref, q_ref, u_ref, g_ref, bias_ref, o_ref, *, seq_len):
    acc = jnp.dot(c_ref[...], yr_ref[...], preferred_element_type=F32)
    acc += jnp.dot(s_ref[...], q_ref[...], preferred_element_type=F32)
    tt = acc.shape[0]
    t_idx = pl.program_id(2) * tt + lax.broadcasted_iota(jnp.int32, acc.shape, 0)
    half_nyq = q_ref[0:1, :].astype(F32)
    acc += jnp.where((t_idx & 1) == 1, -half_nyq, half_nyq)
    u = u_ref[...].astype(F32)
    y = acc * (1.0 / seq_len) + u * bias_ref[0]
    o_ref[...] = (g_ref[...].astype(F32) * y).astype(o_ref.dtype)


def dft_inv(cmat, smat, yr, q, u, ucol0, gsrc, gcol0, bias, order, seq_len, d):
    t = yr.shape[0]
    tt = _tile(seq_len, 512)
    tn = _tile(d, 512)
    ub, gb = ucol0 // tn, gcol0 // tn
    nt = seq_len // tt
    return pl.pallas_call(
        functools.partial(_dft_inv_kernel, seq_len=seq_len),
        out_shape=jax.ShapeDtypeStruct((t, d), BF16),
        grid=(t // seq_len, d // tn, nt),
        in_specs=[pl.BlockSpec((tt, seq_len), lambda b, j, k: (k, 0)),
                  pl.BlockSpec((tt, seq_len), lambda b, j, k: (k, 0)),
                  pl.BlockSpec((seq_len, tn), lambda b, j, k: (b, j)),
                  pl.BlockSpec((seq_len, tn), lambda b, j, k: (b, j)),
                  pl.BlockSpec((tt, tn), lambda b, j, k: (b * nt + k, ub + j)),
                  pl.BlockSpec((tt, tn), lambda b, j, k: (b * nt + k, gb + j)),
                  pl.BlockSpec((1, 1, tn), lambda b, j, k: (order, 0, j))],
        out_specs=pl.BlockSpec((tt, tn), lambda b, j, k: (b * nt + k, j)),
        compiler_params=_params("arbitrary", "arbitrary", "arbitrary"),
        name="dft_inv",
    )(cmat, smat, yr, q, u, gsrc, bias)


def dft_tables(seq_len):
    n = jnp.arange(seq_len, dtype=jnp.int32)
    ang = ((n[:, None] * n[None, :]) % (2 * seq_len)).astype(F32) * (math.pi / seq_len)
    return jnp.cos(ang).astype(BF16), jnp.sin(ang).astype(BF16)


def hyena_long_convs(u3, tables, spectrum, bias, seq_len, d):
    cmat, smat = tables
    kr, km = spectrum
    yr, q = dft_fwd(cmat, smat, u3, 0, kr, km, 0, seq_len, d)
    z = dft_inv(cmat, smat, yr, q, u3, 0, u3, d, bias, 0, seq_len, d)
    yr, q = dft_fwd(cmat, smat, z, 0, kr, km, 1, seq_len, d)
    return dft_inv(cmat, smat, yr, q, z, 0, u3, 2 * d, bias, 1, seq_len, d)


FFT_N1 = 64
SUB = 8


def fft_tables(seq_len):
    n = 2 * seq_len
    n1, n2 = FFT_N1, n // FFT_N1
    two_pi = 2.0 * math.pi
    eye = jnp.eye(SUB, dtype=F32)
    k1 = jnp.arange(n1, dtype=jnp.int32)
    h1 = jnp.arange(n1 // 2, dtype=jnp.int32)
    ang = ((k1[:, None] * h1[None, :]) % n1).astype(F32) * (two_pi / n1)
    ca = jnp.stack([jnp.cos(ang), -jnp.sin(ang)], axis=1)
    ka = jnp.einsum('krn,jm->krjnm', ca, eye).reshape(n1 * 2 * SUB, (n1 // 2) * SUB)
    c2 = jnp.stack([jnp.cos(ang.T), -jnp.sin(ang.T)], axis=2)
    ka2 = jnp.einsum('tkr,jm->tjkrm', c2, eye).reshape((n1 // 2) * SUB, n1 * 2 * SUB)
    k2 = jnp.arange(n2 // 2, dtype=jnp.int32)
    m2 = jnp.arange(n2, dtype=jnp.int32)
    k = k1[:, None, None] + n1 * k2[None, :, None]
    th = ((k * m2[None, None, :]) % n).astype(F32) * (two_pi / n)
    c, s = jnp.cos(th), jnp.sin(th)
    fb = jnp.concatenate([jnp.concatenate([c, s], axis=2),
                          jnp.concatenate([-s, c], axis=2)], axis=1)
    ct, st = jnp.swapaxes(c, 1, 2), jnp.swapaxes(s, 1, 2)
    gb = jnp.concatenate([jnp.concatenate([ct, -st], axis=2),
                          jnp.concatenate([st, ct], axis=2)], axis=1)
    return tuple(a.astype(BF16) for a in (ka, fb, gb, ka2))


def _fft_stage_a(src_ref, ka_ref, a_ref):
    n1h, n2, tn = src_ref.shape
    for g in range(n2 // SUB):
        rhs = src_ref[:, g * SUB:(g + 1) * SUB, :].reshape(n1h * SUB, tn).astype(BF16)
        out = jnp.dot(ka_ref[...], rhs, preferred_element_type=F32)
        a_ref[:, :, g * SUB:(g + 1) * SUB, :] = out.reshape(FFT_N1, 2, SUB, tn)


def _fft_stage_b(fb_ref, a_ref, k1):
    _, _, n2, tn = a_ref.shape
    slab = a_ref[k1].reshape(2 * n2, tn).astype(BF16)
    return jnp.dot(fb_ref[k1], slab, preferred_element_type=F32)


def _block_alt_sum(src_ref):
    return _alt_sum(jnp.sum(src_ref[...], axis=0))


def _fftspec_kernel(tf_ref, tb_ref, ka_ref, fb_ref, kf_ref, knyq_ref, src_ref, a_ref):
    h = a_ref.shape[2] // 2
    tn = a_ref.shape[3]
    nyq = None
    for which, t_ref in enumerate((tf_ref, tb_ref)):
        src_ref[...] = t_ref[0].astype(F32)
        part = _block_alt_sum(src_ref)
        nyq = part if nyq is None else nyq + part
        _fft_stage_a(src_ref, ka_ref, a_ref)

        def body(k1, carry, which=which):
            x = _fft_stage_b(fb_ref, a_ref, k1).reshape(2, h, tn)
            if which == 0:
                kf_ref[0, k1] = x
            else:
                sign = 1.0 - 2.0 * jnp.asarray(k1 & 1, F32)
                kf_ref[0, k1] = kf_ref[0, k1] + sign * x
            return carry

        lax.fori_loop(0, FFT_N1, body, 0)
    knyq_ref[0] = nyq
    dc = kf_ref[0, 0]
    row = lax.broadcasted_iota(jnp.int32, dc.shape, 1)
    kf_ref[0, 0] = jnp.where(row == 0, 0.5 * dc, dc)


def _const_spec(shape):
    return pl.BlockSpec(shape, lambda *_: (0,) * len(shape), pipeline_mode=pl.Buffered(1))


def fft_filter_spectrum(tabs, taps_f, taps_b):
    ka, fb, _, _ = tabs
    _, n1h, n2, d = taps_f.shape
    tn = _tile(d, 256)
    tap = pl.BlockSpec((1, n1h, n2, tn), lambda o, j: (o, 0, 0, j))
    return pl.pallas_call(
        _fftspec_kernel,
        out_shape=(jax.ShapeDtypeStruct((2, FFT_N1, 2, n2 // 2, d), F32),
                   jax.ShapeDtypeStruct((2, 1, d), F32)),
        grid=(2, d // tn),
        in_specs=[tap, tap, _const_spec(ka.shape), _const_spec(fb.shape)],
        out_specs=(pl.BlockSpec((1, FFT_N1, 2, n2 // 2, tn), lambda o, j: (o, 0, 0, 0, j)),
                   pl.BlockSpec((1, 1, tn), lambda o, j: (o, 0, j))),
        scratch_shapes=[pltpu.VMEM((n1h, n2, tn), F32), pltpu.VMEM((FFT_N1, 2, n2, tn), F32)],
        compiler_params=_params("arbitrary", "arbitrary"),
        name="fft_filter_spectrum",
    )(taps_f, taps_b, ka, fb)


def _fftconv_kernel(u_ref, g_ref, bias_ref, kf_ref, knyq_ref, ka_ref, fb_ref, gb_ref, ka2_ref,
                    o_ref, src_ref, a_ref):
    n1h, n2, tn = src_ref.shape
    h = n2 // 2
    seq_len = n1h * n2
    src_ref[...] = u_ref[...].astype(F32)
    ynyq = _block_alt_sum(src_ref) * knyq_ref[0] * (0.5 / seq_len)
    _fft_stage_a(src_ref, ka_ref, a_ref)

    def body(k1, carry):
        x = _fft_stage_b(fb_ref, a_ref, k1)
        xr, xi = x[:h], x[h:]
        kr, ki = kf_ref[0, k1, 0], kf_ref[0, k1, 1]
        y = jnp.concatenate([xr * kr - xi * ki, xr * ki + xi * kr], axis=0).astype(BF16)
        z = jnp.dot(gb_ref[k1], y, preferred_element_type=F32)
        a_ref[k1] = z.reshape(2, n2, tn)
        return carry

    lax.fori_loop(0, FFT_N1, body, 0)
    bias = bias_ref[0]
    pair = 2 * SUB
    row = lax.broadcasted_iota(jnp.int32, (n1h, pair, tn), 1)
    nyq = jnp.where((row & 1) == 1, -ynyq, ynyq)
    for gg in range(n2 // pair):
        parts = []
        for g in (2 * gg, 2 * gg + 1):
            zg = a_ref[:, :, g * SUB:(g + 1) * SUB, :].reshape(FFT_N1 * 2 * SUB, tn).astype(BF16)
            acc = jnp.dot(ka2_ref[...], zg, preferred_element_type=F32)
            parts.append(acc.reshape(n1h, SUB, tn))
        acc = jnp.concatenate(parts, axis=1)
        rows = slice(gg * pair, (gg + 1) * pair)
        y = acc * (1.0 / seq_len) + nyq + src_ref[:, rows, :] * bias
        o_ref[:, rows, :] = (g_ref[:, rows, :].astype(F32) * y).astype(o_ref.dtype)


def fft_long_conv(tabs, spec, u, ucol0, gsrc, gcol0, bias, order, seq_len, d):
    ka, fb, gb, ka2 = tabs
    kf, knyq = spec
    nb, n2, _ = u.shape
    n1h = seq_len // n2
    tn = _tile(d, 256)
    ub, gb_ = ucol0 // tn, gcol0 // tn
    return pl.pallas_call(
        _fftconv_kernel,
        out_shape=jax.ShapeDtypeStruct((nb, n2, d), BF16),
        grid=(d // tn, nb // n1h),
        in_specs=[pl.BlockSpec((n1h, n2, tn), lambda j, b: (b, 0, ub + j)),
                  pl.BlockSpec((n1h, n2, tn), lambda j, b: (b, 0, gb_ + j)),
                  pl.BlockSpec((1, 1, tn), lambda j, b: (order, 0, j)),
                  pl.BlockSpec((1, FFT_N1, 2, n2 // 2, tn), lambda j, b: (order, 0, 0, 0, j),
                               pipeline_mode=pl.Buffered(1)),
                  pl.BlockSpec((1, 1, tn), lambda j, b: (order, 0, j)),
                  _const_spec(ka.shape), _const_spec(fb.shape), _const_spec(gb.shape),
                  _const_spec(ka2.shape)],
        out_specs=pl.BlockSpec((n1h, n2, tn), lambda j, b: (b, 0, j)),
        scratch_shapes=[pltpu.VMEM((n1h, n2, tn), F32), pltpu.VMEM((FFT_N1, 2, n2, tn), F32)],
        compiler_params=_params("arbitrary", "arbitrary"),
        name="fft_long_conv",
    )(u, gsrc, bias, kf, knyq, ka, fb, gb, ka2)


def hyena_long_convs_fft(u3, tabs, spec, bias, seq_len, d):
    t = u3.shape[0]
    n2 = 2 * seq_len // FFT_N1
    u3b = u3.reshape(t // n2, n2, 3 * d)
    z = fft_long_conv(tabs, spec, u3b, 0, u3b, d, bias, 0, seq_len, d)
    z = fft_long_conv(tabs, spec, z, 0, u3b, 2 * d, bias, 1, seq_len, d)
    return z.reshape(t, d)


def _rope_kernel(x_ref, cos_ref, sa_ref, sb_ref, o_ref, *, n_q_blocks, scale):
    j = pl.program_id(1)
    c, sa, sb = cos_ref[...], sa_ref[...], sb_ref[...]
    sc = jnp.where(j < n_q_blocks, scale, 1.0).astype(F32)
    for h in range(x_ref.shape[1] // LANES):
        x = x_ref[:, h * LANES:(h + 1) * LANES].astype(F32)
        y = x * c + pltpu.roll(x, LANES - ROPE_PAIRS, 1) * sa + pltpu.roll(x, ROPE_PAIRS, 1) * sb
        o_ref[:, h * LANES:(h + 1) * LANES] = (y * sc).astype(o_ref.dtype)


def rope_tables(seq_len):
    pos = jnp.arange(seq_len, dtype=jnp.int32)
    r = (pos // GRID_W).astype(F32)[:, None]
    col = (pos % GRID_W).astype(F32)[:, None]
    inv = ROPE_BASE ** (-(2.0 * jnp.arange(ROPE_PAIRS, dtype=F32)) / (2 * ROPE_PAIRS))
    ar, ac = r * inv, col * inv
    zeros = jnp.zeros_like(ar)
    cos = jnp.concatenate([jnp.cos(ar), jnp.cos(ar), jnp.cos(ac), jnp.cos(ac)], axis=1)
    sa = jnp.concatenate([-jnp.sin(ar), zeros, -jnp.sin(ac), zeros], axis=1)
    sb = jnp.concatenate([zeros, jnp.sin(ar), zeros, jnp.sin(ac)], axis=1)
    rep = LANES // HEAD_DIM
    return tuple(jnp.tile(a, (1, rep)) for a in (cos, sa, sb))


def rope_qk(qkv, tables, seq_len, d_q, d_kv):
    t, n = qkv.shape
    tr = _tile(seq_len, 512)
    tc = 2 * LANES
    nb = seq_len // tr
    tab = pl.BlockSpec((tr, LANES), lambda i, j: (i % nb, 0))
    return pl.pallas_call(
        functools.partial(_rope_kernel, n_q_blocks=d_q // tc, scale=Q_SCALE),
        out_shape=jax.ShapeDtypeStruct((t, n), qkv.dtype),
        grid=(t // tr, (d_q + d_kv) // tc),
        in_specs=[pl.BlockSpec((tr, tc), lambda i, j: (i, j)), tab, tab, tab],
        out_specs=pl.BlockSpec((tr, tc), lambda i, j: (i, j)),
        input_output_aliases={0: 0},
        compiler_params=_params("arbitrary", "arbitrary"),
        name="rope",
    )(qkv, *tables)


LOG2E = 1.4426950408889634
Q_SCALE = HEAD_DIM ** -0.5 * LOG2E


def _attend(q_ref, sink_ref, keys, vals, bias, o_ref, n_heads, q_per_kv, q_scale=None):
    nq = q_ref.shape[0]
    s_len = keys.shape[0]
    lane = lax.broadcasted_iota(jnp.int32, (s_len, HEAD_DIM), 1)
    ones_col = jnp.where(lane == 0, 1.0, 0.0).astype(BF16)
    hidx = lax.broadcasted_iota(jnp.int32, (q_per_kv, 1, 1), 0)
    pending = []
    for g in range(n_heads // q_per_kv):
        qs = []
        sink = jnp.zeros((q_per_kv, 1, 1), F32)
        for h in range(q_per_kv):
            col = (g * q_per_kv + h) * HEAD_DIM
            q = q_ref[:, col:col + HEAD_DIM]
            if q_scale is not None:
                q = (q.astype(F32) * q_scale).astype(BF16)
            qs.append(q)
            sink = jnp.where(hidx == h, sink_ref[g * q_per_kv + h] * LOG2E, sink)
        qg = jnp.concatenate(qs, axis=0)
        kg = keys[:, g * HEAD_DIM:(g + 1) * HEAD_DIM]
        vg = jnp.concatenate([vals[:, g * HEAD_DIM:(g + 1) * HEAD_DIM], ones_col], axis=1)
        s = lax.dot_general(qg, kg, (((1,), (1,)), ((), ())), preferred_element_type=F32)
        s = s.reshape(q_per_kv, nq, s_len)
        if bias is not None:
            s = s + bias[None]
        m = jnp.maximum(jnp.max(s, axis=-1, keepdims=True), sink)
        p = jnp.exp2(s - m).astype(BF16).reshape(q_per_kv * nq, s_len)
        acc = jnp.dot(p, vg, preferred_element_type=F32).reshape(q_per_kv, nq, 2 * HEAD_DIM)
        denom = acc[:, :, HEAD_DIM:HEAD_DIM + 1] + jnp.exp2(sink - m)
        out = acc[:, :, :HEAD_DIM] / denom
        for h in range(q_per_kv):
            pending.append(out[h])
            if len(pending) == LANES // HEAD_DIM:
                lo = (g * q_per_kv + h + 1) * HEAD_DIM - LANES
                o_ref[:, lo:lo + LANES] = jnp.concatenate(pending, axis=1).astype(o_ref.dtype)
                pending = []


def _attn_kernel(sink_ref, q_ref, k_ref, v_ref, kc_ref, vc_ref, o_ref, *, seq_len, n_heads, q_per_kv):
    span = Q_BLOCK + 2 * WINDOW
    ctx_len = kc_ref.shape[0]
    start = pl.program_id(1) * Q_BLOCK
    kstart = pl.multiple_of(jnp.clip(start - WINDOW, 0, seq_len - span), Q_BLOCK)
    qpos = start + lax.broadcasted_iota(jnp.int32, (Q_BLOCK, span + ctx_len), 0)
    col = lax.broadcasted_iota(jnp.int32, (Q_BLOCK, span + ctx_len), 1)
    visible = (jnp.abs(qpos - (kstart + col)) <= WINDOW) | (col >= span)
    bias = jnp.where(visible, 0.0, -1e30).astype(F32)
    keys = jnp.concatenate([k_ref[pl.ds(kstart, span), :], kc_ref[...]], axis=0)
    vals = jnp.concatenate([v_ref[pl.ds(kstart, span), :], vc_ref[...]], axis=0)
    _attend(q_ref, sink_ref, keys, vals, bias, o_ref, n_heads, q_per_kv)


def window_attention(qkv, qkv_ctx, sink, batch, seq_len, ctx_len, d_q, d_kv):
    n_heads = d_q // HEAD_DIM
    nqb = seq_len // Q_BLOCK
    kcol, vcol = d_q // d_kv, d_q // d_kv + 1
    return pl.pallas_call(
        functools.partial(_attn_kernel, seq_len=seq_len, n_heads=n_heads,
                          q_per_kv=n_heads // N_KV_HEADS),
        out_shape=jax.ShapeDtypeStruct((batch * seq_len, d_q), BF16),
        grid=(batch, nqb),
        in_specs=[pl.BlockSpec(memory_space=pltpu.SMEM),
                  pl.BlockSpec((Q_BLOCK, d_q), lambda b, i: (b * nqb + i, 0)),
                  pl.BlockSpec((seq_len, d_kv), lambda b, i: (b, kcol)),
                  pl.BlockSpec((seq_len, d_kv), lambda b, i: (b, vcol)),
                  pl.BlockSpec((ctx_len, d_kv), lambda b, i: (b, kcol)),
                  pl.BlockSpec((ctx_len, d_kv), lambda b, i: (b, vcol))],
        out_specs=pl.BlockSpec((Q_BLOCK, d_q), lambda b, i: (b * nqb + i, 0)),
        compiler_params=_params("arbitrary", "arbitrary"),
        name="window_attention",
    )(sink, qkv, qkv, qkv, qkv_ctx, qkv_ctx)


def _ctx_attn_kernel(sink_ref, q_ref, kc_ref, vc_ref, o_ref, *, n_heads, q_per_kv, scale):
    _attend(q_ref, sink_ref, kc_ref[...], vc_ref[...], None, o_ref, n_heads, q_per_kv,
            q_scale=scale)


def ctx_attention(qkv_ctx, sink, batch, ctx_len, d_q, d_kv):
    n_heads = d_q // HEAD_DIM
    kcol, vcol = d_q // d_kv, d_q // d_kv + 1
    return pl.pallas_call(
        functools.partial(_ctx_attn_kernel, n_heads=n_heads, q_per_kv=n_heads // N_KV_HEADS,
                          scale=Q_SCALE),
        out_shape=jax.ShapeDtypeStruct((batch * ctx_len, d_q), BF16),
        grid=(batch,),
        in_specs=[pl.BlockSpec(memory_space=pltpu.SMEM),
                  pl.BlockSpec((ctx_len, d_q), lambda b: (b, 0)),
                  pl.BlockSpec((ctx_len, d_kv), lambda b: (b, kcol)),
                  pl.BlockSpec((ctx_len, d_kv), lambda b: (b, vcol))],
        out_specs=pl.BlockSpec((ctx_len, d_q), lambda b: (b, 0)),
        compiler_params=_params("arbitrary"),
        name="ctx_attention",
    )(sink, qkv_ctx, qkv_ctx, qkv_ctx)


def _router_kernel(x_ref, sh_ref, sc_ref, wr_ref, br_ref, cnt0_ref, route_ref, cnt_ref, carry_ref):
    @pl.when(pl.program_id(0) == 0)
    def _():
        carry_ref[...] = cnt0_ref[...]

    h = _modulated_norm(x_ref[...], sh_ref[0], sc_ref[0])
    logits = jnp.dot(h, wr_ref[...], precision=HIGHEST, preferred_element_type=F32) + br_ref[...]
    tm = logits.shape[0]
    lane_i = lax.broadcasted_iota(jnp.int32, logits.shape, 1)
    lane = lane_i.astype(F32)
    neg = -jnp.inf
    lg = jnp.where(lane_i < N_GROUPS, logits, neg)
    mg = jnp.max(lg, axis=-1, keepdims=True)
    p_g = 1.0 / jnp.sum(jnp.exp(lg - mg), axis=-1, keepdims=True)
    g_sel = jnp.min(jnp.where(lg == mg, lane, float(ROUTE_W)), axis=-1, keepdims=True)
    lo = N_GROUPS + g_sel * EXPERTS_PER_GROUP
    lf = jnp.where((lane >= lo) & (lane < lo + EXPERTS_PER_GROUP), logits, neg)
    m1 = jnp.max(lf, axis=-1, keepdims=True)
    i1 = jnp.min(jnp.where(lf == m1, lane, float(ROUTE_W)), axis=-1, keepdims=True)
    lf2 = jnp.where(lane == i1, neg, lf)
    m2 = jnp.max(lf2, axis=-1, keepdims=True)
    i2 = jnp.min(jnp.where(lf2 == m2, lane, float(ROUTE_W)), axis=-1, keepdims=True)
    a2 = jnp.exp(m2 - m1)
    gate1 = p_g / (1.0 + a2)
    gate2 = gate1 * a2
    sel1 = lane == i1
    sel2 = lane == i2
    onehot = jnp.where(sel1 | sel2, 1.0, 0.0)
    r_i = lax.broadcasted_iota(jnp.int32, (tm, tm), 0)
    c_i = lax.broadcasted_iota(jnp.int32, (tm, tm), 1)
    tri = jnp.where(r_i > c_i, 1.0, 0.0).astype(BF16)
    before = jnp.dot(tri, onehot.astype(BF16), preferred_element_type=F32) + carry_ref[...]
    rank1 = jnp.sum(jnp.where(sel1, before, 0.0), axis=-1, keepdims=True)
    rank2 = jnp.sum(jnp.where(sel2, before, 0.0), axis=-1, keepdims=True)
    carry_ref[...] += jnp.sum(onehot, axis=0, keepdims=True)
    cnt_ref[...] = carry_ref[...]
    vals = (i1 - N_GROUPS, i2 - N_GROUPS, rank1, rank2, gate1, gate2)
    route = jnp.zeros_like(logits)
    for idx, v in enumerate(vals):
        route = jnp.where(lane_i == idx, v, route)
    route_ref[...] = route


def router(x, sh, sc, wr, br, cnt0, rows_per_seg):
    t, d = x.shape
    tm = _tile(rows_per_seg, 512)
    seg = lambda i: ((i * tm) // rows_per_seg, 0, 0)
    return pl.pallas_call(
        _router_kernel,
        out_shape=(jax.ShapeDtypeStruct((t, ROUTE_W), F32), jax.ShapeDtypeStruct((1, ROUTE_W), F32)),
        grid=(t // tm,),
        in_specs=[pl.BlockSpec((tm, d), lambda i: (i, 0)),
                  pl.BlockSpec((1, 1, d), seg), pl.BlockSpec((1, 1, d), seg),
                  pl.BlockSpec((d, ROUTE_W), lambda i: (0, 0)),
                  pl.BlockSpec((1, ROUTE_W), lambda i: (0, 0)),
                  pl.BlockSpec((1, ROUTE_W), lambda i: (0, 0))],
        out_specs=(pl.BlockSpec((tm, ROUTE_W), lambda i: (i, 0)),
                   pl.BlockSpec((1, ROUTE_W), lambda i: (0, 0))),
        scratch_shapes=[pltpu.VMEM((1, ROUTE_W), F32)],
        compiler_params=_params("arbitrary"),
        name="router",
    )(x, sh, sc, wr, br, cnt0)


def _row_copy(src, src_row, dst, dst_row, sem):
    return pltpu.make_async_copy(src.at[pl.ds(src_row, 1)], dst.at[pl.ds(dst_row, 1)], sem)


def _dispatch_kernel(dest_ref, x_ref, sh_ref, sc_ref, xs_in_ref, xs_ref, h_ref, sem):
    del xs_in_ref
    tm = x_ref.shape[0]
    base = pl.program_id(0) * tm
    h_ref[...] = _modulated_norm(x_ref[...], sh_ref[0], sc_ref[0])

    def issue(r, carry):
        _row_copy(h_ref, r, xs_ref, dest_ref[2 * (base + r)], sem).start()
        _row_copy(h_ref, r, xs_ref, dest_ref[2 * (base + r) + 1], sem).start()
        return carry

    def drain(r, carry):
        _row_copy(h_ref, r, xs_ref, dest_ref[2 * (base + r)], sem).wait()
        _row_copy(h_ref, r, xs_ref, dest_ref[2 * (base + r) + 1], sem).wait()
        return carry

    lax.fori_loop(0, tm, issue, 0)
    lax.fori_loop(0, tm, drain, 0)


def dispatch(dest, x, sh, sc, xs, rows_per_seg):
    t, d = x.shape
    tm = _tile(rows_per_seg, 256)
    seg = lambda i, dest_ref: ((i * tm) // rows_per_seg, 0, 0)
    return pl.pallas_call(
        _dispatch_kernel,
        out_shape=jax.ShapeDtypeStruct(xs.shape, xs.dtype),
        grid_spec=pltpu.PrefetchScalarGridSpec(
            num_scalar_prefetch=1,
            grid=(t // tm,),
            in_specs=[pl.BlockSpec((tm, d), lambda i, dest_ref: (i, 0)),
                      pl.BlockSpec((1, 1, d), seg), pl.BlockSpec((1, 1, d), seg),
                      pl.BlockSpec(memory_space=pl.ANY)],
            out_specs=pl.BlockSpec(memory_space=pl.ANY),
            scratch_shapes=[pltpu.VMEM((tm, d), F32), pltpu.SemaphoreType.DMA]),
        input_output_aliases={4: 0},
        compiler_params=_params("arbitrary"),
        name="moe_dispatch",
    )(dest, x, sh, sc, xs)


def _expert_kernel(bexp_ref, nused_ref, xs_ref, w1_ref, w3_ref, w2_ref, ys_ref, w1b, w3b, w2b):
    i = pl.program_id(0)
    e = bexp_ref[i]
    prev = bexp_ref[jnp.maximum(i - 1, 0)]

    @pl.when((i == 0) | (e != prev))
    def _():
        w1b[...] = w1_ref[0, 0].astype(BF16)
        w3b[...] = w3_ref[0, 0].astype(BF16)
        w2b[...] = w2_ref[0, 0].astype(BF16)

    @pl.when(i < nused_ref[0])
    def _():
        xb = xs_ref[...].astype(BF16)
        h1 = jnp.dot(xb, w1b[...], preferred_element_type=F32)
        h3 = jnp.dot(xb, w3b[...], preferred_element_type=F32)
        a = (h1 * jax.nn.sigmoid(h1) * h3).astype(BF16)
        ys_ref[...] = jnp.dot(a, w2b[...], preferred_element_type=F32)

    @pl.when(i >= nused_ref[0])
    def _():
        ys_ref[...] = jnp.zeros_like(ys_ref)


def experts(block_exp, nused, xs, w1, w3, w2, layer):
    n_slots, d = xs.shape
    de = w1.shape[-1]
    blk = lambda i, bexp, nu: (jnp.minimum(i, nu[0] - 1), 0)
    return pl.pallas_call(
        _expert_kernel,
        out_shape=jax.ShapeDtypeStruct((n_slots, d), F32),
        grid_spec=pltpu.PrefetchScalarGridSpec(
            num_scalar_prefetch=2,
            grid=(n_slots // MOE_BLOCK,),
            in_specs=[pl.BlockSpec((MOE_BLOCK, d), blk),
                      pl.BlockSpec((1, 1, d, de), lambda i, bexp, nu: (layer, bexp[i], 0, 0)),
                      pl.BlockSpec((1, 1, d, de), lambda i, bexp, nu: (layer, bexp[i], 0, 0)),
                      pl.BlockSpec((1, 1, de, d), lambda i, bexp, nu: (layer, bexp[i], 0, 0))],
            out_specs=pl.BlockSpec((MOE_BLOCK, d), lambda i, bexp, nu: (i, 0)),
            scratch_shapes=[pltpu.VMEM((d, de), BF16), pltpu.VMEM((d, de), BF16),
                            pltpu.VMEM((de, d), BF16)]),
        compiler_params=_params("arbitrary"),
        name="moe_experts",
    )(block_exp, nused, xs, w1, w3, w2)


def _combine_kernel(dest_ref, x_ref, route_ref, g_ref, gain_ref, ys_ref, o_ref, y0_ref, y1_ref, sem,
                    *, final):
    tm = x_ref.shape[0]
    base = pl.program_id(0) * tm

    def issue(r, carry):
        _row_copy(ys_ref, dest_ref[2 * (base + r)], y0_ref, r, sem).start()
        _row_copy(ys_ref, dest_ref[2 * (base + r) + 1], y1_ref, r, sem).start()
        return carry

    def drain(r, carry):
        _row_copy(ys_ref, dest_ref[2 * (base + r)], y0_ref, r, sem).wait()
        _row_copy(ys_ref, dest_ref[2 * (base + r) + 1], y1_ref, r, sem).wait()
        return carry

    lax.fori_loop(0, tm, issue, 0)
    lax.fori_loop(0, tm, drain, 0)
    route = route_ref[...]
    y = route[:, 4:5] * y0_ref[...] + route[:, 5:6] * y1_ref[...]
    out = x_ref[...] + g_ref[0] * y
    if final:
        out = out * lax.rsqrt(jnp.mean(out * out, axis=-1, keepdims=True) + EPS) * gain_ref[...]
    o_ref[...] = out


def combine(dest, x, route, gate, gain, ys, rows_per_seg, final):
    t, d = x.shape
    tm = _tile(rows_per_seg, 256)
    return pl.pallas_call(
        functools.partial(_combine_kernel, final=final),
        out_shape=jax.ShapeDtypeStruct((t, d), F32),
        grid_spec=pltpu.PrefetchScalarGridSpec(
            num_scalar_prefetch=1,
            grid=(t // tm,),
            in_specs=[pl.BlockSpec((tm, d), lambda i, dest_ref: (i, 0)),
                      pl.BlockSpec((tm, ROUTE_W), lambda i, dest_ref: (i, 0)),
                      pl.BlockSpec((1, 1, d), lambda i, dest_ref: ((i * tm) // rows_per_seg, 0, 0)),
                      pl.BlockSpec((1, d), lambda i, dest_ref: (0, 0)),
                      pl.BlockSpec(memory_space=pl.ANY)],
            out_specs=pl.BlockSpec((tm, d), lambda i, dest_ref: (i, 0)),
            scratch_shapes=[pltpu.VMEM((tm, d), F32), pltpu.VMEM((tm, d), F32),
                            pltpu.SemaphoreType.DMA]),
        compiler_params=_params("arbitrary"),
        name="moe_combine",
    )(dest, x, route, gate, gain, ys)


def _slot_plan(route_list, counts):
    cnt = counts[0, N_GROUPS:N_GROUPS + N_EXPERTS].astype(jnp.int32)
    padded = (cnt + MOE_BLOCK - 1) // MOE_BLOCK * MOE_BLOCK
    pad_end = jnp.cumsum(padded)
    pad_start = pad_end - padded
    n_assign = 2 * sum(r.shape[0] for r in route_list)
    n_blocks = -(-n_assign // MOE_BLOCK) + N_EXPERTS
    block_start = jnp.arange(n_blocks, dtype=jnp.int32) * MOE_BLOCK
    block_exp = jnp.minimum(jnp.sum(block_start[:, None] >= pad_end[None, :], axis=1),
                            N_EXPERTS - 1).astype(jnp.int32)
    nused = (pad_end[-1:] // MOE_BLOCK).astype(jnp.int32)
    dests = []
    for r in route_list:
        e = r[:, 0:2].astype(jnp.int32)
        dests.append((pad_start[e] + r[:, 2:4].astype(jnp.int32)).reshape(-1))
    return dests, block_exp, nused, n_blocks


def hier_moe_layer(streams, layer, wr, br, w1, w3, w2, gain, final):
    counts = jnp.zeros((1, ROUTE_W), F32)
    routes = []
    for x, sh, sc, _, rps in streams:
        route, counts = router(x, sh, sc, wr, br, counts, rps)
        routes.append(route)
    dests, block_exp, nused, n_blocks = _slot_plan(routes, counts)
    d = streams[0][0].shape[1]
    xs = jnp.zeros((n_blocks * MOE_BLOCK, d), F32)
    for (x, sh, sc, _, rps), dest in zip(streams, dests):
        xs = dispatch(dest, x, sh, sc, xs, rps)
    ys = experts(block_exp, nused, xs, w1, w3, w2, layer)
    return [combine(dest, x, route, g, gain, ys, rps, final)
            for (x, _, _, g, rps), dest, route in zip(streams, dests, routes)]


def kernel(x, c, ctx, c_ctx, mod_w, mod_b, hy_w_in, hy_conv_w, hy_conv_b, hy_f_w1, hy_f_b1,
           hy_f_freq1, hy_f_w2, hy_f_b2, hy_f_freq2, hy_f_w3, hy_bias, hy_w_out,
           at_w_qkv, at_sink, at_w_o, moe_wg, moe_bg, moe_we, moe_be, moe_w1, moe_w3, moe_w2,
           final_gain):
    batch, seq_len, d = x.shape
    ctx_len = ctx.shape[1]
    depth = mod_w.shape[0]
    d_q = at_w_qkv.shape[2] - 2 * N_KV_HEADS * HEAD_DIM
    d_kv = N_KV_HEADS * HEAD_DIM
    n_ctx = batch * ctx_len

    xl = x.reshape(batch * seq_len, d)
    xc = ctx.reshape(n_ctx, d)

    n_cond = batch + 1
    cond = jnp.concatenate([c, c_ctx[None, :], jnp.zeros((-n_cond % 8, d), F32)], axis=0)
    mods = adaln(cond, mod_w, mod_b)

    def mod_slices(i):
        parts = [mods[i, :, j * d:(j + 1) * d] for j in range(6)]
        lat = [p[:batch].reshape(batch, 1, d) for p in parts]
        cx = [p[batch:batch + 1].reshape(1, 1, d) for p in parts]
        return lat, cx

    def use_fft(n):
        return n >= 512 and (2 * n) % (FFT_N1 * 2 * SUB) == 0

    tables = {n: (fft_tables(n) if use_fft(n) else dft_tables(n)) for n in {seq_len, ctx_len}}
    rtabs = rope_tables(seq_len)
    gain = final_gain.reshape(1, d)

    for i in range(depth):
        last = i == depth - 1
        j = i // 2
        (sh1, sc1, g1, sh2, sc2, g2), (sh1c, sc1c, g1c, sh2c, sc2c, g2c) = mod_slices(i)
        if i % 2 == 0:
            w_in = hy_w_in[j].astype(BF16)
            w_out = hy_w_out[j].astype(BF16)
            filt = (hy_f_w1[j], hy_f_b1[j], hy_f_freq1[j], hy_f_w2[j], hy_f_b2[j], hy_f_freq2[j],
                    hy_f_w3[j])
            bias = hy_bias[j].reshape(2, 1, d)

            def mixer(xs_, sh, sc, g, n_seq, rps):
                u = modmm(xs_, sh, sc, w_in, rps)
                u3 = short_conv(u, hy_conv_w[j], hy_conv_b[j], n_seq)
                if use_fft(n_seq):
                    taps = hyena_filter_taps(n_seq, *filt, d, even_odd=False)
                    spec = fft_filter_spectrum(tables[n_seq], *taps)
                    z = hyena_long_convs_fft(u3, tables[n_seq], spec, bias, n_seq, d)
                else:
                    taps = hyena_filter_taps(n_seq, *filt, d, even_odd=True)
                    spec = filter_spectrum(*tables[n_seq], *taps)
                    z = hyena_long_convs(u3, tables[n_seq], spec, bias, n_seq, d)
                return mm_res(z, w_out, xs_, g, rps)

            xl = mixer(xl, sh1, sc1, g1, seq_len, seq_len)
            if not last:
                xc = mixer(xc, sh1c, sc1c, g1c, ctx_len, n_ctx)
        else:
            w_qkv = at_w_qkv[j].astype(BF16)
            w_o = at_w_o[j].astype(BF16)
            sink = at_sink[j]
            qkv = rope_qk(modmm(xl, sh1, sc1, w_qkv, seq_len), rtabs, seq_len, d_q, d_kv)
            qkv_c = modmm(xc, sh1c, sc1c, w_qkv, n_ctx)
            o = window_attention(qkv, qkv_c, sink, batch, seq_len, ctx_len, d_q, d_kv)
            xl = mm_res(o, w_o, xl, g1, seq_len)
            if not last:
                o_c = ctx_attention(qkv_c, sink, batch, ctx_len, d_q, d_kv)
                xc = mm_res(o_c, w_o, xc, g1c, n_ctx)

        wr = jnp.concatenate([moe_wg[i], moe_we[i],
                              jnp.zeros((d, ROUTE_W - N_GROUPS - N_EXPERTS), F32)], axis=1)
        br = jnp.concatenate([moe_bg[i], moe_be[i],
                              jnp.zeros((ROUTE_W - N_GROUPS - N_EXPERTS,), F32)]).reshape(1, ROUTE_W)
        streams = [(xl, sh2, sc2, g2, seq_len)]
        if not last:
            streams.append((xc, sh2c, sc2c, g2c, n_ctx))
        outs = hier_moe_layer(streams, i, wr, br, moe_w1, moe_w3, moe_w2, gain, last)
        xl = outs[0]
        if not last:
            xc = outs[1]

    return xl.reshape(batch, seq_len, d)
```

```python
import functools
import math

import jax
import jax.numpy as jnp
from jax import lax
from jax.experimental import pallas as pl
from jax.experimental.pallas import tpu as pltpu

F32 = jnp.float32
BF16 = jnp.bfloat16
HIGHEST = lax.Precision.HIGHEST

EPS = 1e-6
GRID_W = 64
HEAD_DIM = 64
N_KV_HEADS = 4
WINDOW = 128
Q_BLOCK = 128
ROPE_PAIRS = HEAD_DIM // 4
ROPE_BASE = 10000.0
N_GROUPS = 8
EXPERTS_PER_GROUP = 8
N_EXPERTS = N_GROUPS * EXPERTS_PER_GROUP
MOE_BLOCK = 256
HY_EMB = 33
HY_BANDS = (HY_EMB - 1) // 2
HY_FAST_DECAY_PCT = 0.3
HY_SLOW_DECAY_PCT = 1.5
HY_DECAY_TARGET = 1e-2

LANES = 128
ROUTE_W = LANES
VMEM_LIMIT = 56 * 1024 * 1024


def _tile(n, pref):
    if n <= pref:
        return n
    t = pref
    while n % t:
        t -= 1
    return t


def _params(*sem):
    return pltpu.CompilerParams(dimension_semantics=sem, vmem_limit_bytes=VMEM_LIMIT)


def _modulated_norm(x, sh, sc):
    y = x * lax.rsqrt(jnp.mean(x * x, axis=-1, keepdims=True) + EPS)
    return y * (1.0 + sc) + sh


def _adaln_kernel(c_ref, w_ref, b_ref, o_ref):
    c = c_ref[...]
    a = c * jax.nn.sigmoid(c)
    o_ref[0] = jnp.dot(a, w_ref[0], precision=HIGHEST, preferred_element_type=F32) + b_ref[0]


def adaln(cond, mod_w, mod_b):
    depth, d, n = mod_w.shape
    r = cond.shape[0]
    tn = _tile(n, 1024)
    return pl.pallas_call(
        _adaln_kernel,
        out_shape=jax.ShapeDtypeStruct((depth, r, n), F32),
        grid=(depth, n // tn),
        in_specs=[pl.BlockSpec((r, d), lambda l, j: (0, 0)),
                  pl.BlockSpec((1, d, tn), lambda l, j: (l, 0, j)),
                  pl.BlockSpec((1, 1, tn), lambda l, j: (l, 0, j))],
        out_specs=pl.BlockSpec((1, r, tn), lambda l, j: (l, 0, j)),
        compiler_params=_params("arbitrary", "arbitrary"),
        name="adaln",
    )(cond, mod_w, mod_b.reshape(depth, 1, n))


def _modmm_kernel(x_ref, sh_ref, sc_ref, w_ref, *rest, rope):
    if rope is None:
        o_ref, h_ref = rest
    else:
        cos_ref, sa_ref, sb_ref, o_ref, h_ref = rest

    @pl.when(pl.program_id(1) == 0)
    def _():
        h_ref[...] = _modulated_norm(x_ref[...], sh_ref[0], sc_ref[0]).astype(BF16)

    acc = jnp.dot(h_ref[...], w_ref[...], preferred_element_type=F32)
    if rope is None:
        o_ref[...] = acc.astype(o_ref.dtype)
        return
    d_q, d_qk = rope
    tn = acc.shape[1]
    for c in range(tn // LANES):
        col0 = pl.program_id(1) * tn + c * LANES
        x = acc[:, c * LANES:(c + 1) * LANES]

        @pl.when(col0 < d_qk)
        def _(x=x, c=c, col0=col0):
            y = (x * cos_ref[...] + pltpu.roll(x, LANES - ROPE_PAIRS, 1) * sa_ref[...]
                 + pltpu.roll(x, ROPE_PAIRS, 1) * sb_ref[...])
            y = y * jnp.where(col0 < d_q, Q_SCALE, 1.0).astype(F32)
            o_ref[:, c * LANES:(c + 1) * LANES] = y.astype(o_ref.dtype)

        @pl.when(col0 >= d_qk)
        def _(x=x, c=c):
            o_ref[:, c * LANES:(c + 1) * LANES] = x.astype(o_ref.dtype)


def modmm(x, sh, sc, w, rows_per_seg, rope=None, rope_tabs=None):
    t, d = x.shape
    n = w.shape[1]
    tm = _tile(rows_per_seg, 1024)
    tn = _tile(n, 512)
    seg = lambda i, j: ((i * tm) // rows_per_seg, 0, 0)
    in_specs = [pl.BlockSpec((tm, d), lambda i, j: (i, 0)),
                pl.BlockSpec((1, 1, d), seg),
                pl.BlockSpec((1, 1, d), seg),
                pl.BlockSpec((d, tn), lambda i, j: (0, j))]
    args = [x, sh, sc, w]
    if rope is not None:
        nb = rows_per_seg // tm
        in_specs += [pl.BlockSpec((tm, LANES), lambda i, j: (i % nb, 0))] * 3
        args += list(rope_tabs)
    return pl.pallas_call(
        functools.partial(_modmm_kernel, rope=rope),
        out_shape=jax.ShapeDtypeStruct((t, n), BF16),
        grid=(t // tm, n // tn),
        in_specs=in_specs,
        out_specs=pl.BlockSpec((tm, tn), lambda i, j: (i, j)),
        scratch_shapes=[pltpu.VMEM((tm, d), BF16)],
        compiler_params=_params("arbitrary", "arbitrary"),
        name="modmm",
    )(*args)


def _mmres_kernel(a_ref, w_ref, r_ref, g_ref, o_ref):
    acc = jnp.dot(a_ref[...], w_ref[...], preferred_element_type=F32)
    o_ref[...] = r_ref[...] + g_ref[0] * acc


def mm_res(a, w, res, gate, rows_per_seg):
    t, k = a.shape
    n = w.shape[1]
    tm = _tile(rows_per_seg, 1024)
    tn = _tile(n, 512)
    return pl.pallas_call(
        _mmres_kernel,
        out_shape=jax.ShapeDtypeStruct((t, n), F32),
        grid=(t // tm, n // tn),
        in_specs=[pl.BlockSpec((tm, k), lambda i, j: (i, 0)),
                  pl.BlockSpec((k, tn), lambda i, j: (0, j)),
                  pl.BlockSpec((tm, tn), lambda i, j: (i, j)),
                  pl.BlockSpec((1, 1, tn), lambda i, j: ((i * tm) // rows_per_seg, 0, j))],
        out_specs=pl.BlockSpec((tm, tn), lambda i, j: (i, j)),
        compiler_params=_params("arbitrary", "arbitrary"),
        name="mm_res",
    )(a, w, res, gate)


def _sconv_kernel(u_ref, w_ref, b_ref, o_ref):
    x = u_ref[...].astype(F32)
    n = x.shape[0]
    row = lax.broadcasted_iota(jnp.int32, x.shape, 0)
    prev = jnp.where(row == 0, 0.0, pltpu.roll(x, 1, 0))
    nxt = jnp.where(row == n - 1, 0.0, pltpu.roll(x, n - 1, 0))
    y = prev * w_ref[0:1, :] + x * w_ref[1:2, :] + nxt * w_ref[2:3, :] + b_ref[...]
    o_ref[...] = y.astype(o_ref.dtype)


def short_conv(u, w, b, seq_len):
    t, c = u.shape
    tc = _tile(c, 256)
    return pl.pallas_call(
        _sconv_kernel,
        out_shape=jax.ShapeDtypeStruct((t, c), BF16),
        grid=(t // seq_len, c // tc),
        in_specs=[pl.BlockSpec((seq_len, tc), lambda s, j: (s, j)),
                  pl.BlockSpec((3, tc), lambda s, j: (0, j)),
                  pl.BlockSpec((1, tc), lambda s, j: (0, j))],
        out_specs=pl.BlockSpec((seq_len, tc), lambda s, j: (s, j)),
        compiler_params=_params("arbitrary", "arbitrary"),
        name="short_conv",
    )(u, w, b.reshape(1, c))


def _filter_mlp_kernel(feat_ref, w1_ref, b1_ref, f1_ref, w2_ref, b2_ref, f2_ref, h_ref):
    h = jnp.sin(f1_ref[...] * (jnp.dot(feat_ref[...], w1_ref[...], precision=HIGHEST,
                                       preferred_element_type=F32) + b1_ref[...]))
    h_ref[...] = jnp.sin(f2_ref[...] * (jnp.dot(h, w2_ref[...], precision=HIGHEST,
                                                preferred_element_type=F32) + b2_ref[...]))


def _filter_kernel(hf_ref, hb_ref, tf_ref, tb_ref, w3f_ref, w3b_ref, dl_ref, p_ref, q_ref, *,
                   even_odd):
    dl = jnp.abs(dl_ref[...])
    hf = jnp.dot(hf_ref[...], w3f_ref[...], precision=HIGHEST,
                 preferred_element_type=F32) * jnp.exp(-tf_ref[...] * dl)
    hb = jnp.dot(hb_ref[...], w3b_ref[...], precision=HIGHEST,
                 preferred_element_type=F32) * jnp.exp(-tb_ref[...] * dl)
    row = lax.broadcasted_iota(jnp.int32, hf.shape, 0)
    hb = jnp.where(row == 0, 0.0, hb)
    s = lax.rsqrt(jnp.sum(hf * hf + hb * hb, axis=0, keepdims=True) + EPS)
    p, q = ((hf + hb) * s, (hf - hb) * s) if even_odd else (hf * s, hb * s)
    p_ref[0] = p.astype(p_ref.dtype).reshape(p_ref.shape[1:])
    q_ref[0] = q.astype(q_ref.dtype).reshape(q_ref.shape[1:])


def hyena_filter_taps(seq_len, w1, b1, f1, w2, b2, f2, w3, d, even_odd):
    t = jnp.linspace(0.0, 1.0, seq_len, dtype=F32)[:, None]
    omega = (2.0 * math.pi / seq_len) * jnp.arange(seq_len, dtype=F32)[:, None]
    bands = jnp.linspace(1e-4, HY_BANDS - 1, HY_BANDS, dtype=F32)[None, :]
    feats = jnp.concatenate([t, jnp.cos(bands * omega), -jnp.sin(bands * omega),
                             jnp.zeros((seq_len, LANES - HY_EMB), F32)], axis=-1)
    if not even_odd:
        feats = jnp.concatenate([feats, feats[:1], feats[:0:-1]], axis=0)
    w1p = jnp.concatenate([w1, jnp.zeros((LANES - HY_EMB, w1.shape[1]), F32)], axis=0)
    fw = w1.shape[1]
    rows = feats.shape[0]
    vec = lambda a: a.reshape(1, fw)
    h = pl.pallas_call(
        _filter_mlp_kernel,
        out_shape=jax.ShapeDtypeStruct((rows, fw), F32),
        compiler_params=_params(),
        name="hyena_filter_mlp",
    )(feats, w1p, vec(b1), vec(f1), w2, vec(b2), vec(f2))
    tcol = feats[:, 0:1]
    max_decay = math.log(HY_DECAY_TARGET) / HY_FAST_DECAY_PCT
    min_decay = math.log(HY_DECAY_TARGET) / HY_SLOW_DECAY_PCT
    deltas = jnp.linspace(min_decay, max_decay, d, dtype=F32).reshape(1, d)
    tn = _tile(d, 256)
    nb = d // tn
    bsel = 0 if even_odd else 1
    if even_odd:
        oshape, oblock = (2, seq_len, d), (1, seq_len, tn)
        omap = lambda o, j: (o, 0, j)
    else:
        n2 = 2 * seq_len // FFT_N1
        oshape, oblock = (2, seq_len // n2, n2, d), (1, seq_len // n2, n2, tn)
        omap = lambda o, j: (o, 0, 0, j)
    out = jax.ShapeDtypeStruct(oshape, BF16)
    return pl.pallas_call(
        functools.partial(_filter_kernel, even_odd=even_odd),
        out_shape=(out, out),
        grid=(2, nb),
        in_specs=[pl.BlockSpec((seq_len, fw), lambda o, j: (0, 0)),
                  pl.BlockSpec((seq_len, fw), lambda o, j: (bsel, 0)),
                  pl.BlockSpec((seq_len, 1), lambda o, j: (0, 0)),
                  pl.BlockSpec((seq_len, 1), lambda o, j: (bsel, 0)),
                  pl.BlockSpec((fw, tn), lambda o, j: (0, (2 * o) * nb + j)),
                  pl.BlockSpec((fw, tn), lambda o, j: (0, (2 * o + 1) * nb + j)),
                  pl.BlockSpec((1, tn), lambda o, j: (0, j))],
        out_specs=(pl.BlockSpec(oblock, omap), pl.BlockSpec(oblock, omap)),
        compiler_params=_params("arbitrary", "arbitrary"),
        name="hyena_filter",
    )(h, h, tcol, tcol, w3, w3, deltas)


def _alt_sum(x):
    row = lax.broadcasted_iota(jnp.int32, x.shape, 0)
    return jnp.sum(jnp.where((row & 1) == 1, -x, x), axis=0, keepdims=True)


def _spectrum_kernel(c_ref, s_ref, e_ref, o_ref, kr_ref, km_ref):
    kr_ref[0] = jnp.dot(c_ref[...], e_ref[0], preferred_element_type=F32)
    km = jnp.dot(s_ref[...], o_ref[0], preferred_element_type=F32)

    @pl.when(pl.program_id(2) != 0)
    def _():
        km_ref[0] = km

    @pl.when(pl.program_id(2) == 0)
    def _():
        row = lax.broadcasted_iota(jnp.int32, km.shape, 0)
        km_ref[0] = jnp.where(row == 0, _alt_sum(e_ref[0].astype(F32)), km)


def filter_spectrum(cmat, smat, e_taps, o_taps):
    _, n, d = e_taps.shape
    tk = _tile(n, 512)
    tn = _tile(d, 512)
    out = jax.ShapeDtypeStruct((2, n, d), F32)
    return pl.pallas_call(
        _spectrum_kernel,
        out_shape=(out, out),
        grid=(2, d // tn, n // tk),
        in_specs=[pl.BlockSpec((tk, n), lambda o, j, k: (k, 0)),
                  pl.BlockSpec((tk, n), lambda o, j, k: (k, 0)),
                  pl.BlockSpec((1, n, tn), lambda o, j, k: (o, 0, j)),
                  pl.BlockSpec((1, n, tn), lambda o, j, k: (o, 0, j))],
        out_specs=(pl.BlockSpec((1, tk, tn), lambda o, j, k: (o, k, j)),
                   pl.BlockSpec((1, tk, tn), lambda o, j, k: (o, k, j))),
        compiler_params=_params("arbitrary", "arbitrary", "arbitrary"),
        name="filter_spectrum",
    )(cmat, smat, e_taps, o_taps)


def _dft_fwd_kernel(c_ref, s_ref, u_ref, kr_ref, km_ref, yr_ref, q_ref):
    u = u_ref[...]
    a = jnp.dot(c_ref[...], u, preferred_element_type=F32)
    b = jnp.dot(s_ref[...], u, preferred_element_type=F32)
    kr = kr_ref[0]
    km = km_ref[0]
    yr = a * kr - b * km
    q = b * kr + a * km

    @pl.when(pl.program_id(2) != 0)
    def _():
        yr_ref[...] = yr.astype(yr_ref.dtype)
        q_ref[...] = q.astype(q_ref.dtype)

    @pl.when(pl.program_id(2) == 0)
    def _():
        row = lax.broadcasted_iota(jnp.int32, yr.shape, 0)
        ynyq = _alt_sum(u.astype(F32)) * km[0:1, :]
        yr_ref[...] = jnp.where(row == 0, 0.5 * yr, yr).astype(yr_ref.dtype)
        q_ref[...] = jnp.where(row == 0, 0.5 * ynyq, q).astype(q_ref.dtype)


def dft_fwd(cmat, smat, u, col0, kr, km, order, seq_len, d):
    t = u.shape[0]
    tk = _tile(seq_len, 512)
    tn = _tile(d, 512)
    cb = col0 // tn
    out = jax.ShapeDtypeStruct((t, d), BF16)
    nk = seq_len // tk
    return pl.pallas_call(
        _dft_fwd_kernel,
        out_shape=(out, out),
        grid=(t // seq_len, d // tn, nk),
        in_specs=[pl.BlockSpec((tk, seq_len), lambda b, j, k: (k, 0)),
                  pl.BlockSpec((tk, seq_len), lambda b, j, k: (k, 0)),
                  pl.BlockSpec((seq_len, tn), lambda b, j, k: (b, cb + j)),
                  pl.BlockSpec((1, tk, tn), lambda b, j, k: (order, k, j)),
                  pl.BlockSpec((1, tk, tn), lambda b, j, k: (order, k, j))],
        out_specs=(pl.BlockSpec((tk, tn), lambda b, j, k: (b * nk + k, j)),
                   pl.BlockSpec((tk, tn), lambda b, j, k: (b * nk + k, j))),
        compiler_params=_params("arbitrary", "arbitrary", "arbitrary"),
        name="dft_fwd",
    )(cmat, smat, u, kr, km)


def _dft_inv_kernel(c_ref, s_ref, yr_ref, q_ref, u_ref, g_ref, bias_ref, o_ref, *, seq_len):
    acc = jnp.dot(c_ref[...], yr_ref[...], preferred_element_type=F32)
    acc += jnp.dot(s_ref[...], q_ref[...], preferred_element_type=F32)
    tt = acc.shape[0]
    t_idx = pl.program_id(2) * tt + lax.broadcasted_iota(jnp.int32, acc.shape, 0)
    half_nyq = q_ref[0:1, :].astype(F32)
    acc += jnp.where((t_idx & 1) == 1, -half_nyq, half_nyq)
    u = u_ref[...].astype(F32)
    y = acc * (1.0 / seq_len) + u * bias_ref[0]
    o_ref[...] = (g_ref[...].astype(F32) * y).astype(o_ref.dtype)


def dft_inv(cmat, smat, yr, q, u, ucol0, gsrc, gcol0, bias, order, seq_len, d):
    t = yr.shape[0]
    tt = _tile(seq_len, 512)
    tn = _tile(d, 512)
    ub, gb = ucol0 // tn, gcol0 // tn
    nt = seq_len // tt
    return pl.pallas_call(
        functools.partial(_dft_inv_kernel, seq_len=seq_len),
        out_shape=jax.ShapeDtypeStruct((t, d), BF16),
        grid=(t // seq_len, d // tn, nt),
        in_specs=[pl.BlockSpec((tt, seq_len), lambda b, j, k: (k, 0)),
                  pl.BlockSpec((tt, seq_len), lambda b, j, k: (k, 0)),
                  pl.BlockSpec((seq_len, tn), lambda b, j, k: (b, j)),
                  pl.BlockSpec((seq_len, tn), lambda b, j, k: (b, j)),
                  pl.BlockSpec((tt, tn), lambda b, j, k: (b * nt + k, ub + j)),
                  pl.BlockSpec((tt, tn), lambda b, j, k: (b * nt + k, gb + j)),
                  pl.BlockSpec((1, 1, tn), lambda b, j, k: (order, 0, j))],
        out_specs=pl.BlockSpec((tt, tn), lambda b, j, k: (b * nt + k, j)),
        compiler_params=_params("arbitrary", "arbitrary", "arbitrary"),
        name="dft_inv",
    )(cmat, smat, yr, q, u, gsrc, bias)


def dft_tables(seq_len):
    n = jnp.arange(seq_len, dtype=jnp.int32)
    ang = ((n[:, None] * n[None, :]) % (2 * seq_len)).astype(F32) * (math.pi / seq_len)
    return jnp.cos(ang).astype(BF16), jnp.sin(ang).astype(BF16)


def hyena_long_convs(u3, tables, spectrum, bias, seq_len, d):
    cmat, smat = tables
    kr, km = spectrum
    yr, q = dft_fwd(cmat, smat, u3, 0, kr, km, 0, seq_len, d)
    z = dft_inv(cmat, smat, yr, q, u3, 0, u3, d, bias, 0, seq_len, d)
    yr, q = dft_fwd(cmat, smat, z, 0, kr, km, 1, seq_len, d)
    return dft_inv(cmat, smat, yr, q, z, 0, u3, 2 * d, bias, 1, seq_len, d)


FFT_N1 = 64
SUB = 8


def fft_tables(seq_len):
    n = 2 * seq_len
    n1, n2 = FFT_N1, n // FFT_N1
    two_pi = 2.0 * math.pi
    eye = jnp.eye(SUB, dtype=F32)
    k1 = jnp.arange(n1, dtype=jnp.int32)
    h1 = jnp.arange(n1 // 2, dtype=jnp.int32)
    ang = ((k1[:, None] * h1[None, :]) % n1).astype(F32) * (two_pi / n1)
    ca = jnp.stack([jnp.cos(ang), -jnp.sin(ang)], axis=1)
    ka = jnp.einsum('krn,jm->krjnm', ca, eye).reshape(n1 * 2 * SUB, (n1 // 2) * SUB)
    c2 = jnp.stack([jnp.cos(ang.T), -jnp.sin(ang.T)], axis=2)
    ka2 = jnp.einsum('tkr,jm->tjkrm', c2, eye).reshape((n1 // 2) * SUB, n1 * 2 * SUB)
    k2 = jnp.arange(n2 // 2, dtype=jnp.int32)
    m2 = jnp.arange(n2, dtype=jnp.int32)
    k = k1[:, None, None] + n1 * k2[None, :, None]
    th = ((k * m2[None, None, :]) % n).astype(F32) * (two_pi / n)
    c, s = jnp.cos(th), jnp.sin(th)
    fb = jnp.concatenate([jnp.concatenate([c, s], axis=2),
                          jnp.concatenate([-s, c], axis=2)], axis=1)
    ct, st = jnp.swapaxes(c, 1, 2), jnp.swapaxes(s, 1, 2)
    gb = jnp.concatenate([jnp.concatenate([ct, -st], axis=2),
                          jnp.concatenate([st, ct], axis=2)], axis=1)
    return tuple(a.astype(BF16) for a in (ka, fb, gb, ka2))


def _fft_stage_a(src_ref, ka_ref, a_ref):
    n1h, n2, tn = src_ref.shape
    for g in range(n2 // SUB):
        rhs = src_ref[:, g * SUB:(g + 1) * SUB, :].reshape(n1h * SUB, tn).astype(BF16)
        out = jnp.dot(ka_ref[...], rhs, preferred_element_type=F32)
        a_ref[:, :, g * SUB:(g + 1) * SUB, :] = out.reshape(FFT_N1, 2, SUB, tn)


def _fft_stage_b(fb_ref, a_ref, k1):
    _, _, n2, tn = a_ref.shape
    slab = a_ref[k1].reshape(2 * n2, tn).astype(BF16)
    return jnp.dot(fb_ref[k1], slab, preferred_element_type=F32)


def _block_alt_sum(src_ref):
    return _alt_sum(jnp.sum(src_ref[...], axis=0))


def _fftspec_kernel(tf_ref, tb_ref, ka_ref, fb_ref, kf_ref, knyq_ref, src_ref, a_ref):
    h = a_ref.shape[2] // 2
    tn = a_ref.shape[3]
    nyq = None
    for which, t_ref in enumerate((tf_ref, tb_ref)):
        src_ref[...] = t_ref[0].astype(F32)
        part = _block_alt_sum(src_ref)
        nyq = part if nyq is None else nyq + part
        _fft_stage_a(src_ref, ka_ref, a_ref)

        def body(k1, carry, which=which):
            x = _fft_stage_b(fb_ref, a_ref, k1).reshape(2, h, tn)
            if which == 0:
                kf_ref[0, k1] = x
            else:
                sign = 1.0 - 2.0 * jnp.asarray(k1 & 1, F32)
                kf_ref[0, k1] = kf_ref[0, k1] + sign * x
            return carry

        lax.fori_loop(0, FFT_N1, body, 0)
    knyq_ref[0] = nyq
    dc = kf_ref[0, 0]
    row = lax.broadcasted_iota(jnp.int32, dc.shape, 1)
    kf_ref[0, 0] = jnp.where(row == 0, 0.5 * dc, dc)


def _const_spec(shape):
    return pl.BlockSpec(shape, lambda *_: (0,) * len(shape), pipeline_mode=pl.Buffered(1))


def fft_filter_spectrum(tabs, taps_f, taps_b):
    ka, fb, _, _ = tabs
    _, n1h, n2, d = taps_f.shape
    tn = _tile(d, 256)
    tap = pl.BlockSpec((1, n1h, n2, tn), lambda o, j: (o, 0, 0, j))
    return pl.pallas_call(
        _fftspec_kernel,
        out_shape=(jax.ShapeDtypeStruct((2, FFT_N1, 2, n2 // 2, d), F32),
                   jax.ShapeDtypeStruct((2, 1, d), F32)),
        grid=(2, d // tn),
        in_specs=[tap, tap, _const_spec(ka.shape), _const_spec(fb.shape)],
        out_specs=(pl.BlockSpec((1, FFT_N1, 2, n2 // 2, tn), lambda o, j: (o, 0, 0, 0, j)),
                   pl.BlockSpec((1, 1, tn), lambda o, j: (o, 0, j))),
        scratch_shapes=[pltpu.VMEM((n1h, n2, tn), F32), pltpu.VMEM((FFT_N1, 2, n2, tn), F32)],
        compiler_params=_params("arbitrary", "arbitrary"),
        name="fft_filter_spectrum",
    )(taps_f, taps_b, ka, fb)


def _fftconv_kernel(u_ref, g_ref, bias_ref, kf_ref, knyq_ref, ka_ref, fb_ref, gb_ref, ka2_ref,
                    o_ref, src_ref, a_ref):
    n1h, n2, tn = src_ref.shape
    h = n2 // 2
    seq_len = n1h * n2
    src_ref[...] = u_ref[...].astype(F32)
    ynyq = _block_alt_sum(src_ref) * knyq_ref[0] * (0.5 / seq_len)
    _fft_stage_a(src_ref, ka_ref, a_ref)

    def body(k1, carry):
        x = _fft_stage_b(fb_ref, a_ref, k1)
        xr, xi = x[:h], x[h:]
        kr, ki = kf_ref[0, k1, 0], kf_ref[0, k1, 1]
        y = jnp.concatenate([xr * kr - xi * ki, xr * ki + xi * kr], axis=0).astype(BF16)
        z = jnp.dot(gb_ref[k1], y, preferred_element_type=F32)
        a_ref[k1] = z.reshape(2, n2, tn)
        return carry

    lax.fori_loop(0, FFT_N1, body, 0, unroll=4)
    bias = bias_ref[0]
    pair = 2 * SUB
    row = lax.broadcasted_iota(jnp.int32, (n1h, pair, tn), 1)
    nyq = jnp.where((row & 1) == 1, -ynyq, ynyq)
    for gg in range(n2 // pair):
        parts = []
        for g in (2 * gg, 2 * gg + 1):
            zg = a_ref[:, :, g * SUB:(g + 1) * SUB, :].reshape(FFT_N1 * 2 * SUB, tn).astype(BF16)
            acc = jnp.dot(ka2_ref[...], zg, preferred_element_type=F32)
            parts.append(acc.reshape(n1h, SUB, tn))
        acc = jnp.concatenate(parts, axis=1)
        rows = slice(gg * pair, (gg + 1) * pair)
        y = acc * (1.0 / seq_len) + nyq + src_ref[:, rows, :] * bias
        o_ref[:, rows, :] = (g_ref[:, rows, :].astype(F32) * y).astype(o_ref.dtype)


def fft_long_conv(tabs, spec, u, ucol0, gsrc, gcol0, bias, order, seq_len, d):
    ka, fb, gb, ka2 = tabs
    kf, knyq = spec
    nb, n2, _ = u.shape
    n1h = seq_len // n2
    tn = _tile(d, 256)
    ub, gb_ = ucol0 // tn, gcol0 // tn
    return pl.pallas_call(
        _fftconv_kernel,
        out_shape=jax.ShapeDtypeStruct((nb, n2, d), BF16),
        grid=(d // tn, nb // n1h),
        in_specs=[pl.BlockSpec((n1h, n2, tn), lambda j, b: (b, 0, ub + j)),
                  pl.BlockSpec((n1h, n2, tn), lambda j, b: (b, 0, gb_ + j)),
                  pl.BlockSpec((1, 1, tn), lambda j, b: (order, 0, j)),
                  pl.BlockSpec((1, FFT_N1, 2, n2 // 2, tn), lambda j, b: (order, 0, 0, 0, j),
                               pipeline_mode=pl.Buffered(1)),
                  pl.BlockSpec((1, 1, tn), lambda j, b: (order, 0, j)),
                  _const_spec(ka.shape), _const_spec(fb.shape), _const_spec(gb.shape),
                  _const_spec(ka2.shape)],
        out_specs=pl.BlockSpec((n1h, n2, tn), lambda j, b: (b, 0, j)),
        scratch_shapes=[pltpu.VMEM((n1h, n2, tn), F32), pltpu.VMEM((FFT_N1, 2, n2, tn), F32)],
        compiler_params=_params("arbitrary", "arbitrary"),
        name="fft_long_conv",
    )(u, gsrc, bias, kf, knyq, ka, fb, gb, ka2)


def hyena_long_convs_fft(u3, tabs, spec, bias, seq_len, d):
    t = u3.shape[0]
    n2 = 2 * seq_len // FFT_N1
    u3b = u3.reshape(t // n2, n2, 3 * d)
    z = fft_long_conv(tabs, spec, u3b, 0, u3b, d, bias, 0, seq_len, d)
    z = fft_long_conv(tabs, spec, z, 0, u3b, 2 * d, bias, 1, seq_len, d)
    return z.reshape(t, d)


def rope_tables(seq_len):
    pos = jnp.arange(seq_len, dtype=jnp.int32)
    r = (pos // GRID_W).astype(F32)[:, None]
    col = (pos % GRID_W).astype(F32)[:, None]
    inv = ROPE_BASE ** (-(2.0 * jnp.arange(ROPE_PAIRS, dtype=F32)) / (2 * ROPE_PAIRS))
    ar, ac = r * inv, col * inv
    zeros = jnp.zeros_like(ar)
    cos = jnp.concatenate([jnp.cos(ar), jnp.cos(ar), jnp.cos(ac), jnp.cos(ac)], axis=1)
    sa = jnp.concatenate([-jnp.sin(ar), zeros, -jnp.sin(ac), zeros], axis=1)
    sb = jnp.concatenate([zeros, jnp.sin(ar), zeros, jnp.sin(ac)], axis=1)
    rep = LANES // HEAD_DIM
    return tuple(jnp.tile(a, (1, rep)) for a in (cos, sa, sb))


LOG2E = 1.4426950408889634
Q_SCALE = HEAD_DIM ** -0.5 * LOG2E


def _attend(q_ref, sink_ref, keys, vals, bias, o_ref, n_heads, q_per_kv, q_scale=None):
    nq = q_ref.shape[0]
    s_len = keys.shape[0]
    lane = lax.broadcasted_iota(jnp.int32, (s_len, HEAD_DIM), 1)
    ones_col = jnp.where(lane == 0, 1.0, 0.0).astype(BF16)
    hidx = lax.broadcasted_iota(jnp.int32, (q_per_kv, 1, 1), 0)
    pending = []
    for g in range(n_heads // q_per_kv):
        qs = []
        sink = jnp.zeros((q_per_kv, 1, 1), F32)
        for h in range(q_per_kv):
            col = (g * q_per_kv + h) * HEAD_DIM
            q = q_ref[:, col:col + HEAD_DIM]
            if q_scale is not None:
                q = (q.astype(F32) * q_scale).astype(BF16)
            qs.append(q)
            sink = jnp.where(hidx == h, sink_ref[g * q_per_kv + h] * LOG2E, sink)
        qg = jnp.concatenate(qs, axis=0)
        kg = keys[:, g * HEAD_DIM:(g + 1) * HEAD_DIM]
        vg = jnp.concatenate([vals[:, g * HEAD_DIM:(g + 1) * HEAD_DIM], ones_col], axis=1)
        s = lax.dot_general(qg, kg, (((1,), (1,)), ((), ())), preferred_element_type=F32)
        s = s.reshape(q_per_kv, nq, s_len)
        if bias is not None:
            s = s + bias[None]
        m = jnp.maximum(jnp.max(s, axis=-1, keepdims=True), sink)
        p = jnp.exp2(s - m).astype(BF16).reshape(q_per_kv * nq, s_len)
        acc = jnp.dot(p, vg, preferred_element_type=F32).reshape(q_per_kv, nq, 2 * HEAD_DIM)
        denom = acc[:, :, HEAD_DIM:HEAD_DIM + 1] + jnp.exp2(sink - m)
        out = acc[:, :, :HEAD_DIM] / denom
        for h in range(q_per_kv):
            pending.append(out[h])
            if len(pending) == LANES // HEAD_DIM:
                lo = (g * q_per_kv + h + 1) * HEAD_DIM - LANES
                o_ref[:, lo:lo + LANES] = jnp.concatenate(pending, axis=1).astype(o_ref.dtype)
                pending = []


def _attn_kernel(sink_ref, q_ref, k_ref, v_ref, kc_ref, vc_ref, o_ref, *, seq_len, n_heads, q_per_kv):
    span = Q_BLOCK + 2 * WINDOW
    ctx_len = kc_ref.shape[0]
    start = pl.program_id(1) * Q_BLOCK
    kstart = pl.multiple_of(jnp.clip(start - WINDOW, 0, seq_len - span), Q_BLOCK)
    qpos = start + lax.broadcasted_iota(jnp.int32, (Q_BLOCK, span + ctx_len), 0)
    col = lax.broadcasted_iota(jnp.int32, (Q_BLOCK, span + ctx_len), 1)
    visible = (jnp.abs(qpos - (kstart + col)) <= WINDOW) | (col >= span)
    bias = jnp.where(visible, 0.0, -1e30).astype(F32)
    keys = jnp.concatenate([k_ref[pl.ds(kstart, span), :], kc_ref[...]], axis=0)
    vals = jnp.concatenate([v_ref[pl.ds(kstart, span), :], vc_ref[...]], axis=0)
    _attend(q_ref, sink_ref, keys, vals, bias, o_ref, n_heads, q_per_kv)


def window_attention(qkv, qkv_ctx, sink, batch, seq_len, ctx_len, d_q, d_kv):
    n_heads = d_q // HEAD_DIM
    nqb = seq_len // Q_BLOCK
    kcol, vcol = d_q // d_kv, d_q // d_kv + 1
    return pl.pallas_call(
        functools.partial(_attn_kernel, seq_len=seq_len, n_heads=n_heads,
                          q_per_kv=n_heads // N_KV_HEADS),
        out_shape=jax.ShapeDtypeStruct((batch * seq_len, d_q), BF16),
        grid=(batch, nqb),
        in_specs=[pl.BlockSpec(memory_space=pltpu.SMEM),
                  pl.BlockSpec((Q_BLOCK, d_q), lambda b, i: (b * nqb + i, 0)),
                  pl.BlockSpec((seq_len, d_kv), lambda b, i: (b, kcol)),
                  pl.BlockSpec((seq_len, d_kv), lambda b, i: (b, vcol)),
                  pl.BlockSpec((ctx_len, d_kv), lambda b, i: (b, kcol)),
                  pl.BlockSpec((ctx_len, d_kv), lambda b, i: (b, vcol))],
        out_specs=pl.BlockSpec((Q_BLOCK, d_q), lambda b, i: (b * nqb + i, 0)),
        compiler_params=_params("arbitrary", "arbitrary"),
        name="window_attention",
    )(sink, qkv, qkv, qkv, qkv_ctx, qkv_ctx)


def _ctx_attn_kernel(sink_ref, q_ref, kc_ref, vc_ref, o_ref, *, n_heads, q_per_kv, scale):
    _attend(q_ref, sink_ref, kc_ref[...], vc_ref[...], None, o_ref, n_heads, q_per_kv,
            q_scale=scale)


def ctx_attention(qkv_ctx, sink, batch, ctx_len, d_q, d_kv):
    n_heads = d_q // HEAD_DIM
    kcol, vcol = d_q // d_kv, d_q // d_kv + 1
    return pl.pallas_call(
        functools.partial(_ctx_attn_kernel, n_heads=n_heads, q_per_kv=n_heads // N_KV_HEADS,
                          scale=Q_SCALE),
        out_shape=jax.ShapeDtypeStruct((batch * ctx_len, d_q), BF16),
        grid=(batch,),
        in_specs=[pl.BlockSpec(memory_space=pltpu.SMEM),
                  pl.BlockSpec((ctx_len, d_q), lambda b: (b, 0)),
                  pl.BlockSpec((ctx_len, d_kv), lambda b: (b, kcol)),
                  pl.BlockSpec((ctx_len, d_kv), lambda b: (b, vcol))],
        out_specs=pl.BlockSpec((ctx_len, d_q), lambda b: (b, 0)),
        compiler_params=_params("arbitrary"),
        name="ctx_attention",
    )(sink, qkv_ctx, qkv_ctx, qkv_ctx)


def _router_kernel(x_ref, sh_ref, sc_ref, wr_ref, br_ref, cnt0_ref, route_ref, cnt_ref, carry_ref):
    @pl.when(pl.program_id(0) == 0)
    def _():
        carry_ref[...] = cnt0_ref[...]

    h = _modulated_norm(x_ref[...], sh_ref[0], sc_ref[0])
    logits = jnp.dot(h, wr_ref[...], precision=HIGHEST, preferred_element_type=F32) + br_ref[...]
    tm = logits.shape[0]
    lane_i = lax.broadcasted_iota(jnp.int32, logits.shape, 1)
    lane = lane_i.astype(F32)
    neg = -jnp.inf
    lg = jnp.where(lane_i < N_GROUPS, logits, neg)
    mg = jnp.max(lg, axis=-1, keepdims=True)
    p_g = 1.0 / jnp.sum(jnp.exp(lg - mg), axis=-1, keepdims=True)
    g_sel = jnp.min(jnp.where(lg == mg, lane, float(ROUTE_W)), axis=-1, keepdims=True)
    lo = N_GROUPS + g_sel * EXPERTS_PER_GROUP
    lf = jnp.where((lane >= lo) & (lane < lo + EXPERTS_PER_GROUP), logits, neg)
    m1 = jnp.max(lf, axis=-1, keepdims=True)
    i1 = jnp.min(jnp.where(lf == m1, lane, float(ROUTE_W)), axis=-1, keepdims=True)
    lf2 = jnp.where(lane == i1, neg, lf)
    m2 = jnp.max(lf2, axis=-1, keepdims=True)
    i2 = jnp.min(jnp.where(lf2 == m2, lane, float(ROUTE_W)), axis=-1, keepdims=True)
    a2 = jnp.exp(m2 - m1)
    gate1 = p_g / (1.0 + a2)
    gate2 = gate1 * a2
    sel1 = lane == i1
    sel2 = lane == i2
    onehot = jnp.where(sel1 | sel2, 1.0, 0.0)
    r_i = lax.broadcasted_iota(jnp.int32, (tm, tm), 0)
    c_i = lax.broadcasted_iota(jnp.int32, (tm, tm), 1)
    tri = jnp.where(r_i > c_i, 1.0, 0.0).astype(BF16)
    before = jnp.dot(tri, onehot.astype(BF16), preferred_element_type=F32) + carry_ref[...]
    rank1 = jnp.sum(jnp.where(sel1, before, 0.0), axis=-1, keepdims=True)
    rank2 = jnp.sum(jnp.where(sel2, before, 0.0), axis=-1, keepdims=True)
    carry_ref[...] += jnp.sum(onehot, axis=0, keepdims=True)
    cnt_ref[...] = carry_ref[...]
    vals = (i1 - N_GROUPS, i2 - N_GROUPS, rank1, rank2, gate1, gate2)
    route = jnp.zeros_like(logits)
    for idx, v in enumerate(vals):
        route = jnp.where(lane_i == idx, v, route)
    route_ref[...] = route


def router(x, sh, sc, wr, br, cnt0, rows_per_seg):
    t, d = x.shape
    tm = _tile(rows_per_seg, 512)
    seg = lambda i: ((i * tm) // rows_per_seg, 0, 0)
    return pl.pallas_call(
        _router_kernel,
        out_shape=(jax.ShapeDtypeStruct((t, ROUTE_W), F32), jax.ShapeDtypeStruct((1, ROUTE_W), F32)),
        grid=(t // tm,),
        in_specs=[pl.BlockSpec((tm, d), lambda i: (i, 0)),
                  pl.BlockSpec((1, 1, d), seg), pl.BlockSpec((1, 1, d), seg),
                  pl.BlockSpec((d, ROUTE_W), lambda i: (0, 0)),
                  pl.BlockSpec((1, ROUTE_W), lambda i: (0, 0)),
                  pl.BlockSpec((1, ROUTE_W), lambda i: (0, 0))],
        out_specs=(pl.BlockSpec((tm, ROUTE_W), lambda i: (i, 0)),
                   pl.BlockSpec((1, ROUTE_W), lambda i: (0, 0))),
        scratch_shapes=[pltpu.VMEM((1, ROUTE_W), F32)],
        compiler_params=_params("arbitrary"),
        name="router",
    )(x, sh, sc, wr, br, cnt0)


def _token_copy(src, src_tok, dst, dst_tok, sem, rows):
    return pltpu.make_async_copy(src.at[pl.ds(src_tok * rows, rows)],
                                 dst.at[pl.ds(dst_tok * rows, rows)], sem)


def _to_token_rows(mat, rows_ref):
    tm, d = mat.shape
    rows = d // LANES
    for s in range(rows):
        rows_ref[pl.ds(s, tm, stride=rows), :] = mat[:, s * LANES:(s + 1) * LANES]


def _token_cols(rows_ref, s, tm, rows):
    return rows_ref[pl.ds(s, tm, stride=rows), :]


def _dispatch_kernel(dest_ref, pad_end_ref, nused_ref, x_ref, sh_ref, sc_ref, *rest, first, n_blocks):
    if first:
        xs_ref, h_ref, zero_ref, sem, zsem = rest
    else:
        _, xs_ref, h_ref, zero_ref, sem, zsem = rest
    tm, d = x_ref.shape
    rows = d // LANES
    base = pl.program_id(0) * tm

    if first:
        @pl.when(pl.program_id(0) == 0)
        def _():
            zero_ref[...] = jnp.zeros_like(zero_ref)
            blk_rows = MOE_BLOCK * rows

            def zero_block(blk):
                return pltpu.make_async_copy(zero_ref, xs_ref.at[pl.ds(blk * blk_rows, blk_rows)], zsem)

            def last_blocks(op):
                def body(e, carry):
                    hi = pad_end_ref[e]
                    lo = jnp.where(e == 0, 0, pad_end_ref[jnp.maximum(e - 1, 0)])

                    @pl.when(hi > lo)
                    def _():
                        op(zero_block(hi // MOE_BLOCK - 1))
                    return carry
                lax.fori_loop(0, N_EXPERTS, body, 0)

            def tail_blocks(op):
                def body(blk, carry):
                    op(zero_block(blk))
                    return carry
                lax.fori_loop(nused_ref[0], n_blocks, body, 0)

            last_blocks(lambda c: c.start())
            tail_blocks(lambda c: c.start())
            last_blocks(lambda c: c.wait())
            tail_blocks(lambda c: c.wait())

    _to_token_rows(_modulated_norm(x_ref[...], sh_ref[0], sc_ref[0]), h_ref)

    def issue(r, carry):
        _token_copy(h_ref, r, xs_ref, dest_ref[2 * (base + r)], sem, rows).start()
        _token_copy(h_ref, r, xs_ref, dest_ref[2 * (base + r) + 1], sem, rows).start()
        return carry

    def drain(r, carry):
        _token_copy(h_ref, r, xs_ref, dest_ref[2 * (base + r)], sem, rows).wait()
        _token_copy(h_ref, r, xs_ref, dest_ref[2 * (base + r) + 1], sem, rows).wait()
        return carry

    lax.fori_loop(0, tm, issue, 0)
    lax.fori_loop(0, tm, drain, 0)


def dispatch(dest, pad_end, nused, n_blocks, x, sh, sc, xs, rows_per_seg):
    t, d = x.shape
    rows = d // LANES
    tm = _tile(rows_per_seg, 256)
    first = xs is None
    seg = lambda i, *_: ((i * tm) // rows_per_seg, 0, 0)
    in_specs = [pl.BlockSpec((tm, d), lambda i, *_: (i, 0)),
                pl.BlockSpec((1, 1, d), seg), pl.BlockSpec((1, 1, d), seg)]
    args = [dest, pad_end, nused, x, sh, sc]
    if not first:
        in_specs.append(pl.BlockSpec(memory_space=pl.ANY))
        args.append(xs)
    return pl.pallas_call(
        functools.partial(_dispatch_kernel, first=first, n_blocks=n_blocks),
        out_shape=jax.ShapeDtypeStruct((n_blocks * MOE_BLOCK * rows, LANES), F32),
        grid_spec=pltpu.PrefetchScalarGridSpec(
            num_scalar_prefetch=3,
            grid=(t // tm,),
            in_specs=in_specs,
            out_specs=pl.BlockSpec(memory_space=pl.ANY),
            scratch_shapes=[pltpu.VMEM((tm * rows, LANES), F32),
                            pltpu.VMEM((MOE_BLOCK * rows, LANES), F32),
                            pltpu.SemaphoreType.DMA, pltpu.SemaphoreType.DMA]),
        input_output_aliases={} if first else {6: 0},
        compiler_params=_params("arbitrary"),
        name="moe_dispatch",
    )(*args)


def _expert_kernel(bexp_ref, nused_ref, xs_ref, w1_ref, w3_ref, w2_ref, ys_ref, xb, w13b, w2b):
    i = pl.program_id(0)
    e = bexp_ref[i]
    prev = bexp_ref[jnp.maximum(i - 1, 0)]
    tm, d = xb.shape
    rows = d // LANES
    de = w2b.shape[0]

    @pl.when((i == 0) | (e != prev))
    def _():
        w13b[:, :de] = w1_ref[0, 0].astype(BF16)
        w13b[:, de:] = w3_ref[0, 0].astype(BF16)
        w2b[...] = w2_ref[0, 0].astype(BF16)

    @pl.when(i < nused_ref[0])
    def _():
        for s in range(rows):
            xb[:, s * LANES:(s + 1) * LANES] = _token_cols(xs_ref, s, tm, rows).astype(BF16)
        h13 = jnp.dot(xb[...], w13b[...], preferred_element_type=F32)
        h1, h3 = h13[:, :de], h13[:, de:]
        a = (h1 * jax.nn.sigmoid(h1) * h3).astype(BF16)
        _to_token_rows(jnp.dot(a, w2b[...], preferred_element_type=F32), ys_ref)

    @pl.when(i >= nused_ref[0])
    def _():
        ys_ref[...] = jnp.zeros_like(ys_ref)


def experts(block_exp, nused, xs, w1, w3, w2, layer):
    d, de = w1.shape[-2:]
    blk_rows = MOE_BLOCK * (d // LANES)
    blk = lambda i, bexp, nu: (jnp.minimum(i, nu[0] - 1), 0)
    return pl.pallas_call(
        _expert_kernel,
        out_shape=jax.ShapeDtypeStruct(xs.shape, F32),
        grid_spec=pltpu.PrefetchScalarGridSpec(
            num_scalar_prefetch=2,
            grid=(xs.shape[0] // blk_rows,),
            in_specs=[pl.BlockSpec((blk_rows, LANES), blk),
                      pl.BlockSpec((1, 1, d, de), lambda i, bexp, nu: (layer, bexp[i], 0, 0)),
                      pl.BlockSpec((1, 1, d, de), lambda i, bexp, nu: (layer, bexp[i], 0, 0)),
                      pl.BlockSpec((1, 1, de, d), lambda i, bexp, nu: (layer, bexp[i], 0, 0))],
            out_specs=pl.BlockSpec((blk_rows, LANES), lambda i, bexp, nu: (i, 0)),
            scratch_shapes=[pltpu.VMEM((MOE_BLOCK, d), BF16), pltpu.VMEM((d, 2 * de), BF16),
                            pltpu.VMEM((de, d), BF16)]),
        compiler_params=_params("arbitrary"),
        name="moe_experts",
    )(block_exp, nused, xs, w1, w3, w2)


def _combine_kernel(dest_ref, x_ref, route_ref, g_ref, gain_ref, ys_ref, o_ref, y0_ref, y1_ref, sem,
                    *, final):
    tm, d = x_ref.shape
    rows = d // LANES
    base = pl.program_id(0) * tm

    def issue(r, carry):
        _token_copy(ys_ref, dest_ref[2 * (base + r)], y0_ref, r, sem, rows).start()
        _token_copy(ys_ref, dest_ref[2 * (base + r) + 1], y1_ref, r, sem, rows).start()
        return carry

    def drain(r, carry):
        _token_copy(ys_ref, dest_ref[2 * (base + r)], y0_ref, r, sem, rows).wait()
        _token_copy(ys_ref, dest_ref[2 * (base + r) + 1], y1_ref, r, sem, rows).wait()
        return carry

    lax.fori_loop(0, tm, issue, 0)
    lax.fori_loop(0, tm, drain, 0)
    route = route_ref[...]
    g0, g1 = route[:, 4:5], route[:, 5:6]
    ssq = jnp.zeros((tm, 1), F32)
    for s in range(rows):
        cols = slice(s * LANES, (s + 1) * LANES)
        y = g0 * _token_cols(y0_ref, s, tm, rows) + g1 * _token_cols(y1_ref, s, tm, rows)
        out = x_ref[:, cols] + g_ref[0][:, cols] * y
        o_ref[:, cols] = out
        ssq = ssq + jnp.sum(out * out, axis=-1, keepdims=True)
    if final:
        o_ref[...] = o_ref[...] * lax.rsqrt(ssq * (1.0 / d) + EPS) * gain_ref[...]


def combine(dest, x, route, gate, gain, ys, rows_per_seg, final):
    t, d = x.shape
    rows = d // LANES
    tm = _tile(rows_per_seg, 256)
    return pl.pallas_call(
        functools.partial(_combine_kernel, final=final),
        out_shape=jax.ShapeDtypeStruct((t, d), F32),
        grid_spec=pltpu.PrefetchScalarGridSpec(
            num_scalar_prefetch=1,
            grid=(t // tm,),
            in_specs=[pl.BlockSpec((tm, d), lambda i, dest_ref: (i, 0)),
                      pl.BlockSpec((tm, ROUTE_W), lambda i, dest_ref: (i, 0)),
                      pl.BlockSpec((1, 1, d), lambda i, dest_ref: ((i * tm) // rows_per_seg, 0, 0)),
                      pl.BlockSpec((1, d), lambda i, dest_ref: (0, 0)),
                      pl.BlockSpec(memory_space=pl.ANY)],
            out_specs=pl.BlockSpec((tm, d), lambda i, dest_ref: (i, 0)),
            scratch_shapes=[pltpu.VMEM((tm * rows, LANES), F32), pltpu.VMEM((tm * rows, LANES), F32),
                            pltpu.SemaphoreType.DMA]),
        compiler_params=_params("arbitrary"),
        name="moe_combine",
    )(dest, x, route, gate, gain, ys)


def _slot_plan(route_list, counts):
    cnt = counts[0, N_GROUPS:N_GROUPS + N_EXPERTS].astype(jnp.int32)
    padded = (cnt + MOE_BLOCK - 1) // MOE_BLOCK * MOE_BLOCK
    pad_end = jnp.cumsum(padded)
    pad_start = pad_end - padded
    n_assign = 2 * sum(r.shape[0] for r in route_list)
    n_blocks = -(-n_assign // MOE_BLOCK) + N_EXPERTS
    block_start = jnp.arange(n_blocks, dtype=jnp.int32) * MOE_BLOCK
    block_exp = jnp.minimum(jnp.sum(block_start[:, None] >= pad_end[None, :], axis=1),
                            N_EXPERTS - 1).astype(jnp.int32)
    nused = (pad_end[-1:] // MOE_BLOCK).astype(jnp.int32)
    dests = []
    for r in route_list:
        e = r[:, 0:2].astype(jnp.int32)
        dests.append((pad_start[e] + r[:, 2:4].astype(jnp.int32)).reshape(-1))
    return dests, pad_end.astype(jnp.int32), block_exp, nused, n_blocks


def hier_moe_layer(streams, layer, wr, br, w1, w3, w2, gain, final):
    counts = jnp.zeros((1, ROUTE_W), F32)
    routes = []
    for x, sh, sc, _, rps in streams:
        route, counts = router(x, sh, sc, wr, br, counts, rps)
        routes.append(route)
    dests, pad_end, block_exp, nused, n_blocks = _slot_plan(routes, counts)
    xs = None
    for (x, sh, sc, _, rps), dest in zip(streams, dests):
        xs = dispatch(dest, pad_end, nused, n_blocks, x, sh, sc, xs, rps)
    ys = experts(block_exp, nused, xs, w1, w3, w2, layer)
    return [combine(dest, x, route, g, gain, ys, rps, final)
            for (x, _, _, g, rps), dest, route in zip(streams, dests, routes)]


def kernel(x, c, ctx, c_ctx, mod_w, mod_b, hy_w_in, hy_conv_w, hy_conv_b, hy_f_w1, hy_f_b1,
           hy_f_freq1, hy_f_w2, hy_f_b2, hy_f_freq2, hy_f_w3, hy_bias, hy_w_out,
           at_w_qkv, at_sink, at_w_o, moe_wg, moe_bg, moe_we, moe_be, moe_w1, moe_w3, moe_w2,
           final_gain):
    batch, seq_len, d = x.shape
    ctx_len = ctx.shape[1]
    depth = mod_w.shape[0]
    d_q = at_w_qkv.shape[2] - 2 * N_KV_HEADS * HEAD_DIM
    d_kv = N_KV_HEADS * HEAD_DIM
    n_ctx = batch * ctx_len

    xl = x.reshape(batch * seq_len, d)
    xc = ctx.reshape(n_ctx, d)

    n_cond = batch + 1
    cond = jnp.concatenate([c, c_ctx[None, :], jnp.zeros((-n_cond % 8, d), F32)], axis=0)
    mods = adaln(cond, mod_w, mod_b)

    def mod_slices(i):
        parts = [mods[i, :, j * d:(j + 1) * d] for j in range(6)]
        lat = [p[:batch].reshape(batch, 1, d) for p in parts]
        cx = [p[batch:batch + 1].reshape(1, 1, d) for p in parts]
        return lat, cx

    def use_fft(n):
        return n >= 512 and (2 * n) % (FFT_N1 * 2 * SUB) == 0

    tables = {n: (fft_tables(n) if use_fft(n) else dft_tables(n)) for n in {seq_len, ctx_len}}
    rtabs = rope_tables(seq_len)
    gain = final_gain.reshape(1, d)

    for i in range(depth):
        last = i == depth - 1
        j = i // 2
        (sh1, sc1, g1, sh2, sc2, g2), (sh1c, sc1c, g1c, sh2c, sc2c, g2c) = mod_slices(i)
        if i % 2 == 0:
            w_in = hy_w_in[j].astype(BF16)
            w_out = hy_w_out[j].astype(BF16)
            filt = (hy_f_w1[j], hy_f_b1[j], hy_f_freq1[j], hy_f_w2[j], hy_f_b2[j], hy_f_freq2[j],
                    hy_f_w3[j])
            bias = hy_bias[j].reshape(2, 1, d)

            def mixer(xs_, sh, sc, g, n_seq, rps):
                u = modmm(xs_, sh, sc, w_in, rps)
                u3 = short_conv(u, hy_conv_w[j], hy_conv_b[j], n_seq)
                if use_fft(n_seq):
                    taps = hyena_filter_taps(n_seq, *filt, d, even_odd=False)
                    spec = fft_filter_spectrum(tables[n_seq], *taps)
                    z = hyena_long_convs_fft(u3, tables[n_seq], spec, bias, n_seq, d)
                else:
                    taps = hyena_filter_taps(n_seq, *filt, d, even_odd=True)
                    spec = filter_spectrum(*tables[n_seq], *taps)
                    z = hyena_long_convs(u3, tables[n_seq], spec, bias, n_seq, d)
                return mm_res(z, w_out, xs_, g, rps)

            xl = mixer(xl, sh1, sc1, g1, seq_len, seq_len)
            if not last:
                xc = mixer(xc, sh1c, sc1c, g1c, ctx_len, n_ctx)
        else:
            w_qkv = at_w_qkv[j].astype(BF16)
            w_o = at_w_o[j].astype(BF16)
            sink = at_sink[j]
            qkv = modmm(xl, sh1, sc1, w_qkv, seq_len, rope=(d_q, d_q + d_kv), rope_tabs=rtabs)
            qkv_c = modmm(xc, sh1c, sc1c, w_qkv, n_ctx)
            o = window_attention(qkv, qkv_c, sink, batch, seq_len, ctx_len, d_q, d_kv)
            xl = mm_res(o, w_o, xl, g1, seq_len)
            if not last:
                o_c = ctx_attention(qkv_c, sink, batch, ctx_len, d_q, d_kv)
                xc = mm_res(o_c, w_o, xc, g1c, n_ctx)

        wr = jnp.concatenate([moe_wg[i], moe_we[i],
                              jnp.zeros((d, ROUTE_W - N_GROUPS - N_EXPERTS), F32)], axis=1)
        br = jnp.concatenate([moe_bg[i], moe_be[i],
                              jnp.zeros((ROUTE_W - N_GROUPS - N_EXPERTS,), F32)]).reshape(1, ROUTE_W)
        streams = [(xl, sh2, sc2, g2, seq_len)]
        if not last:
            streams.append((xc, sh2c, sc2c, g2c, n_ctx))
        outs = hier_moe_layer(streams, i, wr, br, moe_w1, moe_w3, moe_w2, gain, last)
        xl = outs[0]
        if not last:
            xc = outs[1]

    return xl.reshape(batch, seq_len, d)
```

```python
import functools
import math

import jax
import jax.numpy as jnp
from jax import lax
from jax.experimental import pallas as pl
from jax.experimental.pallas import tpu as pltpu

F32 = jnp.float32
BF16 = jnp.bfloat16
HIGHEST = lax.Precision.HIGHEST

EPS = 1e-6
GRID_W = 64
HEAD_DIM = 64
N_KV_HEADS = 4
WINDOW = 128
Q_BLOCK = 128
ROPE_PAIRS = HEAD_DIM // 4
ROPE_BASE = 10000.0
N_GROUPS = 8
EXPERTS_PER_GROUP = 8
N_EXPERTS = N_GROUPS * EXPERTS_PER_GROUP
MOE_BLOCK = 256
HY_EMB = 33
HY_BANDS = (HY_EMB - 1) // 2
HY_FAST_DECAY_PCT = 0.3
HY_SLOW_DECAY_PCT = 1.5
HY_DECAY_TARGET = 1e-2

LANES = 128
ROUTE_W = LANES
VMEM_LIMIT = 56 * 1024 * 1024


def _tile(n, pref):
    if n <= pref:
        return n
    t = pref
    while n % t:
        t -= 1
    return t


def _params(*sem):
    return pltpu.CompilerParams(dimension_semantics=sem, vmem_limit_bytes=VMEM_LIMIT)


def _modulated_norm(x, sh, sc):
    y = x * lax.rsqrt(jnp.mean(x * x, axis=-1, keepdims=True) + EPS)
    return y * (1.0 + sc) + sh


def _adaln_kernel(c_ref, w_ref, b_ref, o_ref):
    c = c_ref[...]
    a = c * jax.nn.sigmoid(c)
    o_ref[0] = jnp.dot(a, w_ref[0], precision=HIGHEST, preferred_element_type=F32) + b_ref[0]


def adaln(cond, mod_w, mod_b):
    depth, d, n = mod_w.shape
    r = cond.shape[0]
    tn = _tile(n, 1024)
    return pl.pallas_call(
        _adaln_kernel,
        out_shape=jax.ShapeDtypeStruct((depth, r, n), F32),
        grid=(depth, n // tn),
        in_specs=[pl.BlockSpec((r, d), lambda l, j: (0, 0)),
                  pl.BlockSpec((1, d, tn), lambda l, j: (l, 0, j)),
                  pl.BlockSpec((1, 1, tn), lambda l, j: (l, 0, j))],
        out_specs=pl.BlockSpec((1, r, tn), lambda l, j: (l, 0, j)),
        compiler_params=_params("arbitrary", "arbitrary"),
        name="adaln",
    )(cond, mod_w, mod_b.reshape(depth, 1, n))


def _modmm_kernel(x_ref, sh_ref, sc_ref, w_ref, *rest, rope):
    if rope is None:
        o_ref, h_ref = rest
    else:
        cos_ref, sa_ref, sb_ref, o_ref, h_ref = rest

    @pl.when(pl.program_id(1) == 0)
    def _():
        h_ref[...] = _modulated_norm(x_ref[...], sh_ref[0], sc_ref[0]).astype(BF16)

    acc = jnp.dot(h_ref[...], w_ref[...], preferred_element_type=F32)
    if rope is None:
        o_ref[...] = acc.astype(o_ref.dtype)
        return
    d_q, d_qk = rope
    tn = acc.shape[1]
    for c in range(tn // LANES):
        col0 = pl.program_id(1) * tn + c * LANES
        x = acc[:, c * LANES:(c + 1) * LANES]

        @pl.when(col0 < d_qk)
        def _(x=x, c=c, col0=col0):
            y = (x * cos_ref[...] + pltpu.roll(x, LANES - ROPE_PAIRS, 1) * sa_ref[...]
                 + pltpu.roll(x, ROPE_PAIRS, 1) * sb_ref[...])
            y = y * jnp.where(col0 < d_q, Q_SCALE, 1.0).astype(F32)
            o_ref[:, c * LANES:(c + 1) * LANES] = y.astype(o_ref.dtype)

        @pl.when(col0 >= d_qk)
        def _(x=x, c=c):
            o_ref[:, c * LANES:(c + 1) * LANES] = x.astype(o_ref.dtype)


def modmm(x, sh, sc, w, rows_per_seg, rope=None, rope_tabs=None):
    t, d = x.shape
    n = w.shape[1]
    tm = _tile(rows_per_seg, 1024)
    tn = _tile(n, 1024)
    seg = lambda i, j: ((i * tm) // rows_per_seg, 0, 0)
    in_specs = [pl.BlockSpec((tm, d), lambda i, j: (i, 0)),
                pl.BlockSpec((1, 1, d), seg),
                pl.BlockSpec((1, 1, d), seg),
                pl.BlockSpec((d, tn), lambda i, j: (0, j))]
    args = [x, sh, sc, w]
    if rope is not None:
        nb = rows_per_seg // tm
        in_specs += [pl.BlockSpec((tm, LANES), lambda i, j: (i % nb, 0))] * 3
        args += list(rope_tabs)
    return pl.pallas_call(
        functools.partial(_modmm_kernel, rope=rope),
        out_shape=jax.ShapeDtypeStruct((t, n), BF16),
        grid=(t // tm, n // tn),
        in_specs=in_specs,
        out_specs=pl.BlockSpec((tm, tn), lambda i, j: (i, j)),
        scratch_shapes=[pltpu.VMEM((tm, d), BF16)],
        compiler_params=_params("arbitrary", "arbitrary"),
        name="modmm",
    )(*args)


def _mmres_kernel(a_ref, w_ref, r_ref, g_ref, o_ref):
    acc = jnp.dot(a_ref[...], w_ref[...], preferred_element_type=F32)
    o_ref[...] = r_ref[...] + g_ref[0] * acc


def mm_res(a, w, res, gate, rows_per_seg):
    t, k = a.shape
    n = w.shape[1]
    tm = _tile(rows_per_seg, 1024)
    tn = _tile(n, 1024)
    return pl.pallas_call(
        _mmres_kernel,
        out_shape=jax.ShapeDtypeStruct((t, n), F32),
        grid=(t // tm, n // tn),
        in_specs=[pl.BlockSpec((tm, k), lambda i, j: (i, 0)),
                  pl.BlockSpec((k, tn), lambda i, j: (0, j)),
                  pl.BlockSpec((tm, tn), lambda i, j: (i, j)),
                  pl.BlockSpec((1, 1, tn), lambda i, j: ((i * tm) // rows_per_seg, 0, j))],
        out_specs=pl.BlockSpec((tm, tn), lambda i, j: (i, j)),
        compiler_params=_params("arbitrary", "arbitrary"),
        name="mm_res",
    )(a, w, res, gate)


def _sconv_kernel(u_ref, w_ref, b_ref, o_ref):
    x = u_ref[...].astype(F32)
    n = x.shape[0]
    row = lax.broadcasted_iota(jnp.int32, x.shape, 0)
    prev = jnp.where(row == 0, 0.0, pltpu.roll(x, 1, 0))
    nxt = jnp.where(row == n - 1, 0.0, pltpu.roll(x, n - 1, 0))
    y = prev * w_ref[0:1, :] + x * w_ref[1:2, :] + nxt * w_ref[2:3, :] + b_ref[...]
    o_ref[...] = y.astype(o_ref.dtype)


def short_conv(u, w, b, seq_len):
    t, c = u.shape
    tc = _tile(c, 256)
    return pl.pallas_call(
        _sconv_kernel,
        out_shape=jax.ShapeDtypeStruct((t, c), BF16),
        grid=(t // seq_len, c // tc),
        in_specs=[pl.BlockSpec((seq_len, tc), lambda s, j: (s, j)),
                  pl.BlockSpec((3, tc), lambda s, j: (0, j)),
                  pl.BlockSpec((1, tc), lambda s, j: (0, j))],
        out_specs=pl.BlockSpec((seq_len, tc), lambda s, j: (s, j)),
        compiler_params=_params("arbitrary", "arbitrary"),
        name="short_conv",
    )(u, w, b.reshape(1, c))


def _filter_mlp_kernel(feat_ref, w1_ref, b1_ref, f1_ref, w2_ref, b2_ref, f2_ref, h_ref):
    h = jnp.sin(f1_ref[...] * (jnp.dot(feat_ref[...], w1_ref[...], precision=HIGHEST,
                                       preferred_element_type=F32) + b1_ref[...]))
    h_ref[...] = jnp.sin(f2_ref[...] * (jnp.dot(h, w2_ref[...], precision=HIGHEST,
                                                preferred_element_type=F32) + b2_ref[...]))


def _filter_kernel(hf_ref, hb_ref, tf_ref, tb_ref, w3f_ref, w3b_ref, dl_ref, p_ref, q_ref, *,
                   even_odd):
    dl = jnp.abs(dl_ref[...])
    hf = jnp.dot(hf_ref[...], w3f_ref[...], precision=HIGHEST,
                 preferred_element_type=F32) * jnp.exp(-tf_ref[...] * dl)
    hb = jnp.dot(hb_ref[...], w3b_ref[...], precision=HIGHEST,
                 preferred_element_type=F32) * jnp.exp(-tb_ref[...] * dl)
    row = lax.broadcasted_iota(jnp.int32, hf.shape, 0)
    hb = jnp.where(row == 0, 0.0, hb)
    s = lax.rsqrt(jnp.sum(hf * hf + hb * hb, axis=0, keepdims=True) + EPS)
    p, q = ((hf + hb) * s, (hf - hb) * s) if even_odd else (hf * s, hb * s)
    p_ref[0] = p.astype(p_ref.dtype).reshape(p_ref.shape[1:])
    q_ref[0] = q.astype(q_ref.dtype).reshape(q_ref.shape[1:])


def hyena_filter_taps(seq_len, w1, b1, f1, w2, b2, f2, w3, d, even_odd):
    t = jnp.linspace(0.0, 1.0, seq_len, dtype=F32)[:, None]
    omega = (2.0 * math.pi / seq_len) * jnp.arange(seq_len, dtype=F32)[:, None]
    bands = jnp.linspace(1e-4, HY_BANDS - 1, HY_BANDS, dtype=F32)[None, :]
    feats = jnp.concatenate([t, jnp.cos(bands * omega), -jnp.sin(bands * omega),
                             jnp.zeros((seq_len, LANES - HY_EMB), F32)], axis=-1)
    if not even_odd:
        feats = jnp.concatenate([feats, feats[:1], feats[:0:-1]], axis=0)
    w1p = jnp.concatenate([w1, jnp.zeros((LANES - HY_EMB, w1.shape[1]), F32)], axis=0)
    fw = w1.shape[1]
    rows = feats.shape[0]
    vec = lambda a: a.reshape(1, fw)
    h = pl.pallas_call(
        _filter_mlp_kernel,
        out_shape=jax.ShapeDtypeStruct((rows, fw), F32),
        compiler_params=_params(),
        name="hyena_filter_mlp",
    )(feats, w1p, vec(b1), vec(f1), w2, vec(b2), vec(f2))
    tcol = feats[:, 0:1]
    max_decay = math.log(HY_DECAY_TARGET) / HY_FAST_DECAY_PCT
    min_decay = math.log(HY_DECAY_TARGET) / HY_SLOW_DECAY_PCT
    deltas = jnp.linspace(min_decay, max_decay, d, dtype=F32).reshape(1, d)
    tn = _tile(d, 256)
    nb = d // tn
    bsel = 0 if even_odd else 1
    if even_odd:
        oshape, oblock = (2, seq_len, d), (1, seq_len, tn)
        omap = lambda o, j: (o, 0, j)
    else:
        n2 = 2 * seq_len // FFT_N1
        oshape, oblock = (2, seq_len // n2, n2, d), (1, seq_len // n2, n2, tn)
        omap = lambda o, j: (o, 0, 0, j)
    out = jax.ShapeDtypeStruct(oshape, BF16)
    return pl.pallas_call(
        functools.partial(_filter_kernel, even_odd=even_odd),
        out_shape=(out, out),
        grid=(2, nb),
        in_specs=[pl.BlockSpec((seq_len, fw), lambda o, j: (0, 0)),
                  pl.BlockSpec((seq_len, fw), lambda o, j: (bsel, 0)),
                  pl.BlockSpec((seq_len, 1), lambda o, j: (0, 0)),
                  pl.BlockSpec((seq_len, 1), lambda o, j: (bsel, 0)),
                  pl.BlockSpec((fw, tn), lambda o, j: (0, (2 * o) * nb + j)),
                  pl.BlockSpec((fw, tn), lambda o, j: (0, (2 * o + 1) * nb + j)),
                  pl.BlockSpec((1, tn), lambda o, j: (0, j))],
        out_specs=(pl.BlockSpec(oblock, omap), pl.BlockSpec(oblock, omap)),
        compiler_params=_params("arbitrary", "arbitrary"),
        name="hyena_filter",
    )(h, h, tcol, tcol, w3, w3, deltas)


def _alt_sum(x):
    row = lax.broadcasted_iota(jnp.int32, x.shape, 0)
    return jnp.sum(jnp.where((row & 1) == 1, -x, x), axis=0, keepdims=True)


def _spectrum_kernel(c_ref, s_ref, e_ref, o_ref, kr_ref, km_ref):
    kr_ref[0] = jnp.dot(c_ref[...], e_ref[0], preferred_element_type=F32)
    km = jnp.dot(s_ref[...], o_ref[0], preferred_element_type=F32)

    @pl.when(pl.program_id(2) != 0)
    def _():
        km_ref[0] = km

    @pl.when(pl.program_id(2) == 0)
    def _():
        row = lax.broadcasted_iota(jnp.int32, km.shape, 0)
        km_ref[0] = jnp.where(row == 0, _alt_sum(e_ref[0].astype(F32)), km)


def filter_spectrum(cmat, smat, e_taps, o_taps):
    _, n, d = e_taps.shape
    tk = _tile(n, 512)
    tn = _tile(d, 512)
    out = jax.ShapeDtypeStruct((2, n, d), F32)
    return pl.pallas_call(
        _spectrum_kernel,
        out_shape=(out, out),
        grid=(2, d // tn, n // tk),
        in_specs=[pl.BlockSpec((tk, n), lambda o, j, k: (k, 0)),
                  pl.BlockSpec((tk, n), lambda o, j, k: (k, 0)),
                  pl.BlockSpec((1, n, tn), lambda o, j, k: (o, 0, j)),
                  pl.BlockSpec((1, n, tn), lambda o, j, k: (o, 0, j))],
        out_specs=(pl.BlockSpec((1, tk, tn), lambda o, j, k: (o, k, j)),
                   pl.BlockSpec((1, tk, tn), lambda o, j, k: (o, k, j))),
        compiler_params=_params("arbitrary", "arbitrary", "arbitrary"),
        name="filter_spectrum",
    )(cmat, smat, e_taps, o_taps)


def _dft_fwd_kernel(c_ref, s_ref, u_ref, kr_ref, km_ref, yr_ref, q_ref):
    u = u_ref[...]
    a = jnp.dot(c_ref[...], u, preferred_element_type=F32)
    b = jnp.dot(s_ref[...], u, preferred_element_type=F32)
    kr = kr_ref[0]
    km = km_ref[0]
    yr = a * kr - b * km
    q = b * kr + a * km

    @pl.when(pl.program_id(2) != 0)
    def _():
        yr_ref[...] = yr.astype(yr_ref.dtype)
        q_ref[...] = q.astype(q_ref.dtype)

    @pl.when(pl.program_id(2) == 0)
    def _():
        row = lax.broadcasted_iota(jnp.int32, yr.shape, 0)
        ynyq = _alt_sum(u.astype(F32)) * km[0:1, :]
        yr_ref[...] = jnp.where(row == 0, 0.5 * yr, yr).astype(yr_ref.dtype)
        q_ref[...] = jnp.where(row == 0, 0.5 * ynyq, q).astype(q_ref.dtype)


def dft_fwd(cmat, smat, u, col0, kr, km, order, seq_len, d):
    t = u.shape[0]
    tk = _tile(seq_len, 512)
    tn = _tile(d, 512)
    cb = col0 // tn
    out = jax.ShapeDtypeStruct((t, d), BF16)
    nk = seq_len // tk
    return pl.pallas_call(
        _dft_fwd_kernel,
        out_shape=(out, out),
        grid=(t // seq_len, d // tn, nk),
        in_specs=[pl.BlockSpec((tk, seq_len), lambda b, j, k: (k, 0)),
                  pl.BlockSpec((tk, seq_len), lambda b, j, k: (k, 0)),
                  pl.BlockSpec((seq_len, tn), lambda b, j, k: (b, cb + j)),
                  pl.BlockSpec((1, tk, tn), lambda b, j, k: (order, k, j)),
                  pl.BlockSpec((1, tk, tn), lambda b, j, k: (order, k, j))],
        out_specs=(pl.BlockSpec((tk, tn), lambda b, j, k: (b * nk + k, j)),
                   pl.BlockSpec((tk, tn), lambda b, j, k: (b * nk + k, j))),
        compiler_params=_params("arbitrary", "arbitrary", "arbitrary"),
        name="dft_fwd",
    )(cmat, smat, u, kr, km)


def _dft_inv_kernel(c_ref, s_ref, yr_ref, q_ref, u_ref, g_ref, bias_ref, o_ref, *, seq_len):
    acc = jnp.dot(c_ref[...], yr_ref[...], preferred_element_type=F32)
    acc += jnp.dot(s_ref[...], q_ref[...], preferred_element_type=F32)
    tt = acc.shape[0]
    t_idx = pl.program_id(2) * tt + lax.broadcasted_iota(jnp.int32, acc.shape, 0)
    half_nyq = q_ref[0:1, :].astype(F32)
    acc += jnp.where((t_idx & 1) == 1, -half_nyq, half_nyq)
    u = u_ref[...].astype(F32)
    y = acc * (1.0 / seq_len) + u * bias_ref[0]
    o_ref[...] = (g_ref[...].astype(F32) * y).astype(o_ref.dtype)


def dft_inv(cmat, smat, yr, q, u, ucol0, gsrc, gcol0, bias, order, seq_len, d):
    t = yr.shape[0]
    tt = _tile(seq_len, 512)
    tn = _tile(d, 512)
    ub, gb = ucol0 // tn, gcol0 // tn
    nt = seq_len // tt
    return pl.pallas_call(
        functools.partial(_dft_inv_kernel, seq_len=seq_len),
        out_shape=jax.ShapeDtypeStruct((t, d), BF16),
        grid=(t // seq_len, d // tn, nt),
        in_specs=[pl.BlockSpec((tt, seq_len), lambda b, j, k: (k, 0)),
                  pl.BlockSpec((tt, seq_len), lambda b, j, k: (k, 0)),
                  pl.BlockSpec((seq_len, tn), lambda b, j, k: (b, j)),
                  pl.BlockSpec((seq_len, tn), lambda b, j, k: (b, j)),
                  pl.BlockSpec((tt, tn), lambda b, j, k: (b * nt + k, ub + j)),
                  pl.BlockSpec((tt, tn), lambda b, j, k: (b * nt + k, gb + j)),
                  pl.BlockSpec((1, 1, tn), lambda b, j, k: (order, 0, j))],
        out_specs=pl.BlockSpec((tt, tn), lambda b, j, k: (b * nt + k, j)),
        compiler_params=_params("arbitrary", "arbitrary", "arbitrary"),
        name="dft_inv",
    )(cmat, smat, yr, q, u, gsrc, bias)


def dft_tables(seq_len):
    n = jnp.arange(seq_len, dtype=jnp.int32)
    ang = ((n[:, None] * n[None, :]) % (2 * seq_len)).astype(F32) * (math.pi / seq_len)
    return jnp.cos(ang).astype(BF16), jnp.sin(ang).astype(BF16)


def hyena_long_convs(u3, tables, spectrum, bias, seq_len, d):
    cmat, smat = tables
    kr, km = spectrum
    yr, q = dft_fwd(cmat, smat, u3, 0, kr, km, 0, seq_len, d)
    z = dft_inv(cmat, smat, yr, q, u3, 0, u3, d, bias, 0, seq_len, d)
    yr, q = dft_fwd(cmat, smat, z, 0, kr, km, 1, seq_len, d)
    return dft_inv(cmat, smat, yr, q, z, 0, u3, 2 * d, bias, 1, seq_len, d)


FFT_N1 = 64
SUB = 8


def fft_tables(seq_len):
    n = 2 * seq_len
    n1, n2 = FFT_N1, n // FFT_N1
    two_pi = 2.0 * math.pi
    eye = jnp.eye(SUB, dtype=F32)
    k1 = jnp.arange(n1, dtype=jnp.int32)
    h1 = jnp.arange(n1 // 2, dtype=jnp.int32)
    ang = ((k1[:, None] * h1[None, :]) % n1).astype(F32) * (two_pi / n1)
    ca = jnp.stack([jnp.cos(ang), -jnp.sin(ang)], axis=1)
    ka = jnp.einsum('krn,jm->krjnm', ca, eye).reshape(n1 * 2 * SUB, (n1 // 2) * SUB)
    c2 = jnp.stack([jnp.cos(ang.T), -jnp.sin(ang.T)], axis=2)
    ka2 = jnp.einsum('tkr,jm->tjkrm', c2, eye).reshape((n1 // 2) * SUB, n1 * 2 * SUB)
    k2 = jnp.arange(n2 // 2, dtype=jnp.int32)
    m2 = jnp.arange(n2, dtype=jnp.int32)
    k = k1[:, None, None] + n1 * k2[None, :, None]
    th = ((k * m2[None, None, :]) % n).astype(F32) * (two_pi / n)
    c, s = jnp.cos(th), jnp.sin(th)
    fb = jnp.concatenate([jnp.concatenate([c, s], axis=2),
                          jnp.concatenate([-s, c], axis=2)], axis=1)
    ct, st = jnp.swapaxes(c, 1, 2), jnp.swapaxes(s, 1, 2)
    gb = jnp.concatenate([jnp.concatenate([ct, -st], axis=2),
                          jnp.concatenate([st, ct], axis=2)], axis=1)
    return tuple(a.astype(BF16) for a in (ka, fb, gb, ka2))


def _fft_stage_a(src_ref, ka_ref, a_ref):
    n1h, n2, tn = src_ref.shape
    for g in range(n2 // SUB):
        rhs = src_ref[:, g * SUB:(g + 1) * SUB, :].reshape(n1h * SUB, tn).astype(BF16)
        out = jnp.dot(ka_ref[...], rhs, preferred_element_type=F32)
        a_ref[:, :, g * SUB:(g + 1) * SUB, :] = out.reshape(FFT_N1, 2, SUB, tn)


def _fft_stage_b(fb_ref, a_ref, k1):
    _, _, n2, tn = a_ref.shape
    slab = a_ref[k1].reshape(2 * n2, tn).astype(BF16)
    return jnp.dot(fb_ref[k1], slab, preferred_element_type=F32)


def _block_alt_sum(src_ref):
    return _alt_sum(jnp.sum(src_ref[...], axis=0))


def _fftspec_kernel(tf_ref, tb_ref, ka_ref, fb_ref, kf_ref, knyq_ref, src_ref, a_ref):
    h = a_ref.shape[2] // 2
    tn = a_ref.shape[3]
    nyq = None
    for which, t_ref in enumerate((tf_ref, tb_ref)):
        src_ref[...] = t_ref[0].astype(F32)
        part = _block_alt_sum(src_ref)
        nyq = part if nyq is None else nyq + part
        _fft_stage_a(src_ref, ka_ref, a_ref)

        def body(k1, carry, which=which):
            x = _fft_stage_b(fb_ref, a_ref, k1).reshape(2, h, tn)
            if which == 0:
                kf_ref[0, k1] = x
            else:
                sign = 1.0 - 2.0 * jnp.asarray(k1 & 1, F32)
                kf_ref[0, k1] = kf_ref[0, k1] + sign * x
            return carry

        lax.fori_loop(0, FFT_N1, body, 0, unroll=4)
    knyq_ref[0] = nyq
    dc = kf_ref[0, 0]
    row = lax.broadcasted_iota(jnp.int32, dc.shape, 1)
    kf_ref[0, 0] = jnp.where(row == 0, 0.5 * dc, dc)


def _const_spec(shape):
    return pl.BlockSpec(shape, lambda *_: (0,) * len(shape), pipeline_mode=pl.Buffered(1))


def fft_filter_spectrum(tabs, taps_f, taps_b):
    ka, fb, _, _ = tabs
    _, n1h, n2, d = taps_f.shape
    tn = _tile(d, 256)
    tap = pl.BlockSpec((1, n1h, n2, tn), lambda o, j: (o, 0, 0, j))
    return pl.pallas_call(
        _fftspec_kernel,
        out_shape=(jax.ShapeDtypeStruct((2, FFT_N1, 2, n2 // 2, d), F32),
                   jax.ShapeDtypeStruct((2, 1, d), F32)),
        grid=(2, d // tn),
        in_specs=[tap, tap, _const_spec(ka.shape), _const_spec(fb.shape)],
        out_specs=(pl.BlockSpec((1, FFT_N1, 2, n2 // 2, tn), lambda o, j: (o, 0, 0, 0, j)),
                   pl.BlockSpec((1, 1, tn), lambda o, j: (o, 0, j))),
        scratch_shapes=[pltpu.VMEM((n1h, n2, tn), F32), pltpu.VMEM((FFT_N1, 2, n2, tn), F32)],
        compiler_params=_params("arbitrary", "arbitrary"),
        name="fft_filter_spectrum",
    )(taps_f, taps_b, ka, fb)


def _fftconv_kernel(u_ref, g_ref, bias_ref, kf_ref, knyq_ref, ka_ref, fb_ref, gb_ref, ka2_ref,
                    o_ref, src_ref, a_ref):
    n1h, n2, tn = src_ref.shape
    h = n2 // 2
    seq_len = n1h * n2
    src_ref[...] = u_ref[...].astype(F32)
    ynyq = _block_alt_sum(src_ref) * knyq_ref[0] * (0.5 / seq_len)
    _fft_stage_a(src_ref, ka_ref, a_ref)

    def body(k1, carry):
        x = _fft_stage_b(fb_ref, a_ref, k1)
        xr, xi = x[:h], x[h:]
        kr, ki = kf_ref[0, k1, 0], kf_ref[0, k1, 1]
        y = jnp.concatenate([xr * kr - xi * ki, xr * ki + xi * kr], axis=0).astype(BF16)
        z = jnp.dot(gb_ref[k1], y, preferred_element_type=F32)
        a_ref[k1] = z.reshape(2, n2, tn)
        return carry

    lax.fori_loop(0, FFT_N1, body, 0, unroll=16)
    bias = bias_ref[0]
    pair = 2 * SUB
    row = lax.broadcasted_iota(jnp.int32, (n1h, pair, tn), 1)
    nyq = jnp.where((row & 1) == 1, -ynyq, ynyq)
    for gg in range(n2 // pair):
        parts = []
        for g in (2 * gg, 2 * gg + 1):
            zg = a_ref[:, :, g * SUB:(g + 1) * SUB, :].reshape(FFT_N1 * 2 * SUB, tn).astype(BF16)
            acc = jnp.dot(ka2_ref[...], zg, preferred_element_type=F32)
            parts.append(acc.reshape(n1h, SUB, tn))
        acc = jnp.concatenate(parts, axis=1)
        rows = slice(gg * pair, (gg + 1) * pair)
        y = acc * (1.0 / seq_len) + nyq + src_ref[:, rows, :] * bias
        o_ref[:, rows, :] = (g_ref[:, rows, :].astype(F32) * y).astype(o_ref.dtype)


def fft_long_conv(tabs, spec, u, ucol0, gsrc, gcol0, bias, order, seq_len, d):
    ka, fb, gb, ka2 = tabs
    kf, knyq = spec
    nb, n2, _ = u.shape
    n1h = seq_len // n2
    tn = _tile(d, 256)
    ub, gb_ = ucol0 // tn, gcol0 // tn
    return pl.pallas_call(
        _fftconv_kernel,
        out_shape=jax.ShapeDtypeStruct((nb, n2, d), BF16),
        grid=(d // tn, nb // n1h),
        in_specs=[pl.BlockSpec((n1h, n2, tn), lambda j, b: (b, 0, ub + j)),
                  pl.BlockSpec((n1h, n2, tn), lambda j, b: (b, 0, gb_ + j)),
                  pl.BlockSpec((1, 1, tn), lambda j, b: (order, 0, j)),
                  pl.BlockSpec((1, FFT_N1, 2, n2 // 2, tn), lambda j, b: (order, 0, 0, 0, j),
                               pipeline_mode=pl.Buffered(1)),
                  pl.BlockSpec((1, 1, tn), lambda j, b: (order, 0, j)),
                  _const_spec(ka.shape), _const_spec(fb.shape), _const_spec(gb.shape),
                  _const_spec(ka2.shape)],
        out_specs=pl.BlockSpec((n1h, n2, tn), lambda j, b: (b, 0, j)),
        scratch_shapes=[pltpu.VMEM((n1h, n2, tn), F32), pltpu.VMEM((FFT_N1, 2, n2, tn), F32)],
        compiler_params=_params("arbitrary", "arbitrary"),
        name="fft_long_conv",
    )(u, gsrc, bias, kf, knyq, ka, fb, gb, ka2)


def hyena_long_convs_fft(u3, tabs, spec, bias, seq_len, d):
    t = u3.shape[0]
    n2 = 2 * seq_len // FFT_N1
    u3b = u3.reshape(t // n2, n2, 3 * d)
    z = fft_long_conv(tabs, spec, u3b, 0, u3b, d, bias, 0, seq_len, d)
    z = fft_long_conv(tabs, spec, z, 0, u3b, 2 * d, bias, 1, seq_len, d)
    return z.reshape(t, d)


def rope_tables(seq_len):
    pos = jnp.arange(seq_len, dtype=jnp.int32)
    r = (pos // GRID_W).astype(F32)[:, None]
    col = (pos % GRID_W).astype(F32)[:, None]
    inv = ROPE_BASE ** (-(2.0 * jnp.arange(ROPE_PAIRS, dtype=F32)) / (2 * ROPE_PAIRS))
    ar, ac = r * inv, col * inv
    zeros = jnp.zeros_like(ar)
    cos = jnp.concatenate([jnp.cos(ar), jnp.cos(ar), jnp.cos(ac), jnp.cos(ac)], axis=1)
    sa = jnp.concatenate([-jnp.sin(ar), zeros, -jnp.sin(ac), zeros], axis=1)
    sb = jnp.concatenate([zeros, jnp.sin(ar), zeros, jnp.sin(ac)], axis=1)
    rep = LANES // HEAD_DIM
    return tuple(jnp.tile(a, (1, rep)) for a in (cos, sa, sb))


LOG2E = 1.4426950408889634
Q_SCALE = HEAD_DIM ** -0.5 * LOG2E


def _attend(q_ref, sink_ref, keys, vals, bias, o_ref, n_heads, q_per_kv, q_scale=None):
    nq = q_ref.shape[0]
    s_len = keys.shape[0]
    lane = lax.broadcasted_iota(jnp.int32, (s_len, HEAD_DIM), 1)
    ones_col = jnp.where(lane == 0, 1.0, 0.0).astype(BF16)
    hidx = lax.broadcasted_iota(jnp.int32, (q_per_kv, 1, 1), 0)
    pending = []
    for g in range(n_heads // q_per_kv):
        qs = []
        sink = jnp.zeros((q_per_kv, 1, 1), F32)
        for h in range(q_per_kv):
            col = (g * q_per_kv + h) * HEAD_DIM
            q = q_ref[:, col:col + HEAD_DIM]
            if q_scale is not None:
                q = (q.astype(F32) * q_scale).astype(BF16)
            qs.append(q)
            sink = jnp.where(hidx == h, sink_ref[g * q_per_kv + h] * LOG2E, sink)
        qg = jnp.concatenate(qs, axis=0)
        kg = keys[:, g * HEAD_DIM:(g + 1) * HEAD_DIM]
        vg = jnp.concatenate([vals[:, g * HEAD_DIM:(g + 1) * HEAD_DIM], ones_col], axis=1)
        s = lax.dot_general(qg, kg, (((1,), (1,)), ((), ())), preferred_element_type=F32)
        s = s.reshape(q_per_kv, nq, s_len)
        if bias is not None:
            s = s + bias[None]
        m = jnp.maximum(jnp.max(s, axis=-1, keepdims=True), sink)
        p = jnp.exp2(s - m).astype(BF16).reshape(q_per_kv * nq, s_len)
        acc = jnp.dot(p, vg, preferred_element_type=F32).reshape(q_per_kv, nq, 2 * HEAD_DIM)
        denom = acc[:, :, HEAD_DIM:HEAD_DIM + 1] + jnp.exp2(sink - m)
        out = acc[:, :, :HEAD_DIM] / denom
        for h in range(q_per_kv):
            pending.append(out[h])
            if len(pending) == LANES // HEAD_DIM:
                lo = (g * q_per_kv + h + 1) * HEAD_DIM - LANES
                o_ref[:, lo:lo + LANES] = jnp.concatenate(pending, axis=1).astype(o_ref.dtype)
                pending = []


def _attn_kernel(sink_ref, q_ref, k_ref, v_ref, kc_ref, vc_ref, o_ref, *, seq_len, n_heads, q_per_kv):
    span = Q_BLOCK + 2 * WINDOW
    ctx_len = kc_ref.shape[0]
    start = pl.program_id(1) * Q_BLOCK
    kstart = pl.multiple_of(jnp.clip(start - WINDOW, 0, seq_len - span), Q_BLOCK)
    qpos = start + lax.broadcasted_iota(jnp.int32, (Q_BLOCK, span + ctx_len), 0)
    col = lax.broadcasted_iota(jnp.int32, (Q_BLOCK, span + ctx_len), 1)
    visible = (jnp.abs(qpos - (kstart + col)) <= WINDOW) | (col >= span)
    bias = jnp.where(visible, 0.0, -1e30).astype(F32)
    keys = jnp.concatenate([k_ref[pl.ds(kstart, span), :], kc_ref[...]], axis=0)
    vals = jnp.concatenate([v_ref[pl.ds(kstart, span), :], vc_ref[...]], axis=0)
    _attend(q_ref, sink_ref, keys, vals, bias, o_ref, n_heads, q_per_kv)


def window_attention(qkv, qkv_ctx, sink, batch, seq_len, ctx_len, d_q, d_kv):
    n_heads = d_q // HEAD_DIM
    nqb = seq_len // Q_BLOCK
    kcol, vcol = d_q // d_kv, d_q // d_kv + 1
    return pl.pallas_call(
        functools.partial(_attn_kernel, seq_len=seq_len, n_heads=n_heads,
                          q_per_kv=n_heads // N_KV_HEADS),
        out_shape=jax.ShapeDtypeStruct((batch * seq_len, d_q), BF16),
        grid=(batch, nqb),
        in_specs=[pl.BlockSpec(memory_space=pltpu.SMEM),
                  pl.BlockSpec((Q_BLOCK, d_q), lambda b, i: (b * nqb + i, 0)),
                  pl.BlockSpec((seq_len, d_kv), lambda b, i: (b, kcol)),
                  pl.BlockSpec((seq_len, d_kv), lambda b, i: (b, vcol)),
                  pl.BlockSpec((ctx_len, d_kv), lambda b, i: (b, kcol)),
                  pl.BlockSpec((ctx_len, d_kv), lambda b, i: (b, vcol))],
        out_specs=pl.BlockSpec((Q_BLOCK, d_q), lambda b, i: (b * nqb + i, 0)),
        compiler_params=_params("arbitrary", "arbitrary"),
        name="window_attention",
    )(sink, qkv, qkv, qkv, qkv_ctx, qkv_ctx)


def _ctx_attn_kernel(sink_ref, q_ref, kc_ref, vc_ref, o_ref, *, n_heads, q_per_kv, scale):
    _attend(q_ref, sink_ref, kc_ref[...], vc_ref[...], None, o_ref, n_heads, q_per_kv,
            q_scale=scale)


def ctx_attention(qkv_ctx, sink, batch, ctx_len, d_q, d_kv):
    n_heads = d_q // HEAD_DIM
    kcol, vcol = d_q // d_kv, d_q // d_kv + 1
    return pl.pallas_call(
        functools.partial(_ctx_attn_kernel, n_heads=n_heads, q_per_kv=n_heads // N_KV_HEADS,
                          scale=Q_SCALE),
        out_shape=jax.ShapeDtypeStruct((batch * ctx_len, d_q), BF16),
        grid=(batch,),
        in_specs=[pl.BlockSpec(memory_space=pltpu.SMEM),
                  pl.BlockSpec((ctx_len, d_q), lambda b: (b, 0)),
                  pl.BlockSpec((ctx_len, d_kv), lambda b: (b, kcol)),
                  pl.BlockSpec((ctx_len, d_kv), lambda b: (b, vcol))],
        out_specs=pl.BlockSpec((ctx_len, d_q), lambda b: (b, 0)),
        compiler_params=_params("arbitrary"),
        name="ctx_attention",
    )(sink, qkv_ctx, qkv_ctx, qkv_ctx)


def _router_kernel(x_ref, sh_ref, sc_ref, wr_ref, br_ref, cnt0_ref, route_ref, cnt_ref, carry_ref):
    @pl.when(pl.program_id(0) == 0)
    def _():
        carry_ref[...] = cnt0_ref[...]

    h = _modulated_norm(x_ref[...], sh_ref[0], sc_ref[0])
    h_hi = h.astype(BF16)
    h_lo = (h - h_hi.astype(F32)).astype(BF16)
    logits = (jnp.dot(h_hi, wr_ref[0], preferred_element_type=F32)
              + (jnp.dot(h_hi, wr_ref[1], preferred_element_type=F32)
                 + jnp.dot(h_lo, wr_ref[0], preferred_element_type=F32))) + br_ref[...]
    tm = logits.shape[0]
    lane_i = lax.broadcasted_iota(jnp.int32, logits.shape, 1)
    lane = lane_i.astype(F32)
    neg = -jnp.inf
    lg = jnp.where(lane_i < N_GROUPS, logits, neg)
    mg = jnp.max(lg, axis=-1, keepdims=True)
    p_g = 1.0 / jnp.sum(jnp.exp(lg - mg), axis=-1, keepdims=True)
    g_sel = jnp.min(jnp.where(lg == mg, lane, float(ROUTE_W)), axis=-1, keepdims=True)
    lo = N_GROUPS + g_sel * EXPERTS_PER_GROUP
    lf = jnp.where((lane >= lo) & (lane < lo + EXPERTS_PER_GROUP), logits, neg)
    m1 = jnp.max(lf, axis=-1, keepdims=True)
    i1 = jnp.min(jnp.where(lf == m1, lane, float(ROUTE_W)), axis=-1, keepdims=True)
    lf2 = jnp.where(lane == i1, neg, lf)
    m2 = jnp.max(lf2, axis=-1, keepdims=True)
    i2 = jnp.min(jnp.where(lf2 == m2, lane, float(ROUTE_W)), axis=-1, keepdims=True)
    a2 = jnp.exp(m2 - m1)
    gate1 = p_g / (1.0 + a2)
    gate2 = gate1 * a2
    sel1 = lane == i1
    sel2 = lane == i2
    onehot = jnp.where(sel1 | sel2, 1.0, 0.0)
    r_i = lax.broadcasted_iota(jnp.int32, (tm, tm), 0)
    c_i = lax.broadcasted_iota(jnp.int32, (tm, tm), 1)
    tri = jnp.where(r_i > c_i, 1.0, 0.0).astype(BF16)
    before = jnp.dot(tri, onehot.astype(BF16), preferred_element_type=F32) + carry_ref[...]
    rank1 = jnp.sum(jnp.where(sel1, before, 0.0), axis=-1, keepdims=True)
    rank2 = jnp.sum(jnp.where(sel2, before, 0.0), axis=-1, keepdims=True)
    carry_ref[...] += jnp.sum(onehot, axis=0, keepdims=True)
    cnt_ref[...] = carry_ref[...]
    vals = (i1 - N_GROUPS, i2 - N_GROUPS, rank1, rank2, gate1, gate2)
    route = jnp.zeros_like(logits)
    for idx, v in enumerate(vals):
        route = jnp.where(lane_i == idx, v, route)
    route_ref[...] = route


def router(x, sh, sc, wr, br, cnt0, rows_per_seg):
    t, d = x.shape
    tm = _tile(rows_per_seg, 512)
    seg = lambda i: ((i * tm) // rows_per_seg, 0, 0)
    wr_hi = wr.astype(BF16)
    wr = jnp.stack([wr_hi, (wr - wr_hi.astype(F32)).astype(BF16)])
    return pl.pallas_call(
        _router_kernel,
        out_shape=(jax.ShapeDtypeStruct((t, ROUTE_W), F32), jax.ShapeDtypeStruct((1, ROUTE_W), F32)),
        grid=(t // tm,),
        in_specs=[pl.BlockSpec((tm, d), lambda i: (i, 0)),
                  pl.BlockSpec((1, 1, d), seg), pl.BlockSpec((1, 1, d), seg),
                  pl.BlockSpec((2, d, ROUTE_W), lambda i: (0, 0, 0)),
                  pl.BlockSpec((1, ROUTE_W), lambda i: (0, 0)),
                  pl.BlockSpec((1, ROUTE_W), lambda i: (0, 0))],
        out_specs=(pl.BlockSpec((tm, ROUTE_W), lambda i: (i, 0)),
                   pl.BlockSpec((1, ROUTE_W), lambda i: (0, 0))),
        scratch_shapes=[pltpu.VMEM((1, ROUTE_W), F32)],
        compiler_params=_params("arbitrary"),
        name="router",
    )(x, sh, sc, wr, br, cnt0)


def _token_copy(src, src_tok, dst, dst_tok, sem, rows):
    return pltpu.make_async_copy(src.at[pl.ds(src_tok * rows, rows)],
                                 dst.at[pl.ds(dst_tok * rows, rows)], sem)


def _to_token_rows(mat, rows_ref):
    tm, d = mat.shape
    rows = d // LANES
    for s in range(rows):
        rows_ref[pl.ds(s, tm, stride=rows), :] = mat[:, s * LANES:(s + 1) * LANES]


def _token_cols(rows_ref, s, tm, rows):
    return rows_ref[pl.ds(s, tm, stride=rows), :]


def _dispatch_kernel(dest_ref, pad_end_ref, nused_ref, x_ref, sh_ref, sc_ref, *rest, first, n_blocks):
    if first:
        xs_ref, h_ref, zero_ref, sem, zsem = rest
    else:
        _, xs_ref, h_ref, zero_ref, sem, zsem = rest
    tm, d = x_ref.shape
    rows = d // LANES
    base = pl.program_id(0) * tm

    if first:
        @pl.when(pl.program_id(0) == 0)
        def _():
            zero_ref[...] = jnp.zeros_like(zero_ref)
            blk_rows = MOE_BLOCK * rows

            def zero_block(blk):
                return pltpu.make_async_copy(zero_ref, xs_ref.at[pl.ds(blk * blk_rows, blk_rows)], zsem)

            def last_blocks(op):
                def body(e, carry):
                    hi = pad_end_ref[e]
                    lo = jnp.where(e == 0, 0, pad_end_ref[jnp.maximum(e - 1, 0)])

                    @pl.when(hi > lo)
                    def _():
                        op(zero_block(hi // MOE_BLOCK - 1))
                    return carry
                lax.fori_loop(0, N_EXPERTS, body, 0)

            def tail_blocks(op):
                def body(blk, carry):
                    op(zero_block(blk))
                    return carry
                lax.fori_loop(nused_ref[0], n_blocks, body, 0)

            last_blocks(lambda c: c.start())
            tail_blocks(lambda c: c.start())
            last_blocks(lambda c: c.wait())
            tail_blocks(lambda c: c.wait())

    step = pl.program_id(0)
    slot = step % 2
    _to_token_rows(_modulated_norm(x_ref[...], sh_ref[0], sc_ref[0]), h_ref.at[slot])

    def copies(which, row0, op):
        def body(r, carry):
            for k in range(2):
                op(_token_copy(h_ref.at[which], r, xs_ref, dest_ref[2 * (row0 + r) + k],
                               sem.at[which], rows))
            return carry
        lax.fori_loop(0, tm, body, 0)

    copies(slot, base, lambda c: c.start())

    @pl.when(step > 0)
    def _():
        copies(1 - slot, base - tm, lambda c: c.wait())

    @pl.when(step == pl.num_programs(0) - 1)
    def _():
        copies(slot, base, lambda c: c.wait())


def dispatch(dest, pad_end, nused, n_blocks, x, sh, sc, xs, rows_per_seg):
    t, d = x.shape
    rows = d // LANES
    tm = _tile(rows_per_seg, 256)
    first = xs is None
    seg = lambda i, *_: ((i * tm) // rows_per_seg, 0, 0)
    in_specs = [pl.BlockSpec((tm, d), lambda i, *_: (i, 0)),
                pl.BlockSpec((1, 1, d), seg), pl.BlockSpec((1, 1, d), seg)]
    args = [dest, pad_end, nused, x, sh, sc]
    if not first:
        in_specs.append(pl.BlockSpec(memory_space=pl.ANY))
        args.append(xs)
    return pl.pallas_call(
        functools.partial(_dispatch_kernel, first=first, n_blocks=n_blocks),
        out_shape=jax.ShapeDtypeStruct((n_blocks * MOE_BLOCK * rows, LANES), F32),
        grid_spec=pltpu.PrefetchScalarGridSpec(
            num_scalar_prefetch=3,
            grid=(t // tm,),
            in_specs=in_specs,
            out_specs=pl.BlockSpec(memory_space=pl.ANY),
            scratch_shapes=[pltpu.VMEM((2, tm * rows, LANES), F32),
                            pltpu.VMEM((MOE_BLOCK * rows, LANES), F32),
                            pltpu.SemaphoreType.DMA((2,)), pltpu.SemaphoreType.DMA]),
        input_output_aliases={} if first else {6: 0},
        compiler_params=_params("arbitrary"),
        name="moe_dispatch",
    )(*args)


def _expert_kernel(bexp_ref, nused_ref, xs_ref, w1_ref, w3_ref, w2_ref, ys_ref, xb, w13b, w2b):
    i = pl.program_id(0)
    e = bexp_ref[i]
    prev = bexp_ref[jnp.maximum(i - 1, 0)]
    tm, d = xb.shape
    rows = d // LANES
    de = w2b.shape[0]

    @pl.when((i == 0) | (e != prev))
    def _():
        w13b[:, :de] = w1_ref[0, 0].astype(BF16)
        w13b[:, de:] = w3_ref[0, 0].astype(BF16)
        w2b[...] = w2_ref[0, 0].astype(BF16)

    @pl.when(i < nused_ref[0])
    def _():
        for s in range(rows):
            xb[:, s * LANES:(s + 1) * LANES] = _token_cols(xs_ref, s, tm, rows).astype(BF16)
        h13 = jnp.dot(xb[...], w13b[...], preferred_element_type=F32)
        h1, h3 = h13[:, :de], h13[:, de:]
        a = (h1 * jax.nn.sigmoid(h1) * h3).astype(BF16)
        _to_token_rows(jnp.dot(a, w2b[...], preferred_element_type=F32), ys_ref)

    @pl.when(i >= nused_ref[0])
    def _():
        ys_ref[...] = jnp.zeros_like(ys_ref)


def experts(block_exp, nused, xs, w1, w3, w2, layer):
    d, de = w1.shape[-2:]
    blk_rows = MOE_BLOCK * (d // LANES)
    blk = lambda i, bexp, nu: (jnp.minimum(i, nu[0] - 1), 0)
    return pl.pallas_call(
        _expert_kernel,
        out_shape=jax.ShapeDtypeStruct(xs.shape, F32),
        grid_spec=pltpu.PrefetchScalarGridSpec(
            num_scalar_prefetch=2,
            grid=(xs.shape[0] // blk_rows,),
            in_specs=[pl.BlockSpec((blk_rows, LANES), blk),
                      pl.BlockSpec((1, 1, d, de), lambda i, bexp, nu: (layer, bexp[i], 0, 0)),
                      pl.BlockSpec((1, 1, d, de), lambda i, bexp, nu: (layer, bexp[i], 0, 0)),
                      pl.BlockSpec((1, 1, de, d), lambda i, bexp, nu: (layer, bexp[i], 0, 0))],
            out_specs=pl.BlockSpec((blk_rows, LANES), lambda i, bexp, nu: (i, 0)),
            scratch_shapes=[pltpu.VMEM((MOE_BLOCK, d), BF16), pltpu.VMEM((d, 2 * de), BF16),
                            pltpu.VMEM((de, d), BF16)]),
        compiler_params=_params("arbitrary"),
        name="moe_experts",
    )(block_exp, nused, xs, w1, w3, w2)


def _combine_kernel(dest_ref, x_ref, route_ref, g_ref, gain_ref, ys_ref, o_ref, y0_ref, y1_ref, sem,
                    *, final):
    tm, d = x_ref.shape
    rows = d // LANES
    base = pl.program_id(0) * tm

    step = pl.program_id(0)
    slot = step % 2

    def copies(which, row0, op):
        def body(r, carry):
            op(_token_copy(ys_ref, dest_ref[2 * (row0 + r)], y0_ref.at[which], r, sem.at[which], rows))
            op(_token_copy(ys_ref, dest_ref[2 * (row0 + r) + 1], y1_ref.at[which], r, sem.at[which],
                           rows))
            return carry
        lax.fori_loop(0, tm, body, 0)

    @pl.when(step == 0)
    def _():
        copies(slot, base, lambda c: c.start())

    @pl.when(step + 1 < pl.num_programs(0))
    def _():
        copies(1 - slot, base + tm, lambda c: c.start())

    copies(slot, base, lambda c: c.wait())
    y0_rows, y1_rows = y0_ref.at[slot], y1_ref.at[slot]
    route = route_ref[...]
    g0, g1 = route[:, 4:5], route[:, 5:6]
    ssq = jnp.zeros((tm, 1), F32)
    for s in range(rows):
        cols = slice(s * LANES, (s + 1) * LANES)
        y = g0 * _token_cols(y0_rows, s, tm, rows) + g1 * _token_cols(y1_rows, s, tm, rows)
        out = x_ref[:, cols] + g_ref[0][:, cols] * y
        o_ref[:, cols] = out
        ssq = ssq + jnp.sum(out * out, axis=-1, keepdims=True)
    if final:
        o_ref[...] = o_ref[...] * lax.rsqrt(ssq * (1.0 / d) + EPS) * gain_ref[...]


def combine(dest, x, route, gate, gain, ys, rows_per_seg, final):
    t, d = x.shape
    rows = d // LANES
    tm = _tile(rows_per_seg, 256)
    return pl.pallas_call(
        functools.partial(_combine_kernel, final=final),
        out_shape=jax.ShapeDtypeStruct((t, d), F32),
        grid_spec=pltpu.PrefetchScalarGridSpec(
            num_scalar_prefetch=1,
            grid=(t // tm,),
            in_specs=[pl.BlockSpec((tm, d), lambda i, dest_ref: (i, 0)),
                      pl.BlockSpec((tm, ROUTE_W), lambda i, dest_ref: (i, 0)),
                      pl.BlockSpec((1, 1, d), lambda i, dest_ref: ((i * tm) // rows_per_seg, 0, 0)),
                      pl.BlockSpec((1, d), lambda i, dest_ref: (0, 0)),
                      pl.BlockSpec(memory_space=pl.ANY)],
            out_specs=pl.BlockSpec((tm, d), lambda i, dest_ref: (i, 0)),
            scratch_shapes=[pltpu.VMEM((2, tm * rows, LANES), F32),
                            pltpu.VMEM((2, tm * rows, LANES), F32),
                            pltpu.SemaphoreType.DMA((2,))]),
        compiler_params=_params("arbitrary"),
        name="moe_combine",
    )(dest, x, route, gate, gain, ys)


def _slot_plan(route_list, counts):
    cnt = counts[0, N_GROUPS:N_GROUPS + N_EXPERTS].astype(jnp.int32)
    padded = (cnt + MOE_BLOCK - 1) // MOE_BLOCK * MOE_BLOCK
    pad_end = jnp.cumsum(padded)
    pad_start = pad_end - padded
    n_assign = 2 * sum(r.shape[0] for r in route_list)
    n_blocks = -(-n_assign // MOE_BLOCK) + N_EXPERTS
    block_start = jnp.arange(n_blocks, dtype=jnp.int32) * MOE_BLOCK
    block_exp = jnp.minimum(jnp.sum(block_start[:, None] >= pad_end[None, :], axis=1),
                            N_EXPERTS - 1).astype(jnp.int32)
    nused = (pad_end[-1:] // MOE_BLOCK).astype(jnp.int32)
    dests = []
    for r in route_list:
        e = r[:, 0:2].astype(jnp.int32)
        dests.append((pad_start[e] + r[:, 2:4].astype(jnp.int32)).reshape(-1))
    return dests, pad_end.astype(jnp.int32), block_exp, nused, n_blocks


def hier_moe_layer(streams, layer, wr, br, w1, w3, w2, gain, final):
    counts = jnp.zeros((1, ROUTE_W), F32)
    routes = []
    for x, sh, sc, _, rps in streams:
        route, counts = router(x, sh, sc, wr, br, counts, rps)
        routes.append(route)
    dests, pad_end, block_exp, nused, n_blocks = _slot_plan(routes, counts)
    xs = None
    for (x, sh, sc, _, rps), dest in zip(streams, dests):
        xs = dispatch(dest, pad_end, nused, n_blocks, x, sh, sc, xs, rps)
    ys = experts(block_exp, nused, xs, w1, w3, w2, layer)
    return [combine(dest, x, route, g, gain, ys, rps, final)
            for (x, _, _, g, rps), dest, route in zip(streams, dests, routes)]


def kernel(x, c, ctx, c_ctx, mod_w, mod_b, hy_w_in, hy_conv_w, hy_conv_b, hy_f_w1, hy_f_b1,
           hy_f_freq1, hy_f_w2, hy_f_b2, hy_f_freq2, hy_f_w3, hy_bias, hy_w_out,
           at_w_qkv, at_sink, at_w_o, moe_wg, moe_bg, moe_we, moe_be, moe_w1, moe_w3, moe_w2,
           final_gain):
    batch, seq_len, d = x.shape
    ctx_len = ctx.shape[1]
    depth = mod_w.shape[0]
    d_q = at_w_qkv.shape[2] - 2 * N_KV_HEADS * HEAD_DIM
    d_kv = N_KV_HEADS * HEAD_DIM
    n_ctx = batch * ctx_len

    xl = x.reshape(batch * seq_len, d)
    xc = ctx.reshape(n_ctx, d)

    n_cond = batch + 1
    cond = jnp.concatenate([c, c_ctx[None, :], jnp.zeros((-n_cond % 8, d), F32)], axis=0)
    mods = adaln(cond, mod_w, mod_b)

    def mod_slices(i):
        parts = [mods[i, :, j * d:(j + 1) * d] for j in range(6)]
        lat = [p[:batch].reshape(batch, 1, d) for p in parts]
        cx = [p[batch:batch + 1].reshape(1, 1, d) for p in parts]
        return lat, cx

    def use_fft(n):
        return n >= 512 and (2 * n) % (FFT_N1 * 2 * SUB) == 0

    tables = {n: (fft_tables(n) if use_fft(n) else dft_tables(n)) for n in {seq_len, ctx_len}}
    rtabs = rope_tables(seq_len)
    gain = final_gain.reshape(1, d)

    for i in range(depth):
        last = i == depth - 1
        j = i // 2
        (sh1, sc1, g1, sh2, sc2, g2), (sh1c, sc1c, g1c, sh2c, sc2c, g2c) = mod_slices(i)
        if i % 2 == 0:
            w_in = hy_w_in[j].astype(BF16)
            w_out = hy_w_out[j].astype(BF16)
            filt = (hy_f_w1[j], hy_f_b1[j], hy_f_freq1[j], hy_f_w2[j], hy_f_b2[j], hy_f_freq2[j],
                    hy_f_w3[j])
            bias = hy_bias[j].reshape(2, 1, d)

            def mixer(xs_, sh, sc, g, n_seq, rps):
                u = modmm(xs_, sh, sc, w_in, rps)
                u3 = short_conv(u, hy_conv_w[j], hy_conv_b[j], n_seq)
                if use_fft(n_seq):
                    taps = hyena_filter_taps(n_seq, *filt, d, even_odd=False)
                    spec = fft_filter_spectrum(tables[n_seq], *taps)
                    z = hyena_long_convs_fft(u3, tables[n_seq], spec, bias, n_seq, d)
                else:
                    taps = hyena_filter_taps(n_seq, *filt, d, even_odd=True)
                    spec = filter_spectrum(*tables[n_seq], *taps)
                    z = hyena_long_convs(u3, tables[n_seq], spec, bias, n_seq, d)
                return mm_res(z, w_out, xs_, g, rps)

            xl = mixer(xl, sh1, sc1, g1, seq_len, seq_len)
            if not last:
                xc = mixer(xc, sh1c, sc1c, g1c, ctx_len, n_ctx)
        else:
            w_qkv = at_w_qkv[j].astype(BF16)
            w_o = at_w_o[j].astype(BF16)
            sink = at_sink[j]
            qkv = modmm(xl, sh1, sc1, w_qkv, seq_len, rope=(d_q, d_q + d_kv), rope_tabs=rtabs)
            qkv_c = modmm(xc, sh1c, sc1c, w_qkv, n_ctx)
            o = window_attention(qkv, qkv_c, sink, batch, seq_len, ctx_len, d_q, d_kv)
            xl = mm_res(o, w_o, xl, g1, seq_len)
            if not last:
                o_c = ctx_attention(qkv_c, sink, batch, ctx_len, d_q, d_kv)
                xc = mm_res(o_c, w_o, xc, g1c, n_ctx)

        wr = jnp.concatenate([moe_wg[i], moe_we[i],
                              jnp.zeros((d, ROUTE_W - N_GROUPS - N_EXPERTS), F32)], axis=1)
        br = jnp.concatenate([moe_bg[i], moe_be[i],
                              jnp.zeros((ROUTE_W - N_GROUPS - N_EXPERTS,), F32)]).reshape(1, ROUTE_W)
        streams = [(xl, sh2, sc2, g2, seq_len)]
        if not last:
            streams.append((xc, sh2c, sc2c, g2c, n_ctx))
        outs = hier_moe_layer(streams, i, wr, br, moe_w1, moe_w3, moe_w2, gain, last)
        xl = outs[0]
        if not last:
            xc = outs[1]

    return xl.reshape(batch, seq_len, d)
```

```python
import functools
import math

import jax
import jax.numpy as jnp
from jax import lax
from jax.experimental import pallas as pl
from jax.experimental.pallas import tpu as pltpu

F32 = jnp.float32
BF16 = jnp.bfloat16
HIGHEST = lax.Precision.HIGHEST

EPS = 1e-6
GRID_W = 64
HEAD_DIM = 64
N_KV_HEADS = 4
WINDOW = 128
Q_BLOCK = 128
ROPE_PAIRS = HEAD_DIM // 4
ROPE_BASE = 10000.0
N_GROUPS = 8
EXPERTS_PER_GROUP = 8
N_EXPERTS = N_GROUPS * EXPERTS_PER_GROUP
MOE_BLOCK = 256
HY_EMB = 33
HY_BANDS = (HY_EMB - 1) // 2
HY_FAST_DECAY_PCT = 0.3
HY_SLOW_DECAY_PCT = 1.5
HY_DECAY_TARGET = 1e-2

LANES = 128
ROUTE_W = LANES
VMEM_LIMIT = 56 * 1024 * 1024


def _tile(n, pref):
    if n <= pref:
        return n
    t = pref
    while n % t:
        t -= 1
    return t


def _params(*sem):
    return pltpu.CompilerParams(dimension_semantics=sem, vmem_limit_bytes=VMEM_LIMIT)


def _modulated_norm(x, sh, sc):
    y = x * lax.rsqrt(jnp.mean(x * x, axis=-1, keepdims=True) + EPS)
    return y * (1.0 + sc) + sh


def _adaln_kernel(c_ref, w_ref, b_ref, o_ref):
    c = c_ref[...]
    a = c * jax.nn.sigmoid(c)
    o_ref[0] = jnp.dot(a, w_ref[0], precision=HIGHEST, preferred_element_type=F32) + b_ref[0]


def adaln(cond, mod_w, mod_b):
    depth, d, n = mod_w.shape
    r = cond.shape[0]
    tn = _tile(n, 1024)
    return pl.pallas_call(
        _adaln_kernel,
        out_shape=jax.ShapeDtypeStruct((depth, r, n), F32),
        grid=(depth, n // tn),
        in_specs=[pl.BlockSpec((r, d), lambda l, j: (0, 0)),
                  pl.BlockSpec((1, d, tn), lambda l, j: (l, 0, j)),
                  pl.BlockSpec((1, 1, tn), lambda l, j: (l, 0, j))],
        out_specs=pl.BlockSpec((1, r, tn), lambda l, j: (l, 0, j)),
        compiler_params=_params("arbitrary", "arbitrary"),
        name="adaln",
    )(cond, mod_w, mod_b.reshape(depth, 1, n))


def _modmm_kernel(x_ref, sh_ref, sc_ref, w_ref, *rest, rope):
    if rope is None:
        o_ref, h_ref = rest
    else:
        cos_ref, sa_ref, sb_ref, o_ref, h_ref = rest

    @pl.when(pl.program_id(1) == 0)
    def _():
        h_ref[...] = _modulated_norm(x_ref[...], sh_ref[0], sc_ref[0]).astype(BF16)

    acc = jnp.dot(h_ref[...], w_ref[...], preferred_element_type=F32)
    if rope is None:
        o_ref[...] = acc.astype(o_ref.dtype)
        return
    d_q, d_qk = rope
    tn = acc.shape[1]
    for c in range(tn // LANES):
        col0 = pl.program_id(1) * tn + c * LANES
        x = acc[:, c * LANES:(c + 1) * LANES]

        @pl.when(col0 < d_qk)
        def _(x=x, c=c, col0=col0):
            y = (x * cos_ref[...] + pltpu.roll(x, LANES - ROPE_PAIRS, 1) * sa_ref[...]
                 + pltpu.roll(x, ROPE_PAIRS, 1) * sb_ref[...])
            y = y * jnp.where(col0 < d_q, Q_SCALE, 1.0).astype(F32)
            o_ref[:, c * LANES:(c + 1) * LANES] = y.astype(o_ref.dtype)

        @pl.when(col0 >= d_qk)
        def _(x=x, c=c):
            o_ref[:, c * LANES:(c + 1) * LANES] = x.astype(o_ref.dtype)


def modmm(x, sh, sc, w, rows_per_seg, rope=None, rope_tabs=None):
    t, d = x.shape
    n = w.shape[1]
    tm = _tile(rows_per_seg, 1024)
    tn = _tile(n, 1024)
    seg = lambda i, j: ((i * tm) // rows_per_seg, 0, 0)
    in_specs = [pl.BlockSpec((tm, d), lambda i, j: (i, 0)),
                pl.BlockSpec((1, 1, d), seg),
                pl.BlockSpec((1, 1, d), seg),
                pl.BlockSpec((d, tn), lambda i, j: (0, j))]
    args = [x, sh, sc, w]
    if rope is not None:
        nb = rows_per_seg // tm
        in_specs += [pl.BlockSpec((tm, LANES), lambda i, j: (i % nb, 0))] * 3
        args += list(rope_tabs)
    return pl.pallas_call(
        functools.partial(_modmm_kernel, rope=rope),
        out_shape=jax.ShapeDtypeStruct((t, n), BF16),
        grid=(t // tm, n // tn),
        in_specs=in_specs,
        out_specs=pl.BlockSpec((tm, tn), lambda i, j: (i, j)),
        scratch_shapes=[pltpu.VMEM((tm, d), BF16)],
        compiler_params=_params("arbitrary", "arbitrary"),
        name="modmm",
    )(*args)


def _mmres_kernel(a_ref, w_ref, r_ref, g_ref, o_ref):
    acc = jnp.dot(a_ref[...], w_ref[...], preferred_element_type=F32)
    o_ref[...] = r_ref[...] + g_ref[0] * acc


def mm_res(a, w, res, gate, rows_per_seg):
    t, k = a.shape
    n = w.shape[1]
    tm = _tile(rows_per_seg, 1024)
    tn = _tile(n, 1024)
    return pl.pallas_call(
        _mmres_kernel,
        out_shape=jax.ShapeDtypeStruct((t, n), F32),
        grid=(t // tm, n // tn),
        in_specs=[pl.BlockSpec((tm, k), lambda i, j: (i, 0)),
                  pl.BlockSpec((k, tn), lambda i, j: (0, j)),
                  pl.BlockSpec((tm, tn), lambda i, j: (i, j)),
                  pl.BlockSpec((1, 1, tn), lambda i, j: ((i * tm) // rows_per_seg, 0, j))],
        out_specs=pl.BlockSpec((tm, tn), lambda i, j: (i, j)),
        compiler_params=_params("arbitrary", "arbitrary"),
        name="mm_res",
    )(a, w, res, gate)


def _sconv_kernel(u_ref, w_ref, b_ref, o_ref):
    x = u_ref[...].astype(F32)
    n = x.shape[0]
    row = lax.broadcasted_iota(jnp.int32, x.shape, 0)
    prev = jnp.where(row == 0, 0.0, pltpu.roll(x, 1, 0))
    nxt = jnp.where(row == n - 1, 0.0, pltpu.roll(x, n - 1, 0))
    y = prev * w_ref[0:1, :] + x * w_ref[1:2, :] + nxt * w_ref[2:3, :] + b_ref[...]
    o_ref[...] = y.astype(o_ref.dtype)


def short_conv(u, w, b, seq_len):
    t, c = u.shape
    tc = _tile(c, 256)
    return pl.pallas_call(
        _sconv_kernel,
        out_shape=jax.ShapeDtypeStruct((t, c), BF16),
        grid=(t // seq_len, c // tc),
        in_specs=[pl.BlockSpec((seq_len, tc), lambda s, j: (s, j)),
                  pl.BlockSpec((3, tc), lambda s, j: (0, j)),
                  pl.BlockSpec((1, tc), lambda s, j: (0, j))],
        out_specs=pl.BlockSpec((seq_len, tc), lambda s, j: (s, j)),
        compiler_params=_params("arbitrary", "arbitrary"),
        name="short_conv",
    )(u, w, b.reshape(1, c))


def _filter_mlp_kernel(feat_ref, w1_ref, b1_ref, f1_ref, w2_ref, b2_ref, f2_ref, h_ref):
    h = jnp.sin(f1_ref[...] * (jnp.dot(feat_ref[...], w1_ref[...], precision=HIGHEST,
                                       preferred_element_type=F32) + b1_ref[...]))
    h_ref[...] = jnp.sin(f2_ref[...] * (jnp.dot(h, w2_ref[...], precision=HIGHEST,
                                                preferred_element_type=F32) + b2_ref[...]))


def _filter_kernel(hf_ref, hb_ref, tf_ref, tb_ref, w3f_ref, w3b_ref, dl_ref, p_ref, q_ref, *,
                   even_odd):
    dl = jnp.abs(dl_ref[...])
    hf = jnp.dot(hf_ref[...], w3f_ref[...], precision=HIGHEST,
                 preferred_element_type=F32) * jnp.exp(-tf_ref[...] * dl)
    hb = jnp.dot(hb_ref[...], w3b_ref[...], precision=HIGHEST,
                 preferred_element_type=F32) * jnp.exp(-tb_ref[...] * dl)
    row = lax.broadcasted_iota(jnp.int32, hf.shape, 0)
    hb = jnp.where(row == 0, 0.0, hb)
    s = lax.rsqrt(jnp.sum(hf * hf + hb * hb, axis=0, keepdims=True) + EPS)
    p, q = ((hf + hb) * s, (hf - hb) * s) if even_odd else (hf * s, hb * s)
    p_ref[0] = p.astype(p_ref.dtype).reshape(p_ref.shape[1:])
    q_ref[0] = q.astype(q_ref.dtype).reshape(q_ref.shape[1:])


def hyena_filter_taps(seq_len, w1, b1, f1, w2, b2, f2, w3, d, even_odd):
    t = jnp.linspace(0.0, 1.0, seq_len, dtype=F32)[:, None]
    omega = (2.0 * math.pi / seq_len) * jnp.arange(seq_len, dtype=F32)[:, None]
    bands = jnp.linspace(1e-4, HY_BANDS - 1, HY_BANDS, dtype=F32)[None, :]
    feats = jnp.concatenate([t, jnp.cos(bands * omega), -jnp.sin(bands * omega),
                             jnp.zeros((seq_len, LANES - HY_EMB), F32)], axis=-1)
    if not even_odd:
        feats = jnp.concatenate([feats, feats[:1], feats[:0:-1]], axis=0)
    w1p = jnp.concatenate([w1, jnp.zeros((LANES - HY_EMB, w1.shape[1]), F32)], axis=0)
    fw = w1.shape[1]
    rows = feats.shape[0]
    vec = lambda a: a.reshape(1, fw)
    h = pl.pallas_call(
        _filter_mlp_kernel,
        out_shape=jax.ShapeDtypeStruct((rows, fw), F32),
        compiler_params=_params(),
        name="hyena_filter_mlp",
    )(feats, w1p, vec(b1), vec(f1), w2, vec(b2), vec(f2))
    tcol = feats[:, 0:1]
    max_decay = math.log(HY_DECAY_TARGET) / HY_FAST_DECAY_PCT
    min_decay = math.log(HY_DECAY_TARGET) / HY_SLOW_DECAY_PCT
    deltas = jnp.linspace(min_decay, max_decay, d, dtype=F32).reshape(1, d)
    tn = _tile(d, 256)
    nb = d // tn
    bsel = 0 if even_odd else 1
    if even_odd:
        oshape, oblock = (2, seq_len, d), (1, seq_len, tn)
        omap = lambda o, j: (o, 0, j)
    else:
        n2 = 2 * seq_len // FFT_N1
        oshape, oblock = (2, seq_len // n2, n2, d), (1, seq_len // n2, n2, tn)
        omap = lambda o, j: (o, 0, 0, j)
    out = jax.ShapeDtypeStruct(oshape, BF16)
    return pl.pallas_call(
        functools.partial(_filter_kernel, even_odd=even_odd),
        out_shape=(out, out),
        grid=(2, nb),
        in_specs=[pl.BlockSpec((seq_len, fw), lambda o, j: (0, 0)),
                  pl.BlockSpec((seq_len, fw), lambda o, j: (bsel, 0)),
                  pl.BlockSpec((seq_len, 1), lambda o, j: (0, 0)),
                  pl.BlockSpec((seq_len, 1), lambda o, j: (bsel, 0)),
                  pl.BlockSpec((fw, tn), lambda o, j: (0, (2 * o) * nb + j)),
                  pl.BlockSpec((fw, tn), lambda o, j: (0, (2 * o + 1) * nb + j)),
                  pl.BlockSpec((1, tn), lambda o, j: (0, j))],
        out_specs=(pl.BlockSpec(oblock, omap), pl.BlockSpec(oblock, omap)),
        compiler_params=_params("arbitrary", "arbitrary"),
        name="hyena_filter",
    )(h, h, tcol, tcol, w3, w3, deltas)


def _alt_sum(x):
    row = lax.broadcasted_iota(jnp.int32, x.shape, 0)
    return jnp.sum(jnp.where((row & 1) == 1, -x, x), axis=0, keepdims=True)


def _spectrum_kernel(c_ref, s_ref, e_ref, o_ref, kr_ref, km_ref):
    kr_ref[0] = jnp.dot(c_ref[...], e_ref[0], preferred_element_type=F32)
    km = jnp.dot(s_ref[...], o_ref[0], preferred_element_type=F32)

    @pl.when(pl.program_id(2) != 0)
    def _():
        km_ref[0] = km

    @pl.when(pl.program_id(2) == 0)
    def _():
        row = lax.broadcasted_iota(jnp.int32, km.shape, 0)
        km_ref[0] = jnp.where(row == 0, _alt_sum(e_ref[0].astype(F32)), km)


def filter_spectrum(cmat, smat, e_taps, o_taps):
    _, n, d = e_taps.shape
    tk = _tile(n, 512)
    tn = _tile(d, 512)
    out = jax.ShapeDtypeStruct((2, n, d), F32)
    return pl.pallas_call(
        _spectrum_kernel,
        out_shape=(out, out),
        grid=(2, d // tn, n // tk),
        in_specs=[pl.BlockSpec((tk, n), lambda o, j, k: (k, 0)),
                  pl.BlockSpec((tk, n), lambda o, j, k: (k, 0)),
                  pl.BlockSpec((1, n, tn), lambda o, j, k: (o, 0, j)),
                  pl.BlockSpec((1, n, tn), lambda o, j, k: (o, 0, j))],
        out_specs=(pl.BlockSpec((1, tk, tn), lambda o, j, k: (o, k, j)),
                   pl.BlockSpec((1, tk, tn), lambda o, j, k: (o, k, j))),
        compiler_params=_params("arbitrary", "arbitrary", "arbitrary"),
        name="filter_spectrum",
    )(cmat, smat, e_taps, o_taps)


def _dft_fwd_kernel(c_ref, s_ref, u_ref, kr_ref, km_ref, yr_ref, q_ref):
    u = u_ref[...]
    a = jnp.dot(c_ref[...], u, preferred_element_type=F32)
    b = jnp.dot(s_ref[...], u, preferred_element_type=F32)
    kr = kr_ref[0]
    km = km_ref[0]
    yr = a * kr - b * km
    q = b * kr + a * km

    @pl.when(pl.program_id(2) != 0)
    def _():
        yr_ref[...] = yr.astype(yr_ref.dtype)
        q_ref[...] = q.astype(q_ref.dtype)

    @pl.when(pl.program_id(2) == 0)
    def _():
        row = lax.broadcasted_iota(jnp.int32, yr.shape, 0)
        ynyq = _alt_sum(u.astype(F32)) * km[0:1, :]
        yr_ref[...] = jnp.where(row == 0, 0.5 * yr, yr).astype(yr_ref.dtype)
        q_ref[...] = jnp.where(row == 0, 0.5 * ynyq, q).astype(q_ref.dtype)


def dft_fwd(cmat, smat, u, col0, kr, km, order, seq_len, d):
    t = u.shape[0]
    tk = _tile(seq_len, 512)
    tn = _tile(d, 512)
    cb = col0 // tn
    out = jax.ShapeDtypeStruct((t, d), BF16)
    nk = seq_len // tk
    return pl.pallas_call(
        _dft_fwd_kernel,
        out_shape=(out, out),
        grid=(t // seq_len, d // tn, nk),
        in_specs=[pl.BlockSpec((tk, seq_len), lambda b, j, k: (k, 0)),
                  pl.BlockSpec((tk, seq_len), lambda b, j, k: (k, 0)),
                  pl.BlockSpec((seq_len, tn), lambda b, j, k: (b, cb + j)),
                  pl.BlockSpec((1, tk, tn), lambda b, j, k: (order, k, j)),
                  pl.BlockSpec((1, tk, tn), lambda b, j, k: (order, k, j))],
        out_specs=(pl.BlockSpec((tk, tn), lambda b, j, k: (b * nk + k, j)),
                   pl.BlockSpec((tk, tn), lambda b, j, k: (b * nk + k, j))),
        compiler_params=_params("arbitrary", "arbitrary", "arbitrary"),
        name="dft_fwd",
    )(cmat, smat, u, kr, km)


def _dft_inv_kernel(c_ref, s_ref, yr_ref, q_ref, u_ref, g_ref, bias_ref, o_ref, *, seq_len):
    acc = jnp.dot(c_ref[...], yr_ref[...], preferred_element_type=F32)
    acc += jnp.dot(s_ref[...], q_ref[...], preferred_element_type=F32)
    tt = acc.shape[0]
    t_idx = pl.program_id(2) * tt + lax.broadcasted_iota(jnp.int32, acc.shape, 0)
    half_nyq = q_ref[0:1, :].astype(F32)
    acc += jnp.where((t_idx & 1) == 1, -half_nyq, half_nyq)
    u = u_ref[...].astype(F32)
    y = acc * (1.0 / seq_len) + u * bias_ref[0]
    o_ref[...] = (g_ref[...].astype(F32) * y).astype(o_ref.dtype)


def dft_inv(cmat, smat, yr, q, u, ucol0, gsrc, gcol0, bias, order, seq_len, d):
    t = yr.shape[0]
    tt = _tile(seq_len, 512)
    tn = _tile(d, 512)
    ub, gb = ucol0 // tn, gcol0 // tn
    nt = seq_len // tt
    return pl.pallas_call(
        functools.partial(_dft_inv_kernel, seq_len=seq_len),
        out_shape=jax.ShapeDtypeStruct((t, d), BF16),
        grid=(t // seq_len, d // tn, nt),
        in_specs=[pl.BlockSpec((tt, seq_len), lambda b, j, k: (k, 0)),
                  pl.BlockSpec((tt, seq_len), lambda b, j, k: (k, 0)),
                  pl.BlockSpec((seq_len, tn), lambda b, j, k: (b, j)),
                  pl.BlockSpec((seq_len, tn), lambda b, j, k: (b, j)),
                  pl.BlockSpec((tt, tn), lambda b, j, k: (b * nt + k, ub + j)),
                  pl.BlockSpec((tt, tn), lambda b, j, k: (b * nt + k, gb + j)),
                  pl.BlockSpec((1, 1, tn), lambda b, j, k: (order, 0, j))],
        out_specs=pl.BlockSpec((tt, tn), lambda b, j, k: (b * nt + k, j)),
        compiler_params=_params("arbitrary", "arbitrary", "arbitrary"),
        name="dft_inv",
    )(cmat, smat, yr, q, u, gsrc, bias)


def dft_tables(seq_len):
    n = jnp.arange(seq_len, dtype=jnp.int32)
    ang = ((n[:, None] * n[None, :]) % (2 * seq_len)).astype(F32) * (math.pi / seq_len)
    return jnp.cos(ang).astype(BF16), jnp.sin(ang).astype(BF16)


def hyena_long_convs(u3, tables, spectrum, bias, seq_len, d):
    cmat, smat = tables
    kr, km = spectrum
    yr, q = dft_fwd(cmat, smat, u3, 0, kr, km, 0, seq_len, d)
    z = dft_inv(cmat, smat, yr, q, u3, 0, u3, d, bias, 0, seq_len, d)
    yr, q = dft_fwd(cmat, smat, z, 0, kr, km, 1, seq_len, d)
    return dft_inv(cmat, smat, yr, q, z, 0, u3, 2 * d, bias, 1, seq_len, d)


FFT_N1 = 64
SUB = 8


def fft_tables(seq_len):
    n = 2 * seq_len
    n1, n2 = FFT_N1, n // FFT_N1
    two_pi = 2.0 * math.pi
    eye = jnp.eye(SUB, dtype=F32)
    k1 = jnp.arange(n1, dtype=jnp.int32)
    h1 = jnp.arange(n1 // 2, dtype=jnp.int32)
    ang = ((k1[:, None] * h1[None, :]) % n1).astype(F32) * (two_pi / n1)
    ca = jnp.stack([jnp.cos(ang), -jnp.sin(ang)], axis=1)
    ka = jnp.einsum('krn,jm->krjnm', ca, eye).reshape(n1 * 2 * SUB, (n1 // 2) * SUB)
    c2 = jnp.stack([jnp.cos(ang.T), -jnp.sin(ang.T)], axis=2)
    ka2 = jnp.einsum('tkr,jm->tjkrm', c2, eye).reshape((n1 // 2) * SUB, n1 * 2 * SUB)
    k2 = jnp.arange(n2 // 2, dtype=jnp.int32)
    m2 = jnp.arange(n2, dtype=jnp.int32)
    k = k1[:, None, None] + n1 * k2[None, :, None]
    th = ((k * m2[None, None, :]) % n).astype(F32) * (two_pi / n)
    c, s = jnp.cos(th), jnp.sin(th)
    fb = jnp.concatenate([jnp.concatenate([c, s], axis=2),
                          jnp.concatenate([-s, c], axis=2)], axis=1)
    ct, st = jnp.swapaxes(c, 1, 2), jnp.swapaxes(s, 1, 2)
    gb = jnp.concatenate([jnp.concatenate([ct, -st], axis=2),
                          jnp.concatenate([st, ct], axis=2)], axis=1)
    return tuple(a.astype(BF16) for a in (ka, fb, gb, ka2))


def _fft_stage_a(src_ref, ka_ref, a_ref):
    n1h, n2, tn = src_ref.shape
    for g in range(n2 // SUB):
        rhs = src_ref[:, g * SUB:(g + 1) * SUB, :].reshape(n1h * SUB, tn).astype(BF16)
        out = jnp.dot(ka_ref[...], rhs, preferred_element_type=F32)
        a_ref[:, :, g * SUB:(g + 1) * SUB, :] = out.reshape(FFT_N1, 2, SUB, tn)


def _fft_stage_b(fb_ref, a_ref, k1):
    _, _, n2, tn = a_ref.shape
    slab = a_ref[k1].reshape(2 * n2, tn).astype(BF16)
    return jnp.dot(fb_ref[k1], slab, preferred_element_type=F32)


def _block_alt_sum(src_ref):
    return _alt_sum(jnp.sum(src_ref[...], axis=0))


def _fftspec_kernel(tf_ref, tb_ref, ka_ref, fb_ref, kf_ref, knyq_ref, src_ref, a_ref):
    h = a_ref.shape[2] // 2
    tn = a_ref.shape[3]
    nyq = None
    for which, t_ref in enumerate((tf_ref, tb_ref)):
        src_ref[...] = t_ref[0].astype(F32)
        part = _block_alt_sum(src_ref)
        nyq = part if nyq is None else nyq + part
        _fft_stage_a(src_ref, ka_ref, a_ref)

        def body(k1, carry, which=which):
            x = _fft_stage_b(fb_ref, a_ref, k1).reshape(2, h, tn)
            if which == 0:
                kf_ref[0, k1] = x
            else:
                sign = 1.0 - 2.0 * jnp.asarray(k1 & 1, F32)
                kf_ref[0, k1] = kf_ref[0, k1] + sign * x
            return carry

        lax.fori_loop(0, FFT_N1, body, 0, unroll=4)
    knyq_ref[0] = nyq
    dc = kf_ref[0, 0]
    row = lax.broadcasted_iota(jnp.int32, dc.shape, 1)
    kf_ref[0, 0] = jnp.where(row == 0, 0.5 * dc, dc)


def _const_spec(shape):
    return pl.BlockSpec(shape, lambda *_: (0,) * len(shape), pipeline_mode=pl.Buffered(1))


def fft_filter_spectrum(tabs, taps_f, taps_b):
    ka, fb, _, _ = tabs
    _, n1h, n2, d = taps_f.shape
    tn = _tile(d, 256)
    tap = pl.BlockSpec((1, n1h, n2, tn), lambda o, j: (o, 0, 0, j))
    return pl.pallas_call(
        _fftspec_kernel,
        out_shape=(jax.ShapeDtypeStruct((2, FFT_N1, 2, n2 // 2, d), F32),
                   jax.ShapeDtypeStruct((2, 1, d), F32)),
        grid=(2, d // tn),
        in_specs=[tap, tap, _const_spec(ka.shape), _const_spec(fb.shape)],
        out_specs=(pl.BlockSpec((1, FFT_N1, 2, n2 // 2, tn), lambda o, j: (o, 0, 0, 0, j)),
                   pl.BlockSpec((1, 1, tn), lambda o, j: (o, 0, j))),
        scratch_shapes=[pltpu.VMEM((n1h, n2, tn), F32), pltpu.VMEM((FFT_N1, 2, n2, tn), F32)],
        compiler_params=_params("arbitrary", "arbitrary"),
        name="fft_filter_spectrum",
    )(taps_f, taps_b, ka, fb)


def _fftconv_kernel(u_ref, g_ref, bias_ref, kf_ref, knyq_ref, ka_ref, fb_ref, gb_ref, ka2_ref,
                    o_ref, src_ref, a_ref):
    n1h, n2, tn = src_ref.shape
    h = n2 // 2
    seq_len = n1h * n2
    src_ref[...] = u_ref[...].astype(F32)
    ynyq = _block_alt_sum(src_ref) * knyq_ref[0] * (0.5 / seq_len)
    _fft_stage_a(src_ref, ka_ref, a_ref)

    def body(k1, carry):
        x = _fft_stage_b(fb_ref, a_ref, k1)
        xr, xi = x[:h], x[h:]
        kr, ki = kf_ref[0, k1, 0], kf_ref[0, k1, 1]
        y = jnp.concatenate([xr * kr - xi * ki, xr * ki + xi * kr], axis=0).astype(BF16)
        z = jnp.dot(gb_ref[k1], y, preferred_element_type=F32)
        a_ref[k1] = z.reshape(2, n2, tn)
        return carry

    lax.fori_loop(0, FFT_N1, body, 0, unroll=True)
    bias = bias_ref[0]
    pair = 2 * SUB
    row = lax.broadcasted_iota(jnp.int32, (n1h, pair, tn), 1)
    nyq = jnp.where((row & 1) == 1, -ynyq, ynyq)
    for gg in range(n2 // pair):
        parts = []
        for g in (2 * gg, 2 * gg + 1):
            zg = a_ref[:, :, g * SUB:(g + 1) * SUB, :].reshape(FFT_N1 * 2 * SUB, tn).astype(BF16)
            acc = jnp.dot(ka2_ref[...], zg, preferred_element_type=F32)
            parts.append(acc.reshape(n1h, SUB, tn))
        acc = jnp.concatenate(parts, axis=1)
        rows = slice(gg * pair, (gg + 1) * pair)
        y = acc * (1.0 / seq_len) + nyq + src_ref[:, rows, :] * bias
        o_ref[:, rows, :] = (g_ref[:, rows, :].astype(F32) * y).astype(o_ref.dtype)


def fft_long_conv(tabs, spec, u, ucol0, gsrc, gcol0, bias, order, seq_len, d):
    ka, fb, gb, ka2 = tabs
    kf, knyq = spec
    nb, n2, _ = u.shape
    n1h = seq_len // n2
    tn = _tile(d, 256)
    ub, gb_ = ucol0 // tn, gcol0 // tn
    return pl.pallas_call(
        _fftconv_kernel,
        out_shape=jax.ShapeDtypeStruct((nb, n2, d), BF16),
        grid=(d // tn, nb // n1h),
        in_specs=[pl.BlockSpec((n1h, n2, tn), lambda j, b: (b, 0, ub + j)),
                  pl.BlockSpec((n1h, n2, tn), lambda j, b: (b, 0, gb_ + j)),
                  pl.BlockSpec((1, 1, tn), lambda j, b: (order, 0, j)),
                  pl.BlockSpec((1, FFT_N1, 2, n2 // 2, tn), lambda j, b: (order, 0, 0, 0, j),
                               pipeline_mode=pl.Buffered(1)),
                  pl.BlockSpec((1, 1, tn), lambda j, b: (order, 0, j)),
                  _const_spec(ka.shape), _const_spec(fb.shape), _const_spec(gb.shape),
                  _const_spec(ka2.shape)],
        out_specs=pl.BlockSpec((n1h, n2, tn), lambda j, b: (b, 0, j)),
        scratch_shapes=[pltpu.VMEM((n1h, n2, tn), F32), pltpu.VMEM((FFT_N1, 2, n2, tn), F32)],
        compiler_params=_params("arbitrary", "arbitrary"),
        name="fft_long_conv",
    )(u, gsrc, bias, kf, knyq, ka, fb, gb, ka2)


def hyena_long_convs_fft(u3, tabs, spec, bias, seq_len, d):
    t = u3.shape[0]
    n2 = 2 * seq_len // FFT_N1
    u3b = u3.reshape(t // n2, n2, 3 * d)
    z = fft_long_conv(tabs, spec, u3b, 0, u3b, d, bias, 0, seq_len, d)
    z = fft_long_conv(tabs, spec, z, 0, u3b, 2 * d, bias, 1, seq_len, d)
    return z.reshape(t, d)


def rope_tables(seq_len):
    pos = jnp.arange(seq_len, dtype=jnp.int32)
    r = (pos // GRID_W).astype(F32)[:, None]
    col = (pos % GRID_W).astype(F32)[:, None]
    inv = ROPE_BASE ** (-(2.0 * jnp.arange(ROPE_PAIRS, dtype=F32)) / (2 * ROPE_PAIRS))
    ar, ac = r * inv, col * inv
    zeros = jnp.zeros_like(ar)
    cos = jnp.concatenate([jnp.cos(ar), jnp.cos(ar), jnp.cos(ac), jnp.cos(ac)], axis=1)
    sa = jnp.concatenate([-jnp.sin(ar), zeros, -jnp.sin(ac), zeros], axis=1)
    sb = jnp.concatenate([zeros, jnp.sin(ar), zeros, jnp.sin(ac)], axis=1)
    rep = LANES // HEAD_DIM
    return tuple(jnp.tile(a, (1, rep)) for a in (cos, sa, sb))


LOG2E = 1.4426950408889634
Q_SCALE = HEAD_DIM ** -0.5 * LOG2E


def _attend(q_ref, sink_ref, keys, vals, bias, o_ref, n_heads, q_per_kv, q_scale=None):
    nq = q_ref.shape[0]
    s_len = keys.shape[0]
    lane = lax.broadcasted_iota(jnp.int32, (s_len, HEAD_DIM), 1)
    ones_col = jnp.where(lane == 0, 1.0, 0.0).astype(BF16)
    hidx = lax.broadcasted_iota(jnp.int32, (q_per_kv, 1, 1), 0)
    pending = []
    for g in range(n_heads // q_per_kv):
        qs = []
        sink = jnp.zeros((q_per_kv, 1, 1), F32)
        for h in range(q_per_kv):
            col = (g * q_per_kv + h) * HEAD_DIM
            q = q_ref[:, col:col + HEAD_DIM]
            if q_scale is not None:
                q = (q.astype(F32) * q_scale).astype(BF16)
            qs.append(q)
            sink = jnp.where(hidx == h, sink_ref[g * q_per_kv + h] * LOG2E, sink)
        qg = jnp.concatenate(qs, axis=0)
        kg = keys[:, g * HEAD_DIM:(g + 1) * HEAD_DIM]
        vg = jnp.concatenate([vals[:, g * HEAD_DIM:(g + 1) * HEAD_DIM], ones_col], axis=1)
        s = lax.dot_general(qg, kg, (((1,), (1,)), ((), ())), preferred_element_type=F32)
        s = s.reshape(q_per_kv, nq, s_len)
        if bias is not None:
            s = s + bias[None]
        m = jnp.maximum(jnp.max(s, axis=-1, keepdims=True), sink)
        p = jnp.exp2(s - m).astype(BF16).reshape(q_per_kv * nq, s_len)
        acc = jnp.dot(p, vg, preferred_element_type=F32).reshape(q_per_kv, nq, 2 * HEAD_DIM)
        denom = acc[:, :, HEAD_DIM:HEAD_DIM + 1] + jnp.exp2(sink - m)
        out = acc[:, :, :HEAD_DIM] / denom
        for h in range(q_per_kv):
            pending.append(out[h])
            if len(pending) == LANES // HEAD_DIM:
                lo = (g * q_per_kv + h + 1) * HEAD_DIM - LANES
                o_ref[:, lo:lo + LANES] = jnp.concatenate(pending, axis=1).astype(o_ref.dtype)
                pending = []


def _attn_kernel(sink_ref, q_ref, k_ref, v_ref, kc_ref, vc_ref, o_ref, *, seq_len, n_heads, q_per_kv):
    span = Q_BLOCK + 2 * WINDOW
    ctx_len = kc_ref.shape[0]
    start = pl.program_id(1) * Q_BLOCK
    kstart = pl.multiple_of(jnp.clip(start - WINDOW, 0, seq_len - span), Q_BLOCK)
    qpos = start + lax.broadcasted_iota(jnp.int32, (Q_BLOCK, span + ctx_len), 0)
    col = lax.broadcasted_iota(jnp.int32, (Q_BLOCK, span + ctx_len), 1)
    visible = (jnp.abs(qpos - (kstart + col)) <= WINDOW) | (col >= span)
    bias = jnp.where(visible, 0.0, -1e30).astype(F32)
    keys = jnp.concatenate([k_ref[pl.ds(kstart, span), :], kc_ref[...]], axis=0)
    vals = jnp.concatenate([v_ref[pl.ds(kstart, span), :], vc_ref[...]], axis=0)
    _attend(q_ref, sink_ref, keys, vals, bias, o_ref, n_heads, q_per_kv)


def window_attention(qkv, qkv_ctx, sink, batch, seq_len, ctx_len, d_q, d_kv):
    n_heads = d_q // HEAD_DIM
    nqb = seq_len // Q_BLOCK
    kcol, vcol = d_q // d_kv, d_q // d_kv + 1
    return pl.pallas_call(
        functools.partial(_attn_kernel, seq_len=seq_len, n_heads=n_heads,
                          q_per_kv=n_heads // N_KV_HEADS),
        out_shape=jax.ShapeDtypeStruct((batch * seq_len, d_q), BF16),
        grid=(batch, nqb),
        in_specs=[pl.BlockSpec(memory_space=pltpu.SMEM),
                  pl.BlockSpec((Q_BLOCK, d_q), lambda b, i: (b * nqb + i, 0)),
                  pl.BlockSpec((seq_len, d_kv), lambda b, i: (b, kcol)),
                  pl.BlockSpec((seq_len, d_kv), lambda b, i: (b, vcol)),
                  pl.BlockSpec((ctx_len, d_kv), lambda b, i: (b, kcol)),
                  pl.BlockSpec((ctx_len, d_kv), lambda b, i: (b, vcol))],
        out_specs=pl.BlockSpec((Q_BLOCK, d_q), lambda b, i: (b * nqb + i, 0)),
        compiler_params=_params("arbitrary", "arbitrary"),
        name="window_attention",
    )(sink, qkv, qkv, qkv, qkv_ctx, qkv_ctx)


def _ctx_attn_kernel(sink_ref, q_ref, kc_ref, vc_ref, o_ref, *, n_heads, q_per_kv, scale):
    _attend(q_ref, sink_ref, kc_ref[...], vc_ref[...], None, o_ref, n_heads, q_per_kv,
            q_scale=scale)


def ctx_attention(qkv_ctx, sink, batch, ctx_len, d_q, d_kv):
    n_heads = d_q // HEAD_DIM
    kcol, vcol = d_q // d_kv, d_q // d_kv + 1
    return pl.pallas_call(
        functools.partial(_ctx_attn_kernel, n_heads=n_heads, q_per_kv=n_heads // N_KV_HEADS,
                          scale=Q_SCALE),
        out_shape=jax.ShapeDtypeStruct((batch * ctx_len, d_q), BF16),
        grid=(batch,),
        in_specs=[pl.BlockSpec(memory_space=pltpu.SMEM),
                  pl.BlockSpec((ctx_len, d_q), lambda b: (b, 0)),
                  pl.BlockSpec((ctx_len, d_kv), lambda b: (b, kcol)),
                  pl.BlockSpec((ctx_len, d_kv), lambda b: (b, vcol))],
        out_specs=pl.BlockSpec((ctx_len, d_q), lambda b: (b, 0)),
        compiler_params=_params("arbitrary"),
        name="ctx_attention",
    )(sink, qkv_ctx, qkv_ctx, qkv_ctx)


def _router_kernel(x_ref, sh_ref, sc_ref, wr_ref, br_ref, cnt0_ref, route_ref, cnt_ref, carry_ref):
    @pl.when(pl.program_id(0) == 0)
    def _():
        carry_ref[...] = cnt0_ref[...]

    h = _modulated_norm(x_ref[...], sh_ref[0], sc_ref[0])
    h_hi = h.astype(BF16)
    h_lo = (h - h_hi.astype(F32)).astype(BF16)
    logits = (jnp.dot(h_hi, wr_ref[0], preferred_element_type=F32)
              + (jnp.dot(h_hi, wr_ref[1], preferred_element_type=F32)
                 + jnp.dot(h_lo, wr_ref[0], preferred_element_type=F32))) + br_ref[...]
    tm = logits.shape[0]
    lane_i = lax.broadcasted_iota(jnp.int32, logits.shape, 1)
    lane = lane_i.astype(F32)
    neg = -jnp.inf
    lg = jnp.where(lane_i < N_GROUPS, logits, neg)
    mg = jnp.max(lg, axis=-1, keepdims=True)
    p_g = 1.0 / jnp.sum(jnp.exp(lg - mg), axis=-1, keepdims=True)
    g_sel = jnp.min(jnp.where(lg == mg, lane, float(ROUTE_W)), axis=-1, keepdims=True)
    lo = N_GROUPS + g_sel * EXPERTS_PER_GROUP
    lf = jnp.where((lane >= lo) & (lane < lo + EXPERTS_PER_GROUP), logits, neg)
    m1 = jnp.max(lf, axis=-1, keepdims=True)
    i1 = jnp.min(jnp.where(lf == m1, lane, float(ROUTE_W)), axis=-1, keepdims=True)
    lf2 = jnp.where(lane == i1, neg, lf)
    m2 = jnp.max(lf2, axis=-1, keepdims=True)
    i2 = jnp.min(jnp.where(lf2 == m2, lane, float(ROUTE_W)), axis=-1, keepdims=True)
    a2 = jnp.exp(m2 - m1)
    gate1 = p_g / (1.0 + a2)
    gate2 = gate1 * a2
    sel1 = lane == i1
    sel2 = lane == i2
    onehot = jnp.where(sel1 | sel2, 1.0, 0.0)
    r_i = lax.broadcasted_iota(jnp.int32, (tm, tm), 0)
    c_i = lax.broadcasted_iota(jnp.int32, (tm, tm), 1)
    tri = jnp.where(r_i > c_i, 1.0, 0.0).astype(BF16)
    before = jnp.dot(tri, onehot.astype(BF16), preferred_element_type=F32) + carry_ref[...]
    rank1 = jnp.sum(jnp.where(sel1, before, 0.0), axis=-1, keepdims=True)
    rank2 = jnp.sum(jnp.where(sel2, before, 0.0), axis=-1, keepdims=True)
    carry_ref[...] += jnp.sum(onehot, axis=0, keepdims=True)
    cnt_ref[...] = carry_ref[...]
    vals = (i1 - N_GROUPS, i2 - N_GROUPS, rank1, rank2, gate1, gate2)
    route = jnp.zeros_like(logits)
    for idx, v in enumerate(vals):
        route = jnp.where(lane_i == idx, v, route)
    route_ref[...] = route


def router(x, sh, sc, wr, br, cnt0, rows_per_seg):
    t, d = x.shape
    tm = _tile(rows_per_seg, 512)
    seg = lambda i: ((i * tm) // rows_per_seg, 0, 0)
    wr_hi = wr.astype(BF16)
    wr = jnp.stack([wr_hi, (wr - wr_hi.astype(F32)).astype(BF16)])
    return pl.pallas_call(
        _router_kernel,
        out_shape=(jax.ShapeDtypeStruct((t, ROUTE_W), F32), jax.ShapeDtypeStruct((1, ROUTE_W), F32)),
        grid=(t // tm,),
        in_specs=[pl.BlockSpec((tm, d), lambda i: (i, 0)),
                  pl.BlockSpec((1, 1, d), seg), pl.BlockSpec((1, 1, d), seg),
                  pl.BlockSpec((2, d, ROUTE_W), lambda i: (0, 0, 0)),
                  pl.BlockSpec((1, ROUTE_W), lambda i: (0, 0)),
                  pl.BlockSpec((1, ROUTE_W), lambda i: (0, 0))],
        out_specs=(pl.BlockSpec((tm, ROUTE_W), lambda i: (i, 0)),
                   pl.BlockSpec((1, ROUTE_W), lambda i: (0, 0))),
        scratch_shapes=[pltpu.VMEM((1, ROUTE_W), F32)],
        compiler_params=_params("arbitrary"),
        name="router",
    )(x, sh, sc, wr, br, cnt0)


def _token_copy(src, src_tok, dst, dst_tok, sem, rows):
    return pltpu.make_async_copy(src.at[pl.ds(src_tok * rows, rows)],
                                 dst.at[pl.ds(dst_tok * rows, rows)], sem)


TOKENS_PER_TRIP = 2


def _start_copy(copy, k):
    copy.start(priority=k)


def _wait_copy(copy, k):
    del k
    copy.wait()


def _to_token_rows(mat, rows_ref):
    tm, d = mat.shape
    rows = d // LANES
    for s in range(rows):
        rows_ref[pl.ds(s, tm, stride=rows), :] = mat[:, s * LANES:(s + 1) * LANES]


def _token_cols(rows_ref, s, tm, rows):
    return rows_ref[pl.ds(s, tm, stride=rows), :]


SLOT_BITS = 20


def _slot(code_ref, pad_start_ref, idx):
    code = code_ref[idx]
    return pad_start_ref[code >> SLOT_BITS] + (code & ((1 << SLOT_BITS) - 1))


def _dispatch_kernel(code_ref, pad_start_ref, pad_end_ref, nused_ref, x_ref, sh_ref, sc_ref, *rest,
                     first, n_blocks):
    if first:
        xs_ref, h_ref, zero_ref, sem, zsem = rest
    else:
        _, xs_ref, h_ref, zero_ref, sem, zsem = rest
    tm, d = x_ref.shape
    rows = d // LANES
    base = pl.program_id(0) * tm

    if first:
        @pl.when(pl.program_id(0) == 0)
        def _():
            zero_ref[...] = jnp.zeros_like(zero_ref)
            blk_rows = MOE_BLOCK * rows

            def zero_block(blk):
                return pltpu.make_async_copy(zero_ref, xs_ref.at[pl.ds(blk * blk_rows, blk_rows)], zsem)

            def last_blocks(op):
                def body(e, carry):
                    hi = pad_end_ref[e]
                    lo = pad_start_ref[e]

                    @pl.when(hi > lo)
                    def _():
                        op(zero_block(hi // MOE_BLOCK - 1))
                    return carry
                lax.fori_loop(0, N_EXPERTS, body, 0)

            def tail_blocks(op):
                def body(blk, carry):
                    op(zero_block(blk))
                    return carry
                lax.fori_loop(nused_ref[0], n_blocks, body, 0)

            last_blocks(lambda c: c.start())
            tail_blocks(lambda c: c.start())
            last_blocks(lambda c: c.wait())
            tail_blocks(lambda c: c.wait())

    step = pl.program_id(0)
    slot = step % 2
    _to_token_rows(_modulated_norm(x_ref[...], sh_ref[0], sc_ref[0]), h_ref.at[slot])

    def copies(which, row0, op):
        def body(i, carry):
            for j in range(TOKENS_PER_TRIP):
                r = i * TOKENS_PER_TRIP + j
                for k in range(2):
                    op(_token_copy(h_ref.at[which], r, xs_ref,
                                   _slot(code_ref, pad_start_ref, 2 * (row0 + r) + k), sem.at[which],
                                   rows), k)
            return carry
        lax.fori_loop(0, tm // TOKENS_PER_TRIP, body, 0)

    copies(slot, base, _start_copy)

    @pl.when(step > 0)
    def _():
        copies(1 - slot, base - tm, _wait_copy)

    @pl.when(step == pl.num_programs(0) - 1)
    def _():
        copies(slot, base, _wait_copy)


def dispatch(code, pad_start, pad_end, nused, n_blocks, x, sh, sc, xs, rows_per_seg):
    t, d = x.shape
    rows = d // LANES
    tm = _tile(rows_per_seg, 256)
    first = xs is None
    seg = lambda i, *_: ((i * tm) // rows_per_seg, 0, 0)
    in_specs = [pl.BlockSpec((tm, d), lambda i, *_: (i, 0)),
                pl.BlockSpec((1, 1, d), seg), pl.BlockSpec((1, 1, d), seg)]
    args = [code, pad_start, pad_end, nused, x, sh, sc]
    if not first:
        in_specs.append(pl.BlockSpec(memory_space=pl.ANY))
        args.append(xs)
    return pl.pallas_call(
        functools.partial(_dispatch_kernel, first=first, n_blocks=n_blocks),
        out_shape=jax.ShapeDtypeStruct((n_blocks * MOE_BLOCK * rows, LANES), F32),
        grid_spec=pltpu.PrefetchScalarGridSpec(
            num_scalar_prefetch=4,
            grid=(t // tm,),
            in_specs=in_specs,
            out_specs=pl.BlockSpec(memory_space=pl.ANY),
            scratch_shapes=[pltpu.VMEM((2, tm * rows, LANES), F32),
                            pltpu.VMEM((MOE_BLOCK * rows, LANES), F32),
                            pltpu.SemaphoreType.DMA((2,)), pltpu.SemaphoreType.DMA]),
        input_output_aliases={} if first else {7: 0},
        compiler_params=_params("arbitrary"),
        name="moe_dispatch",
    )(*args)


def _expert_kernel(bexp_ref, nused_ref, xs_ref, w1_ref, w3_ref, w2_ref, ys_ref, xb, w13b, w2b):
    i = pl.program_id(0)
    e = bexp_ref[i]
    prev = bexp_ref[jnp.maximum(i - 1, 0)]
    tm, d = xb.shape
    rows = d // LANES
    de = w2b.shape[0]

    @pl.when((i == 0) | (e != prev))
    def _():
        w13b[:, :de] = w1_ref[0, 0].astype(BF16)
        w13b[:, de:] = w3_ref[0, 0].astype(BF16)
        w2b[...] = w2_ref[0, 0].astype(BF16)

    @pl.when(i < nused_ref[0])
    def _():
        for s in range(rows):
            xb[:, s * LANES:(s + 1) * LANES] = _token_cols(xs_ref, s, tm, rows).astype(BF16)
        h13 = jnp.dot(xb[...], w13b[...], preferred_element_type=F32)
        h1, h3 = h13[:, :de], h13[:, de:]
        a = (h1 * jax.nn.sigmoid(h1) * h3).astype(BF16)
        _to_token_rows(jnp.dot(a, w2b[...], preferred_element_type=F32), ys_ref)

    @pl.when(i >= nused_ref[0])
    def _():
        ys_ref[...] = jnp.zeros_like(ys_ref)


def experts(block_exp, nused, xs, w1, w3, w2, layer):
    d, de = w1.shape[-2:]
    blk_rows = MOE_BLOCK * (d // LANES)
    blk = lambda i, bexp, nu: (jnp.minimum(i, nu[0] - 1), 0)
    wspec = lambda shape: pl.BlockSpec(shape, lambda i, bexp, nu: (layer, bexp[i], 0, 0))
    return pl.pallas_call(
        _expert_kernel,
        out_shape=jax.ShapeDtypeStruct(xs.shape, F32),
        grid_spec=pltpu.PrefetchScalarGridSpec(
            num_scalar_prefetch=2,
            grid=(xs.shape[0] // blk_rows,),
            in_specs=[pl.BlockSpec((blk_rows, LANES), blk),
                      wspec((1, 1, d, de)), wspec((1, 1, d, de)), wspec((1, 1, de, d))],
            out_specs=pl.BlockSpec((blk_rows, LANES), lambda i, bexp, nu: (i, 0)),
            scratch_shapes=[pltpu.VMEM((MOE_BLOCK, d), BF16), pltpu.VMEM((d, 2 * de), BF16),
                            pltpu.VMEM((de, d), BF16)]),
        compiler_params=_params("arbitrary"),
        name="moe_experts",
    )(block_exp, nused, xs, w1, w3, w2)


def _combine_kernel(code_ref, pad_start_ref, x_ref, route_ref, g_ref, gain_ref, ys_ref, o_ref, y0_ref,
                    y1_ref, sem, *, final):
    tm, d = x_ref.shape
    rows = d // LANES
    base = pl.program_id(0) * tm

    step = pl.program_id(0)
    slot = step % 2

    def copies(which, row0, op):
        def body(i, carry):
            for j in range(TOKENS_PER_TRIP):
                r = i * TOKENS_PER_TRIP + j
                for k, y_ref in enumerate((y0_ref, y1_ref)):
                    op(_token_copy(ys_ref, _slot(code_ref, pad_start_ref, 2 * (row0 + r) + k),
                                   y_ref.at[which], r, sem.at[which], rows), k)
            return carry
        lax.fori_loop(0, tm // TOKENS_PER_TRIP, body, 0)

    @pl.when(step == 0)
    def _():
        copies(slot, base, _start_copy)

    @pl.when(step + 1 < pl.num_programs(0))
    def _():
        copies(1 - slot, base + tm, _start_copy)

    copies(slot, base, _wait_copy)
    y0_rows, y1_rows = y0_ref.at[slot], y1_ref.at[slot]
    route = route_ref[...]
    g0, g1 = route[:, 4:5], route[:, 5:6]
    ssq = jnp.zeros((tm, 1), F32)
    for s in range(rows):
        cols = slice(s * LANES, (s + 1) * LANES)
        y = g0 * _token_cols(y0_rows, s, tm, rows) + g1 * _token_cols(y1_rows, s, tm, rows)
        out = x_ref[:, cols] + g_ref[0][:, cols] * y
        o_ref[:, cols] = out
        ssq = ssq + jnp.sum(out * out, axis=-1, keepdims=True)
    if final:
        o_ref[...] = o_ref[...] * lax.rsqrt(ssq * (1.0 / d) + EPS) * gain_ref[...]


def combine(code, pad_start, x, route, gate, gain, ys, rows_per_seg, final):
    t, d = x.shape
    rows = d // LANES
    tm = _tile(rows_per_seg, 256)
    return pl.pallas_call(
        functools.partial(_combine_kernel, final=final),
        out_shape=jax.ShapeDtypeStruct((t, d), F32),
        grid_spec=pltpu.PrefetchScalarGridSpec(
            num_scalar_prefetch=2,
            grid=(t // tm,),
            in_specs=[pl.BlockSpec((tm, d), lambda i, *_: (i, 0)),
                      pl.BlockSpec((tm, ROUTE_W), lambda i, *_: (i, 0)),
                      pl.BlockSpec((1, 1, d), lambda i, *_: ((i * tm) // rows_per_seg, 0, 0)),
                      pl.BlockSpec((1, d), lambda i, *_: (0, 0)),
                      pl.BlockSpec(memory_space=pl.ANY)],
            out_specs=pl.BlockSpec((tm, d), lambda i, *_: (i, 0)),
            scratch_shapes=[pltpu.VMEM((2, tm * rows, LANES), F32),
                            pltpu.VMEM((2, tm * rows, LANES), F32),
                            pltpu.SemaphoreType.DMA((2,))]),
        compiler_params=_params("arbitrary"),
        name="moe_combine",
    )(code, pad_start, x, route, gate, gain, ys)


def _slot_plan(route_list, counts):
    cnt = counts[0, N_GROUPS:N_GROUPS + N_EXPERTS].astype(jnp.int32)
    padded = (cnt + MOE_BLOCK - 1) // MOE_BLOCK * MOE_BLOCK
    pad_end = jnp.cumsum(padded)
    pad_start = pad_end - padded
    n_assign = 2 * sum(r.shape[0] for r in route_list)
    n_blocks = -(-n_assign // MOE_BLOCK) + N_EXPERTS
    block_start = jnp.arange(n_blocks, dtype=jnp.int32) * MOE_BLOCK
    block_exp = jnp.minimum(jnp.sum(block_start[:, None] >= pad_end[None, :], axis=1),
                            N_EXPERTS - 1).astype(jnp.int32)
    nused = (pad_end[-1:] // MOE_BLOCK).astype(jnp.int32)
    assert n_assign < (1 << SLOT_BITS)
    codes = [((r[:, 0:2].astype(jnp.int32) << SLOT_BITS) | r[:, 2:4].astype(jnp.int32)).reshape(-1)
             for r in route_list]
    return codes, pad_start.astype(jnp.int32), pad_end.astype(jnp.int32), block_exp, nused, n_blocks


def hier_moe_layer(streams, layer, wr, br, w1, w3, w2, gain, final):
    counts = jnp.zeros((1, ROUTE_W), F32)
    routes = []
    for x, sh, sc, _, rps in streams:
        route, counts = router(x, sh, sc, wr, br, counts, rps)
        routes.append(route)
    codes, pad_start, pad_end, block_exp, nused, n_blocks = _slot_plan(routes, counts)
    xs = None
    for (x, sh, sc, _, rps), code in zip(streams, codes):
        xs = dispatch(code, pad_start, pad_end, nused, n_blocks, x, sh, sc, xs, rps)
    ys = experts(block_exp, nused, xs, w1, w3, w2, layer)
    return [combine(code, pad_start, x, route, g, gain, ys, rps, final)
            for (x, _, _, g, rps), code, route in zip(streams, codes, routes)]


def kernel(x, c, ctx, c_ctx, mod_w, mod_b, hy_w_in, hy_conv_w, hy_conv_b, hy_f_w1, hy_f_b1,
           hy_f_freq1, hy_f_w2, hy_f_b2, hy_f_freq2, hy_f_w3, hy_bias, hy_w_out,
           at_w_qkv, at_sink, at_w_o, moe_wg, moe_bg, moe_we, moe_be, moe_w1, moe_w3, moe_w2,
           final_gain):
    batch, seq_len, d = x.shape
    ctx_len = ctx.shape[1]
    depth = mod_w.shape[0]
    d_q = at_w_qkv.shape[2] - 2 * N_KV_HEADS * HEAD_DIM
    d_kv = N_KV_HEADS * HEAD_DIM
    n_ctx = batch * ctx_len

    xl = x.reshape(batch * seq_len, d)
    xc = ctx.reshape(n_ctx, d)

    n_cond = batch + 1
    cond = jnp.concatenate([c, c_ctx[None, :], jnp.zeros((-n_cond % 8, d), F32)], axis=0)
    mods = adaln(cond, mod_w, mod_b)

    def mod_slices(i):
        parts = [mods[i, :, j * d:(j + 1) * d] for j in range(6)]
        lat = [p[:batch].reshape(batch, 1, d) for p in parts]
        cx = [p[batch:batch + 1].reshape(1, 1, d) for p in parts]
        return lat, cx

    def use_fft(n):
        return n >= 512 and (2 * n) % (FFT_N1 * 2 * SUB) == 0

    tables = {n: (fft_tables(n) if use_fft(n) else dft_tables(n)) for n in {seq_len, ctx_len}}
    rtabs = rope_tables(seq_len)
    gain = final_gain.reshape(1, d)

    for i in range(depth):
        last = i == depth - 1
        j = i // 2
        (sh1, sc1, g1, sh2, sc2, g2), (sh1c, sc1c, g1c, sh2c, sc2c, g2c) = mod_slices(i)
        if i % 2 == 0:
            w_in = hy_w_in[j].astype(BF16)
            w_out = hy_w_out[j].astype(BF16)
            filt = (hy_f_w1[j], hy_f_b1[j], hy_f_freq1[j], hy_f_w2[j], hy_f_b2[j], hy_f_freq2[j],
                    hy_f_w3[j])
            bias = hy_bias[j].reshape(2, 1, d)

            def mixer(xs_, sh, sc, g, n_seq, rps):
                u = modmm(xs_, sh, sc, w_in, rps)
                u3 = short_conv(u, hy_conv_w[j], hy_conv_b[j], n_seq)
                if use_fft(n_seq):
                    taps = hyena_filter_taps(n_seq, *filt, d, even_odd=False)
                    spec = fft_filter_spectrum(tables[n_seq], *taps)
                    z = hyena_long_convs_fft(u3, tables[n_seq], spec, bias, n_seq, d)
                else:
                    taps = hyena_filter_taps(n_seq, *filt, d, even_odd=True)
                    spec = filter_spectrum(*tables[n_seq], *taps)
                    z = hyena_long_convs(u3, tables[n_seq], spec, bias, n_seq, d)
                return mm_res(z, w_out, xs_, g, rps)

            xl = mixer(xl, sh1, sc1, g1, seq_len, seq_len)
            if not last:
                xc = mixer(xc, sh1c, sc1c, g1c, ctx_len, n_ctx)
        else:
            w_qkv = at_w_qkv[j].astype(BF16)
            w_o = at_w_o[j].astype(BF16)
            sink = at_sink[j]
            qkv = modmm(xl, sh1, sc1, w_qkv, seq_len, rope=(d_q, d_q + d_kv), rope_tabs=rtabs)
            qkv_c = modmm(xc, sh1c, sc1c, w_qkv, n_ctx)
            o = window_attention(qkv, qkv_c, sink, batch, seq_len, ctx_len, d_q, d_kv)
            xl = mm_res(o, w_o, xl, g1, seq_len)
            if not last:
                o_c = ctx_attention(qkv_c, sink, batch, ctx_len, d_q, d_kv)
                xc = mm_res(o_c, w_o, xc, g1c, n_ctx)

        wr = jnp.concatenate([moe_wg[i], moe_we[i],
                              jnp.zeros((d, ROUTE_W - N_GROUPS - N_EXPERTS), F32)], axis=1)
        br = jnp.concatenate([moe_bg[i], moe_be[i],
                              jnp.zeros((ROUTE_W - N_GROUPS - N_EXPERTS,), F32)]).reshape(1, ROUTE_W)
        streams = [(xl, sh2, sc2, g2, seq_len)]
        if not last:
            streams.append((xc, sh2c, sc2c, g2c, n_ctx))
        outs = hier_moe_layer(streams, i, wr, br, moe_w1, moe_w3, moe_w2, gain, last)
        xl = outs[0]
        if not last:
            xc = outs[1]

    return xl.reshape(batch, seq_len, d)
```

```python
import functools
import math

import jax
import jax.numpy as jnp
from jax import lax
from jax.experimental import pallas as pl
from jax.experimental.pallas import tpu as pltpu

F32 = jnp.float32
BF16 = jnp.bfloat16
HIGHEST = lax.Precision.HIGHEST

EPS = 1e-6
GRID_W = 64
HEAD_DIM = 64
N_KV_HEADS = 4
WINDOW = 128
Q_BLOCK = 128
ROPE_PAIRS = HEAD_DIM // 4
ROPE_BASE = 10000.0
N_GROUPS = 8
EXPERTS_PER_GROUP = 8
N_EXPERTS = N_GROUPS * EXPERTS_PER_GROUP
MOE_BLOCK = 256
HY_EMB = 33
HY_BANDS = (HY_EMB - 1) // 2
HY_FAST_DECAY_PCT = 0.3
HY_SLOW_DECAY_PCT = 1.5
HY_DECAY_TARGET = 1e-2

LANES = 128
ROUTE_W = LANES
VMEM_LIMIT = 56 * 1024 * 1024


def _tile(n, pref):
    if n <= pref:
        return n
    t = pref
    while n % t:
        t -= 1
    return t


def _params(*sem):
    return pltpu.CompilerParams(dimension_semantics=sem, vmem_limit_bytes=VMEM_LIMIT)


def _modulated_norm(x, sh, sc):
    y = x * lax.rsqrt(jnp.mean(x * x, axis=-1, keepdims=True) + EPS)
    return y * (1.0 + sc) + sh


def _adaln_kernel(c_ref, w_ref, b_ref, o_ref):
    c = c_ref[...]
    a = c * jax.nn.sigmoid(c)
    o_ref[0] = jnp.dot(a, w_ref[0], precision=HIGHEST, preferred_element_type=F32) + b_ref[0]


def adaln(cond, mod_w, mod_b):
    depth, d, n = mod_w.shape
    r = cond.shape[0]
    tn = _tile(n, 1024)
    return pl.pallas_call(
        _adaln_kernel,
        out_shape=jax.ShapeDtypeStruct((depth, r, n), F32),
        grid=(depth, n // tn),
        in_specs=[pl.BlockSpec((r, d), lambda l, j: (0, 0)),
                  pl.BlockSpec((1, d, tn), lambda l, j: (l, 0, j)),
                  pl.BlockSpec((1, 1, tn), lambda l, j: (l, 0, j))],
        out_specs=pl.BlockSpec((1, r, tn), lambda l, j: (l, 0, j)),
        compiler_params=_params("arbitrary", "arbitrary"),
        name="adaln",
    )(cond, mod_w, mod_b.reshape(depth, 1, n))


def _modmm_kernel(x_ref, sh_ref, sc_ref, w_ref, *rest, rope):
    if rope is None:
        o_ref, h_ref = rest
    else:
        cos_ref, sa_ref, sb_ref, o_ref, h_ref = rest

    @pl.when(pl.program_id(1) == 0)
    def _():
        h_ref[...] = _modulated_norm(x_ref[...], sh_ref[0], sc_ref[0]).astype(BF16)

    acc = jnp.dot(h_ref[...], w_ref[...], preferred_element_type=F32)
    if rope is None:
        o_ref[...] = acc.astype(o_ref.dtype)
        return
    d_q, d_qk = rope
    tn = acc.shape[1]
    for c in range(tn // LANES):
        col0 = pl.program_id(1) * tn + c * LANES
        x = acc[:, c * LANES:(c + 1) * LANES]

        @pl.when(col0 < d_qk)
        def _(x=x, c=c, col0=col0):
            y = (x * cos_ref[...] + pltpu.roll(x, LANES - ROPE_PAIRS, 1) * sa_ref[...]
                 + pltpu.roll(x, ROPE_PAIRS, 1) * sb_ref[...])
            y = y * jnp.where(col0 < d_q, Q_SCALE, 1.0).astype(F32)
            o_ref[:, c * LANES:(c + 1) * LANES] = y.astype(o_ref.dtype)

        @pl.when(col0 >= d_qk)
        def _(x=x, c=c):
            o_ref[:, c * LANES:(c + 1) * LANES] = x.astype(o_ref.dtype)


def modmm(x, sh, sc, w, rows_per_seg, rope=None, rope_tabs=None):
    t, d = x.shape
    n = w.shape[1]
    tm = _tile(rows_per_seg, 1024)
    tn = _tile(n, 1024)
    seg = lambda i, j: ((i * tm) // rows_per_seg, 0, 0)
    in_specs = [pl.BlockSpec((tm, d), lambda i, j: (i, 0)),
                pl.BlockSpec((1, 1, d), seg),
                pl.BlockSpec((1, 1, d), seg),
                pl.BlockSpec((d, tn), lambda i, j: (0, j))]
    args = [x, sh, sc, w]
    if rope is not None:
        nb = rows_per_seg // tm
        in_specs += [pl.BlockSpec((tm, LANES), lambda i, j: (i % nb, 0))] * 3
        args += list(rope_tabs)
    return pl.pallas_call(
        functools.partial(_modmm_kernel, rope=rope),
        out_shape=jax.ShapeDtypeStruct((t, n), BF16),
        grid=(t // tm, n // tn),
        in_specs=in_specs,
        out_specs=pl.BlockSpec((tm, tn), lambda i, j: (i, j)),
        scratch_shapes=[pltpu.VMEM((tm, d), BF16)],
        compiler_params=_params("arbitrary", "arbitrary"),
        name="modmm",
    )(*args)


HALO = 16


def _convmm_kernel(x_ref, xp_ref, xn_ref, sh_ref, sc_ref, w_ref, cw_ref, cb_ref, o_ref, h_ref, *,
                   seq_len):
    tm = x_ref.shape[0]

    @pl.when(pl.program_id(1) == 0)
    def _():
        sh, sc = sh_ref[0], sc_ref[0]
        h_ref[0:tm] = _modulated_norm(x_ref[...], sh, sc).astype(BF16)
        h_ref[tm:tm + HALO] = _modulated_norm(xp_ref[...], sh, sc).astype(BF16)
        h_ref[tm + HALO:tm + 2 * HALO] = _modulated_norm(xn_ref[...], sh, sc).astype(BF16)

    acc = jnp.dot(h_ref[...], w_ref[...], preferred_element_type=F32)
    u = acc[:tm]
    row = lax.broadcasted_iota(jnp.int32, u.shape, 0)
    pos = (pl.program_id(0) * tm + row) % seq_len
    prev = jnp.where(row == 0, acc[tm + HALO - 1:tm + HALO], pltpu.roll(u, 1, 0))
    prev = jnp.where(pos == 0, 0.0, prev)
    nxt = jnp.where(row == tm - 1, acc[tm + HALO:tm + HALO + 1], pltpu.roll(u, tm - 1, 0))
    nxt = jnp.where(pos == seq_len - 1, 0.0, nxt)
    y = prev * cw_ref[0:1, :] + u * cw_ref[1:2, :] + nxt * cw_ref[2:3, :] + cb_ref[...]
    o_ref[...] = y.astype(o_ref.dtype)


def modmm_conv(x, sh, sc, w, conv_w, conv_b, rows_per_seg, seq_len):
    t, d = x.shape
    n = w.shape[1]
    tm = _tile(rows_per_seg, 1024)
    tn = _tile(n, 1024)
    per = tm // HALO
    seg = lambda i, j: ((i * tm) // rows_per_seg, 0, 0)
    return pl.pallas_call(
        functools.partial(_convmm_kernel, seq_len=seq_len),
        out_shape=jax.ShapeDtypeStruct((t, n), BF16),
        grid=(t // tm, n // tn),
        in_specs=[pl.BlockSpec((tm, d), lambda i, j: (i, 0)),
                  pl.BlockSpec((HALO, d), lambda i, j: (jnp.maximum(i * per - 1, 0), 0)),
                  pl.BlockSpec((HALO, d), lambda i, j: (jnp.minimum((i + 1) * per, t // HALO - 1), 0)),
                  pl.BlockSpec((1, 1, d), seg),
                  pl.BlockSpec((1, 1, d), seg),
                  pl.BlockSpec((d, tn), lambda i, j: (0, j)),
                  pl.BlockSpec((3, tn), lambda i, j: (0, j)),
                  pl.BlockSpec((1, tn), lambda i, j: (0, j))],
        out_specs=pl.BlockSpec((tm, tn), lambda i, j: (i, j)),
        scratch_shapes=[pltpu.VMEM((tm + 2 * HALO, d), BF16)],
        compiler_params=_params("arbitrary", "arbitrary"),
        name="modmm_conv",
    )(x, x, x, sh, sc, w, conv_w, conv_b.reshape(1, n))


def _mmres_kernel(a_ref, w_ref, r_ref, g_ref, o_ref):
    acc = jnp.dot(a_ref[...], w_ref[...], preferred_element_type=F32)
    o_ref[...] = r_ref[...] + g_ref[0] * acc


def mm_res(a, w, res, gate, rows_per_seg):
    t, k = a.shape
    n = w.shape[1]
    tm = _tile(rows_per_seg, 1024)
    tn = _tile(n, 1024)
    return pl.pallas_call(
        _mmres_kernel,
        out_shape=jax.ShapeDtypeStruct((t, n), F32),
        grid=(t // tm, n // tn),
        in_specs=[pl.BlockSpec((tm, k), lambda i, j: (i, 0)),
                  pl.BlockSpec((k, tn), lambda i, j: (0, j)),
                  pl.BlockSpec((tm, tn), lambda i, j: (i, j)),
                  pl.BlockSpec((1, 1, tn), lambda i, j: ((i * tm) // rows_per_seg, 0, j))],
        out_specs=pl.BlockSpec((tm, tn), lambda i, j: (i, j)),
        compiler_params=_params("arbitrary", "arbitrary"),
        name="mm_res",
    )(a, w, res, gate)


def _filter_mlp_kernel(feat_ref, w1_ref, b1_ref, f1_ref, w2_ref, b2_ref, f2_ref, h_ref):
    h = jnp.sin(f1_ref[...] * (jnp.dot(feat_ref[...], w1_ref[...], precision=HIGHEST,
                                       preferred_element_type=F32) + b1_ref[...]))
    h_ref[...] = jnp.sin(f2_ref[...] * (jnp.dot(h, w2_ref[...], precision=HIGHEST,
                                                preferred_element_type=F32) + b2_ref[...]))


def _filter_kernel(hf_ref, hb_ref, tf_ref, tb_ref, w3f_ref, w3b_ref, dl_ref, p_ref, q_ref, *,
                   even_odd):
    dl = jnp.abs(dl_ref[...])
    hf = jnp.dot(hf_ref[...], w3f_ref[...], precision=HIGHEST,
                 preferred_element_type=F32) * jnp.exp(-tf_ref[...] * dl)
    hb = jnp.dot(hb_ref[...], w3b_ref[...], precision=HIGHEST,
                 preferred_element_type=F32) * jnp.exp(-tb_ref[...] * dl)
    row = lax.broadcasted_iota(jnp.int32, hf.shape, 0)
    hb = jnp.where(row == 0, 0.0, hb)
    s = lax.rsqrt(jnp.sum(hf * hf + hb * hb, axis=0, keepdims=True) + EPS)
    p, q = ((hf + hb) * s, (hf - hb) * s) if even_odd else (hf * s, hb * s)
    p_ref[0] = p.astype(p_ref.dtype).reshape(p_ref.shape[1:])
    q_ref[0] = q.astype(q_ref.dtype).reshape(q_ref.shape[1:])


def hyena_filter_taps(seq_len, w1, b1, f1, w2, b2, f2, w3, d, even_odd):
    t = jnp.linspace(0.0, 1.0, seq_len, dtype=F32)[:, None]
    omega = (2.0 * math.pi / seq_len) * jnp.arange(seq_len, dtype=F32)[:, None]
    bands = jnp.linspace(1e-4, HY_BANDS - 1, HY_BANDS, dtype=F32)[None, :]
    feats = jnp.concatenate([t, jnp.cos(bands * omega), -jnp.sin(bands * omega),
                             jnp.zeros((seq_len, LANES - HY_EMB), F32)], axis=-1)
    if not even_odd:
        feats = jnp.concatenate([feats, feats[:1], feats[:0:-1]], axis=0)
    w1p = jnp.concatenate([w1, jnp.zeros((LANES - HY_EMB, w1.shape[1]), F32)], axis=0)
    fw = w1.shape[1]
    rows = feats.shape[0]
    vec = lambda a: a.reshape(1, fw)
    h = pl.pallas_call(
        _filter_mlp_kernel,
        out_shape=jax.ShapeDtypeStruct((rows, fw), F32),
        compiler_params=_params(),
        name="hyena_filter_mlp",
    )(feats, w1p, vec(b1), vec(f1), w2, vec(b2), vec(f2))
    tcol = feats[:, 0:1]
    max_decay = math.log(HY_DECAY_TARGET) / HY_FAST_DECAY_PCT
    min_decay = math.log(HY_DECAY_TARGET) / HY_SLOW_DECAY_PCT
    deltas = jnp.linspace(min_decay, max_decay, d, dtype=F32).reshape(1, d)
    tn = _tile(d, 256)
    nb = d // tn
    bsel = 0 if even_odd else 1
    if even_odd:
        oshape, oblock = (2, seq_len, d), (1, seq_len, tn)
        omap = lambda o, j: (o, 0, j)
    else:
        n2 = 2 * seq_len // FFT_N1
        oshape, oblock = (2, seq_len // n2, n2, d), (1, seq_len // n2, n2, tn)
        omap = lambda o, j: (o, 0, 0, j)
    out = jax.ShapeDtypeStruct(oshape, BF16)
    return pl.pallas_call(
        functools.partial(_filter_kernel, even_odd=even_odd),
        out_shape=(out, out),
        grid=(2, nb),
        in_specs=[pl.BlockSpec((seq_len, fw), lambda o, j: (0, 0)),
                  pl.BlockSpec((seq_len, fw), lambda o, j: (bsel, 0)),
                  pl.BlockSpec((seq_len, 1), lambda o, j: (0, 0)),
                  pl.BlockSpec((seq_len, 1), lambda o, j: (bsel, 0)),
                  pl.BlockSpec((fw, tn), lambda o, j: (0, (2 * o) * nb + j)),
                  pl.BlockSpec((fw, tn), lambda o, j: (0, (2 * o + 1) * nb + j)),
                  pl.BlockSpec((1, tn), lambda o, j: (0, j))],
        out_specs=(pl.BlockSpec(oblock, omap), pl.BlockSpec(oblock, omap)),
        compiler_params=_params("arbitrary", "arbitrary"),
        name="hyena_filter",
    )(h, h, tcol, tcol, w3, w3, deltas)


def _alt_sum(x):
    row = lax.broadcasted_iota(jnp.int32, x.shape, 0)
    return jnp.sum(jnp.where((row & 1) == 1, -x, x), axis=0, keepdims=True)


def _spectrum_kernel(c_ref, s_ref, e_ref, o_ref, kr_ref, km_ref):
    kr_ref[0] = jnp.dot(c_ref[...], e_ref[0], preferred_element_type=F32)
    km = jnp.dot(s_ref[...], o_ref[0], preferred_element_type=F32)

    @pl.when(pl.program_id(2) != 0)
    def _():
        km_ref[0] = km

    @pl.when(pl.program_id(2) == 0)
    def _():
        row = lax.broadcasted_iota(jnp.int32, km.shape, 0)
        km_ref[0] = jnp.where(row == 0, _alt_sum(e_ref[0].astype(F32)), km)


def filter_spectrum(cmat, smat, e_taps, o_taps):
    _, n, d = e_taps.shape
    tk = _tile(n, 512)
    tn = _tile(d, 512)
    out = jax.ShapeDtypeStruct((2, n, d), F32)
    return pl.pallas_call(
        _spectrum_kernel,
        out_shape=(out, out),
        grid=(2, d // tn, n // tk),
        in_specs=[pl.BlockSpec((tk, n), lambda o, j, k: (k, 0)),
                  pl.BlockSpec((tk, n), lambda o, j, k: (k, 0)),
                  pl.BlockSpec((1, n, tn), lambda o, j, k: (o, 0, j)),
                  pl.BlockSpec((1, n, tn), lambda o, j, k: (o, 0, j))],
        out_specs=(pl.BlockSpec((1, tk, tn), lambda o, j, k: (o, k, j)),
                   pl.BlockSpec((1, tk, tn), lambda o, j, k: (o, k, j))),
        compiler_params=_params("arbitrary", "arbitrary", "arbitrary"),
        name="filter_spectrum",
    )(cmat, smat, e_taps, o_taps)


def _dft_fwd_kernel(c_ref, s_ref, u_ref, kr_ref, km_ref, yr_ref, q_ref):
    u = u_ref[...]
    a = jnp.dot(c_ref[...], u, preferred_element_type=F32)
    b = jnp.dot(s_ref[...], u, preferred_element_type=F32)
    kr = kr_ref[0]
    km = km_ref[0]
    yr = a * kr - b * km
    q = b * kr + a * km

    @pl.when(pl.program_id(2) != 0)
    def _():
        yr_ref[...] = yr.astype(yr_ref.dtype)
        q_ref[...] = q.astype(q_ref.dtype)

    @pl.when(pl.program_id(2) == 0)
    def _():
        row = lax.broadcasted_iota(jnp.int32, yr.shape, 0)
        ynyq = _alt_sum(u.astype(F32)) * km[0:1, :]
        yr_ref[...] = jnp.where(row == 0, 0.5 * yr, yr).astype(yr_ref.dtype)
        q_ref[...] = jnp.where(row == 0, 0.5 * ynyq, q).astype(q_ref.dtype)


def dft_fwd(cmat, smat, u, col0, kr, km, order, seq_len, d):
    t = u.shape[0]
    tk = _tile(seq_len, 512)
    tn = _tile(d, 512)
    cb = col0 // tn
    out = jax.ShapeDtypeStruct((t, d), BF16)
    nk = seq_len // tk
    return pl.pallas_call(
        _dft_fwd_kernel,
        out_shape=(out, out),
        grid=(t // seq_len, d // tn, nk),
        in_specs=[pl.BlockSpec((tk, seq_len), lambda b, j, k: (k, 0)),
                  pl.BlockSpec((tk, seq_len), lambda b, j, k: (k, 0)),
                  pl.BlockSpec((seq_len, tn), lambda b, j, k: (b, cb + j)),
                  pl.BlockSpec((1, tk, tn), lambda b, j, k: (order, k, j)),
                  pl.BlockSpec((1, tk, tn), lambda b, j, k: (order, k, j))],
        out_specs=(pl.BlockSpec((tk, tn), lambda b, j, k: (b * nk + k, j)),
                   pl.BlockSpec((tk, tn), lambda b, j, k: (b * nk + k, j))),
        compiler_params=_params("arbitrary", "arbitrary", "arbitrary"),
        name="dft_fwd",
    )(cmat, smat, u, kr, km)


def _dft_inv_kernel(c_ref, s_ref, yr_ref, q_ref, u_ref, g_ref, bias_ref, o_ref, *, seq_len):
    acc = jnp.dot(c_ref[...], yr_ref[...], preferred_element_type=F32)
    acc += jnp.dot(s_ref[...], q_ref[...], preferred_element_type=F32)
    tt = acc.shape[0]
    t_idx = pl.program_id(2) * tt + lax.broadcasted_iota(jnp.int32, acc.shape, 0)
    half_nyq = q_ref[0:1, :].astype(F32)
    acc += jnp.where((t_idx & 1) == 1, -half_nyq, half_nyq)
    u = u_ref[...].astype(F32)
    y = acc * (1.0 / seq_len) + u * bias_ref[0]
    o_ref[...] = (g_ref[...].astype(F32) * y).astype(o_ref.dtype)


def dft_inv(cmat, smat, yr, q, u, ucol0, gsrc, gcol0, bias, order, seq_len, d):
    t = yr.shape[0]
    tt = _tile(seq_len, 512)
    tn = _tile(d, 512)
    ub, gb = ucol0 // tn, gcol0 // tn
    nt = seq_len // tt
    return pl.pallas_call(
        functools.partial(_dft_inv_kernel, seq_len=seq_len),
        out_shape=jax.ShapeDtypeStruct((t, d), BF16),
        grid=(t // seq_len, d // tn, nt),
        in_specs=[pl.BlockSpec((tt, seq_len), lambda b, j, k: (k, 0)),
                  pl.BlockSpec((tt, seq_len), lambda b, j, k: (k, 0)),
                  pl.BlockSpec((seq_len, tn), lambda b, j, k: (b, j)),
                  pl.BlockSpec((seq_len, tn), lambda b, j, k: (b, j)),
                  pl.BlockSpec((tt, tn), lambda b, j, k: (b * nt + k, ub + j)),
                  pl.BlockSpec((tt, tn), lambda b, j, k: (b * nt + k, gb + j)),
                  pl.BlockSpec((1, 1, tn), lambda b, j, k: (order, 0, j))],
        out_specs=pl.BlockSpec((tt, tn), lambda b, j, k: (b * nt + k, j)),
        compiler_params=_params("arbitrary", "arbitrary", "arbitrary"),
        name="dft_inv",
    )(cmat, smat, yr, q, u, gsrc, bias)


def dft_tables(seq_len):
    n = jnp.arange(seq_len, dtype=jnp.int32)
    ang = ((n[:, None] * n[None, :]) % (2 * seq_len)).astype(F32) * (math.pi / seq_len)
    return jnp.cos(ang).astype(BF16), jnp.sin(ang).astype(BF16)


def hyena_long_convs(u3, tables, spectrum, bias, seq_len, d):
    cmat, smat = tables
    kr, km = spectrum
    yr, q = dft_fwd(cmat, smat, u3, 0, kr, km, 0, seq_len, d)
    z = dft_inv(cmat, smat, yr, q, u3, 0, u3, d, bias, 0, seq_len, d)
    yr, q = dft_fwd(cmat, smat, z, 0, kr, km, 1, seq_len, d)
    return dft_inv(cmat, smat, yr, q, z, 0, u3, 2 * d, bias, 1, seq_len, d)


FFT_N1 = 64
SUB = 8


def fft_tables(seq_len):
    n = 2 * seq_len
    n1, n2 = FFT_N1, n // FFT_N1
    two_pi = 2.0 * math.pi
    eye = jnp.eye(SUB, dtype=F32)
    k1 = jnp.arange(n1, dtype=jnp.int32)
    h1 = jnp.arange(n1 // 2, dtype=jnp.int32)
    ang = ((k1[:, None] * h1[None, :]) % n1).astype(F32) * (two_pi / n1)
    ca = jnp.stack([jnp.cos(ang), -jnp.sin(ang)], axis=1)
    ka = jnp.einsum('krn,jm->krjnm', ca, eye).reshape(n1 * 2 * SUB, (n1 // 2) * SUB)
    c2 = jnp.stack([jnp.cos(ang.T), -jnp.sin(ang.T)], axis=2)
    ka2 = jnp.einsum('tkr,jm->tjkrm', c2, eye).reshape((n1 // 2) * SUB, n1 * 2 * SUB)
    k2 = jnp.arange(n2 // 2, dtype=jnp.int32)
    m2 = jnp.arange(n2, dtype=jnp.int32)
    k = k1[:, None, None] + n1 * k2[None, :, None]
    th = ((k * m2[None, None, :]) % n).astype(F32) * (two_pi / n)
    c, s = jnp.cos(th), jnp.sin(th)
    fb = jnp.concatenate([jnp.concatenate([c, s], axis=2),
                          jnp.concatenate([-s, c], axis=2)], axis=1)
    ct, st = jnp.swapaxes(c, 1, 2), jnp.swapaxes(s, 1, 2)
    gb = jnp.concatenate([jnp.concatenate([ct, -st], axis=2),
                          jnp.concatenate([st, ct], axis=2)], axis=1)
    return tuple(a.astype(BF16) for a in (ka, fb, gb, ka2))


def _fft_stage_a(src_ref, ka_ref, a_ref):
    n1h, n2, tn = src_ref.shape
    for g in range(n2 // SUB):
        rhs = src_ref[:, g * SUB:(g + 1) * SUB, :].reshape(n1h * SUB, tn).astype(BF16)
        out = jnp.dot(ka_ref[...], rhs, preferred_element_type=F32)
        a_ref[:, :, g * SUB:(g + 1) * SUB, :] = out.reshape(FFT_N1, 2, SUB, tn)


def _fft_stage_b(fb_ref, a_ref, k1):
    _, _, n2, tn = a_ref.shape
    slab = a_ref[k1].reshape(2 * n2, tn).astype(BF16)
    return jnp.dot(fb_ref[k1], slab, preferred_element_type=F32)


def _block_alt_sum(src_ref):
    return _alt_sum(jnp.sum(src_ref[...], axis=0))


def _fftspec_kernel(tf_ref, tb_ref, ka_ref, fb_ref, kf_ref, knyq_ref, src_ref, a_ref):
    h = a_ref.shape[2] // 2
    tn = a_ref.shape[3]
    nyq = None
    for which, t_ref in enumerate((tf_ref, tb_ref)):
        src_ref[...] = t_ref[0].astype(F32)
        part = _block_alt_sum(src_ref)
        nyq = part if nyq is None else nyq + part
        _fft_stage_a(src_ref, ka_ref, a_ref)

        def body(k1, carry, which=which):
            x = _fft_stage_b(fb_ref, a_ref, k1).reshape(2, h, tn)
            if which == 0:
                kf_ref[0, k1] = x
            else:
                sign = 1.0 - 2.0 * jnp.asarray(k1 & 1, F32)
                kf_ref[0, k1] = kf_ref[0, k1] + sign * x
            return carry

        lax.fori_loop(0, FFT_N1, body, 0, unroll=4)
    knyq_ref[0] = nyq
    dc = kf_ref[0, 0]
    row = lax.broadcasted_iota(jnp.int32, dc.shape, 1)
    kf_ref[0, 0] = jnp.where(row == 0, 0.5 * dc, dc)


def _const_spec(shape):
    return pl.BlockSpec(shape, lambda *_: (0,) * len(shape), pipeline_mode=pl.Buffered(1))


def fft_filter_spectrum(tabs, taps_f, taps_b):
    ka, fb, _, _ = tabs
    _, n1h, n2, d = taps_f.shape
    tn = _tile(d, 256)
    tap = pl.BlockSpec((1, n1h, n2, tn), lambda o, j: (o, 0, 0, j))
    return pl.pallas_call(
        _fftspec_kernel,
        out_shape=(jax.ShapeDtypeStruct((2, FFT_N1, 2, n2 // 2, d), F32),
                   jax.ShapeDtypeStruct((2, 1, d), F32)),
        grid=(2, d // tn),
        in_specs=[tap, tap, _const_spec(ka.shape), _const_spec(fb.shape)],
        out_specs=(pl.BlockSpec((1, FFT_N1, 2, n2 // 2, tn), lambda o, j: (o, 0, 0, 0, j)),
                   pl.BlockSpec((1, 1, tn), lambda o, j: (o, 0, j))),
        scratch_shapes=[pltpu.VMEM((n1h, n2, tn), F32), pltpu.VMEM((FFT_N1, 2, n2, tn), F32)],
        compiler_params=_params("arbitrary", "arbitrary"),
        name="fft_filter_spectrum",
    )(taps_f, taps_b, ka, fb)


def _fftconv_kernel(u_ref, g_ref, bias_ref, kf_ref, knyq_ref, ka_ref, fb_ref, gb_ref, ka2_ref,
                    o_ref, src_ref, a_ref):
    n1h, n2, tn = src_ref.shape
    h = n2 // 2
    seq_len = n1h * n2
    src_ref[...] = u_ref[...].astype(F32)
    ynyq = _block_alt_sum(src_ref) * knyq_ref[0] * (0.5 / seq_len)
    _fft_stage_a(src_ref, ka_ref, a_ref)

    def body(k1, carry):
        x = _fft_stage_b(fb_ref, a_ref, k1)
        xr, xi = x[:h], x[h:]
        kr, ki = kf_ref[0, k1, 0], kf_ref[0, k1, 1]
        y = jnp.concatenate([xr * kr - xi * ki, xr * ki + xi * kr], axis=0).astype(BF16)
        z = jnp.dot(gb_ref[k1], y, preferred_element_type=F32)
        a_ref[k1] = z.reshape(2, n2, tn)
        return carry

    lax.fori_loop(0, FFT_N1, body, 0, unroll=True)
    bias = bias_ref[0]
    pair = 2 * SUB
    row = lax.broadcasted_iota(jnp.int32, (n1h, pair, tn), 1)
    nyq = jnp.where((row & 1) == 1, -ynyq, ynyq)
    for gg in range(n2 // pair):
        parts = []
        for g in (2 * gg, 2 * gg + 1):
            zg = a_ref[:, :, g * SUB:(g + 1) * SUB, :].reshape(FFT_N1 * 2 * SUB, tn).astype(BF16)
            acc = jnp.dot(ka2_ref[...], zg, preferred_element_type=F32)
            parts.append(acc.reshape(n1h, SUB, tn))
        acc = jnp.concatenate(parts, axis=1)
        rows = slice(gg * pair, (gg + 1) * pair)
        y = acc * (1.0 / seq_len) + nyq + src_ref[:, rows, :] * bias
        o_ref[:, rows, :] = (g_ref[:, rows, :].astype(F32) * y).astype(o_ref.dtype)


def fft_long_conv(tabs, spec, u, ucol0, gsrc, gcol0, bias, order, seq_len, d):
    ka, fb, gb, ka2 = tabs
    kf, knyq = spec
    nb, n2, _ = u.shape
    n1h = seq_len // n2
    tn = _tile(d, 256)
    ub, gb_ = ucol0 // tn, gcol0 // tn
    return pl.pallas_call(
        _fftconv_kernel,
        out_shape=jax.ShapeDtypeStruct((nb, n2, d), BF16),
        grid=(d // tn, nb // n1h),
        in_specs=[pl.BlockSpec((n1h, n2, tn), lambda j, b: (b, 0, ub + j)),
                  pl.BlockSpec((n1h, n2, tn), lambda j, b: (b, 0, gb_ + j)),
                  pl.BlockSpec((1, 1, tn), lambda j, b: (order, 0, j)),
                  pl.BlockSpec((1, FFT_N1, 2, n2 // 2, tn), lambda j, b: (order, 0, 0, 0, j),
                               pipeline_mode=pl.Buffered(1)),
                  pl.BlockSpec((1, 1, tn), lambda j, b: (order, 0, j)),
                  _const_spec(ka.shape), _const_spec(fb.shape), _const_spec(gb.shape),
                  _const_spec(ka2.shape)],
        out_specs=pl.BlockSpec((n1h, n2, tn), lambda j, b: (b, 0, j)),
        scratch_shapes=[pltpu.VMEM((n1h, n2, tn), F32), pltpu.VMEM((FFT_N1, 2, n2, tn), F32)],
        compiler_params=_params("arbitrary", "arbitrary"),
        name="fft_long_conv",
    )(u, gsrc, bias, kf, knyq, ka, fb, gb, ka2)


def hyena_long_convs_fft(u3, tabs, spec, bias, seq_len, d):
    t = u3.shape[0]
    n2 = 2 * seq_len // FFT_N1
    u3b = u3.reshape(t // n2, n2, 3 * d)
    z = fft_long_conv(tabs, spec, u3b, 0, u3b, d, bias, 0, seq_len, d)
    z = fft_long_conv(tabs, spec, z, 0, u3b, 2 * d, bias, 1, seq_len, d)
    return z.reshape(t, d)


def rope_tables(seq_len):
    pos = jnp.arange(seq_len, dtype=jnp.int32)
    r = (pos // GRID_W).astype(F32)[:, None]
    col = (pos % GRID_W).astype(F32)[:, None]
    inv = ROPE_BASE ** (-(2.0 * jnp.arange(ROPE_PAIRS, dtype=F32)) / (2 * ROPE_PAIRS))
    ar, ac = r * inv, col * inv
    zeros = jnp.zeros_like(ar)
    cos = jnp.concatenate([jnp.cos(ar), jnp.cos(ar), jnp.cos(ac), jnp.cos(ac)], axis=1)
    sa = jnp.concatenate([-jnp.sin(ar), zeros, -jnp.sin(ac), zeros], axis=1)
    sb = jnp.concatenate([zeros, jnp.sin(ar), zeros, jnp.sin(ac)], axis=1)
    rep = LANES // HEAD_DIM
    return tuple(jnp.tile(a, (1, rep)) for a in (cos, sa, sb))


LOG2E = 1.4426950408889634
Q_SCALE = HEAD_DIM ** -0.5 * LOG2E


def _attend(q_ref, sink_ref, keys, vals, bias, o_ref, n_heads, q_per_kv, q_scale=None):
    nq = q_ref.shape[0]
    s_len = keys.shape[0]
    lane = lax.broadcasted_iota(jnp.int32, (s_len, HEAD_DIM), 1)
    ones_col = jnp.where(lane == 0, 1.0, 0.0).astype(BF16)
    hidx = lax.broadcasted_iota(jnp.int32, (q_per_kv, 1, 1), 0)
    pending = []
    for g in range(n_heads // q_per_kv):
        qs = []
        sink = jnp.zeros((q_per_kv, 1, 1), F32)
        for h in range(q_per_kv):
            col = (g * q_per_kv + h) * HEAD_DIM
            q = q_ref[:, col:col + HEAD_DIM]
            if q_scale is not None:
                q = (q.astype(F32) * q_scale).astype(BF16)
            qs.append(q)
            sink = jnp.where(hidx == h, sink_ref[g * q_per_kv + h] * LOG2E, sink)
        qg = jnp.concatenate(qs, axis=0)
        kg = keys[:, g * HEAD_DIM:(g + 1) * HEAD_DIM]
        vg = jnp.concatenate([vals[:, g * HEAD_DIM:(g + 1) * HEAD_DIM], ones_col], axis=1)
        s = lax.dot_general(qg, kg, (((1,), (1,)), ((), ())), preferred_element_type=F32)
        s = s.reshape(q_per_kv, nq, s_len)
        if bias is not None:
            s = s + bias[None]
        m = jnp.maximum(jnp.max(s, axis=-1, keepdims=True), sink)
        p = jnp.exp2(s - m).astype(BF16).reshape(q_per_kv * nq, s_len)
        acc = jnp.dot(p, vg, preferred_element_type=F32).reshape(q_per_kv, nq, 2 * HEAD_DIM)
        denom = acc[:, :, HEAD_DIM:HEAD_DIM + 1] + jnp.exp2(sink - m)
        out = acc[:, :, :HEAD_DIM] / denom
        for h in range(q_per_kv):
            pending.append(out[h])
            if len(pending) == LANES // HEAD_DIM:
                lo = (g * q_per_kv + h + 1) * HEAD_DIM - LANES
                o_ref[:, lo:lo + LANES] = jnp.concatenate(pending, axis=1).astype(o_ref.dtype)
                pending = []


def _attn_kernel(sink_ref, q_ref, k_ref, v_ref, kc_ref, vc_ref, o_ref, *, seq_len, n_heads, q_per_kv):
    span = Q_BLOCK + 2 * WINDOW
    ctx_len = kc_ref.shape[0]
    start = pl.program_id(1) * Q_BLOCK
    kstart = pl.multiple_of(jnp.clip(start - WINDOW, 0, seq_len - span), Q_BLOCK)
    qpos = start + lax.broadcasted_iota(jnp.int32, (Q_BLOCK, span + ctx_len), 0)
    col = lax.broadcasted_iota(jnp.int32, (Q_BLOCK, span + ctx_len), 1)
    visible = (jnp.abs(qpos - (kstart + col)) <= WINDOW) | (col >= span)
    bias = jnp.where(visible, 0.0, -1e30).astype(F32)
    keys = jnp.concatenate([k_ref[pl.ds(kstart, span), :], kc_ref[...]], axis=0)
    vals = jnp.concatenate([v_ref[pl.ds(kstart, span), :], vc_ref[...]], axis=0)
    _attend(q_ref, sink_ref, keys, vals, bias, o_ref, n_heads, q_per_kv)


def window_attention(qkv, qkv_ctx, sink, batch, seq_len, ctx_len, d_q, d_kv):
    n_heads = d_q // HEAD_DIM
    nqb = seq_len // Q_BLOCK
    kcol, vcol = d_q // d_kv, d_q // d_kv + 1
    return pl.pallas_call(
        functools.partial(_attn_kernel, seq_len=seq_len, n_heads=n_heads,
                          q_per_kv=n_heads // N_KV_HEADS),
        out_shape=jax.ShapeDtypeStruct((batch * seq_len, d_q), BF16),
        grid=(batch, nqb),
        in_specs=[pl.BlockSpec(memory_space=pltpu.SMEM),
                  pl.BlockSpec((Q_BLOCK, d_q), lambda b, i: (b * nqb + i, 0)),
                  pl.BlockSpec((seq_len, d_kv), lambda b, i: (b, kcol)),
                  pl.BlockSpec((seq_len, d_kv), lambda b, i: (b, vcol)),
                  pl.BlockSpec((ctx_len, d_kv), lambda b, i: (b, kcol)),
                  pl.BlockSpec((ctx_len, d_kv), lambda b, i: (b, vcol))],
        out_specs=pl.BlockSpec((Q_BLOCK, d_q), lambda b, i: (b * nqb + i, 0)),
        compiler_params=_params("arbitrary", "arbitrary"),
        name="window_attention",
    )(sink, qkv, qkv, qkv, qkv_ctx, qkv_ctx)


def _ctx_attn_kernel(sink_ref, q_ref, kc_ref, vc_ref, o_ref, *, n_heads, q_per_kv, scale):
    _attend(q_ref, sink_ref, kc_ref[...], vc_ref[...], None, o_ref, n_heads, q_per_kv,
            q_scale=scale)


def ctx_attention(qkv_ctx, sink, batch, ctx_len, d_q, d_kv):
    n_heads = d_q // HEAD_DIM
    kcol, vcol = d_q // d_kv, d_q // d_kv + 1
    return pl.pallas_call(
        functools.partial(_ctx_attn_kernel, n_heads=n_heads, q_per_kv=n_heads // N_KV_HEADS,
                          scale=Q_SCALE),
        out_shape=jax.ShapeDtypeStruct((batch * ctx_len, d_q), BF16),
        grid=(batch,),
        in_specs=[pl.BlockSpec(memory_space=pltpu.SMEM),
                  pl.BlockSpec((ctx_len, d_q), lambda b: (b, 0)),
                  pl.BlockSpec((ctx_len, d_kv), lambda b: (b, kcol)),
                  pl.BlockSpec((ctx_len, d_kv), lambda b: (b, vcol))],
        out_specs=pl.BlockSpec((ctx_len, d_q), lambda b: (b, 0)),
        compiler_params=_params("arbitrary"),
        name="ctx_attention",
    )(sink, qkv_ctx, qkv_ctx, qkv_ctx)


def _router_kernel(x_ref, sh_ref, sc_ref, wr_ref, br_ref, cnt0_ref, route_ref, cnt_ref, carry_ref):
    @pl.when(pl.program_id(0) == 0)
    def _():
        carry_ref[...] = cnt0_ref[...]

    h = _modulated_norm(x_ref[...], sh_ref[0], sc_ref[0])
    h_hi = h.astype(BF16)
    h_lo = (h - h_hi.astype(F32)).astype(BF16)
    logits = (jnp.dot(h_hi, wr_ref[0], preferred_element_type=F32)
              + (jnp.dot(h_hi, wr_ref[1], preferred_element_type=F32)
                 + jnp.dot(h_lo, wr_ref[0], preferred_element_type=F32))) + br_ref[...]
    tm = logits.shape[0]
    lane_i = lax.broadcasted_iota(jnp.int32, logits.shape, 1)
    lane = lane_i.astype(F32)
    neg = -jnp.inf
    lg = jnp.where(lane_i < N_GROUPS, logits, neg)
    mg = jnp.max(lg, axis=-1, keepdims=True)
    p_g = 1.0 / jnp.sum(jnp.exp(lg - mg), axis=-1, keepdims=True)
    g_sel = jnp.min(jnp.where(lg == mg, lane, float(ROUTE_W)), axis=-1, keepdims=True)
    lo = N_GROUPS + g_sel * EXPERTS_PER_GROUP
    lf = jnp.where((lane >= lo) & (lane < lo + EXPERTS_PER_GROUP), logits, neg)
    m1 = jnp.max(lf, axis=-1, keepdims=True)
    i1 = jnp.min(jnp.where(lf == m1, lane, float(ROUTE_W)), axis=-1, keepdims=True)
    lf2 = jnp.where(lane == i1, neg, lf)
    m2 = jnp.max(lf2, axis=-1, keepdims=True)
    i2 = jnp.min(jnp.where(lf2 == m2, lane, float(ROUTE_W)), axis=-1, keepdims=True)
    a2 = jnp.exp(m2 - m1)
    gate1 = p_g / (1.0 + a2)
    gate2 = gate1 * a2
    sel1 = lane == i1
    sel2 = lane == i2
    onehot = jnp.where(sel1 | sel2, 1.0, 0.0)
    r_i = lax.broadcasted_iota(jnp.int32, (tm, tm), 0)
    c_i = lax.broadcasted_iota(jnp.int32, (tm, tm), 1)
    tri = jnp.where(r_i > c_i, 1.0, 0.0).astype(BF16)
    before = jnp.dot(tri, onehot.astype(BF16), preferred_element_type=F32) + carry_ref[...]
    rank1 = jnp.sum(jnp.where(sel1, before, 0.0), axis=-1, keepdims=True)
    rank2 = jnp.sum(jnp.where(sel2, before, 0.0), axis=-1, keepdims=True)
    carry_ref[...] += jnp.sum(onehot, axis=0, keepdims=True)
    cnt_ref[...] = carry_ref[...]
    vals = (i1 - N_GROUPS, i2 - N_GROUPS, rank1, rank2, gate1, gate2)
    route = jnp.zeros_like(logits)
    for idx, v in enumerate(vals):
        route = jnp.where(lane_i == idx, v, route)
    route_ref[...] = route


def router(x, sh, sc, wr, br, cnt0, rows_per_seg):
    t, d = x.shape
    tm = _tile(rows_per_seg, 512)
    seg = lambda i: ((i * tm) // rows_per_seg, 0, 0)
    wr_hi = wr.astype(BF16)
    wr = jnp.stack([wr_hi, (wr - wr_hi.astype(F32)).astype(BF16)])
    return pl.pallas_call(
        _router_kernel,
        out_shape=(jax.ShapeDtypeStruct((t, ROUTE_W), F32), jax.ShapeDtypeStruct((1, ROUTE_W), F32)),
        grid=(t // tm,),
        in_specs=[pl.BlockSpec((tm, d), lambda i: (i, 0)),
                  pl.BlockSpec((1, 1, d), seg), pl.BlockSpec((1, 1, d), seg),
                  pl.BlockSpec((2, d, ROUTE_W), lambda i: (0, 0, 0)),
                  pl.BlockSpec((1, ROUTE_W), lambda i: (0, 0)),
                  pl.BlockSpec((1, ROUTE_W), lambda i: (0, 0))],
        out_specs=(pl.BlockSpec((tm, ROUTE_W), lambda i: (i, 0)),
                   pl.BlockSpec((1, ROUTE_W), lambda i: (0, 0))),
        scratch_shapes=[pltpu.VMEM((1, ROUTE_W), F32)],
        compiler_params=_params("arbitrary"),
        name="router",
    )(x, sh, sc, wr, br, cnt0)


def _token_copy(src, src_tok, dst, dst_tok, sem, rows):
    return pltpu.make_async_copy(src.at[pl.ds(src_tok * rows, rows)],
                                 dst.at[pl.ds(dst_tok * rows, rows)], sem)


TOKENS_PER_TRIP = 2


def _start_copy(copy, k):
    copy.start(priority=k)


def _wait_copy(copy, k):
    del k
    copy.wait()


def _to_token_rows(mat, rows_ref):
    tm, d = mat.shape
    rows = d // LANES
    for s in range(rows):
        rows_ref[pl.ds(s, tm, stride=rows), :] = mat[:, s * LANES:(s + 1) * LANES]


def _token_cols(rows_ref, s, tm, rows):
    return rows_ref[pl.ds(s, tm, stride=rows), :]


SLOT_BITS = 20


def _slot(code_ref, pad_start_ref, idx):
    code = code_ref[idx]
    return pad_start_ref[code >> SLOT_BITS] + (code & ((1 << SLOT_BITS) - 1))


def _dispatch_kernel(code_ref, pad_start_ref, pad_end_ref, nused_ref, x_ref, sh_ref, sc_ref, *rest,
                     first, n_blocks):
    if first:
        xs_ref, h_ref, zero_ref, sem, zsem = rest
    else:
        _, xs_ref, h_ref, zero_ref, sem, zsem = rest
    tm, d = x_ref.shape
    rows = d // LANES
    base = pl.program_id(0) * tm

    if first:
        @pl.when(pl.program_id(0) == 0)
        def _():
            zero_ref[...] = jnp.zeros_like(zero_ref)
            blk_rows = MOE_BLOCK * rows

            def zero_block(blk):
                return pltpu.make_async_copy(zero_ref, xs_ref.at[pl.ds(blk * blk_rows, blk_rows)], zsem)

            def last_blocks(op):
                def body(e, carry):
                    hi = pad_end_ref[e]
                    lo = pad_start_ref[e]

                    @pl.when(hi > lo)
                    def _():
                        op(zero_block(hi // MOE_BLOCK - 1))
                    return carry
                lax.fori_loop(0, N_EXPERTS, body, 0)

            def tail_blocks(op):
                def body(blk, carry):
                    op(zero_block(blk))
                    return carry
                lax.fori_loop(nused_ref[0], n_blocks, body, 0)

            last_blocks(lambda c: c.start())
            tail_blocks(lambda c: c.start())
            last_blocks(lambda c: c.wait())
            tail_blocks(lambda c: c.wait())

    step = pl.program_id(0)
    slot = step % 2
    _to_token_rows(_modulated_norm(x_ref[...], sh_ref[0], sc_ref[0]), h_ref.at[slot])

    def copies(which, row0, op):
        def body(i, carry):
            for j in range(TOKENS_PER_TRIP):
                r = i * TOKENS_PER_TRIP + j
                for k in range(2):
                    op(_token_copy(h_ref.at[which], r, xs_ref,
                                   _slot(code_ref, pad_start_ref, 2 * (row0 + r) + k), sem.at[which],
                                   rows), k)
            return carry
        lax.fori_loop(0, tm // TOKENS_PER_TRIP, body, 0)

    copies(slot, base, _start_copy)

    @pl.when(step > 0)
    def _():
        copies(1 - slot, base - tm, _wait_copy)

    @pl.when(step == pl.num_programs(0) - 1)
    def _():
        copies(slot, base, _wait_copy)


def dispatch(code, pad_start, pad_end, nused, n_blocks, x, sh, sc, xs, rows_per_seg):
    t, d = x.shape
    rows = d // LANES
    tm = _tile(rows_per_seg, 256)
    first = xs is None
    seg = lambda i, *_: ((i * tm) // rows_per_seg, 0, 0)
    in_specs = [pl.BlockSpec((tm, d), lambda i, *_: (i, 0)),
                pl.BlockSpec((1, 1, d), seg), pl.BlockSpec((1, 1, d), seg)]
    args = [code, pad_start, pad_end, nused, x, sh, sc]
    if not first:
        in_specs.append(pl.BlockSpec(memory_space=pl.ANY))
        args.append(xs)
    return pl.pallas_call(
        functools.partial(_dispatch_kernel, first=first, n_blocks=n_blocks),
        out_shape=jax.ShapeDtypeStruct((n_blocks * MOE_BLOCK * rows, LANES), F32),
        grid_spec=pltpu.PrefetchScalarGridSpec(
            num_scalar_prefetch=4,
            grid=(t // tm,),
            in_specs=in_specs,
            out_specs=pl.BlockSpec(memory_space=pl.ANY),
            scratch_shapes=[pltpu.VMEM((2, tm * rows, LANES), F32),
                            pltpu.VMEM((MOE_BLOCK * rows, LANES), F32),
                            pltpu.SemaphoreType.DMA((2,)), pltpu.SemaphoreType.DMA]),
        input_output_aliases={} if first else {7: 0},
        compiler_params=_params("arbitrary"),
        name="moe_dispatch",
    )(*args)


def _expert_kernel(bexp_ref, nused_ref, xs_ref, w1_ref, w3_ref, w2_ref, ys_ref, xb, w13b, w2b):
    i = pl.program_id(0)
    e = bexp_ref[i]
    prev = bexp_ref[jnp.maximum(i - 1, 0)]
    tm, d = xb.shape
    rows = d // LANES
    de = w2b.shape[0]

    @pl.when((i == 0) | (e != prev))
    def _():
        w13b[:, :de] = w1_ref[0, 0].astype(BF16)
        w13b[:, de:] = w3_ref[0, 0].astype(BF16)
        w2b[...] = w2_ref[0, 0].astype(BF16)

    @pl.when(i < nused_ref[0])
    def _():
        for s in range(rows):
            xb[:, s * LANES:(s + 1) * LANES] = _token_cols(xs_ref, s, tm, rows).astype(BF16)
        h13 = jnp.dot(xb[...], w13b[...], preferred_element_type=F32)
        h1, h3 = h13[:, :de], h13[:, de:]
        a = (h1 * jax.nn.sigmoid(h1) * h3).astype(BF16)
        _to_token_rows(jnp.dot(a, w2b[...], preferred_element_type=F32), ys_ref)

    @pl.when(i >= nused_ref[0])
    def _():
        ys_ref[...] = jnp.zeros_like(ys_ref)


def experts(block_exp, nused, xs, w1, w3, w2, layer):
    d, de = w1.shape[-2:]
    blk_rows = MOE_BLOCK * (d // LANES)
    blk = lambda i, bexp, nu: (jnp.minimum(i, nu[0] - 1), 0)
    wspec = lambda shape: pl.BlockSpec(shape, lambda i, bexp, nu: (layer, bexp[i], 0, 0))
    return pl.pallas_call(
        _expert_kernel,
        out_shape=jax.ShapeDtypeStruct(xs.shape, F32),
        grid_spec=pltpu.PrefetchScalarGridSpec(
            num_scalar_prefetch=2,
            grid=(xs.shape[0] // blk_rows,),
            in_specs=[pl.BlockSpec((blk_rows, LANES), blk),
                      wspec((1, 1, d, de)), wspec((1, 1, d, de)), wspec((1, 1, de, d))],
            out_specs=pl.BlockSpec((blk_rows, LANES), lambda i, bexp, nu: (i, 0)),
            scratch_shapes=[pltpu.VMEM((MOE_BLOCK, d), BF16), pltpu.VMEM((d, 2 * de), BF16),
                            pltpu.VMEM((de, d), BF16)]),
        compiler_params=_params("arbitrary"),
        name="moe_experts",
    )(block_exp, nused, xs, w1, w3, w2)


def _combine_kernel(code_ref, pad_start_ref, x_ref, route_ref, g_ref, gain_ref, ys_ref, o_ref, y0_ref,
                    y1_ref, sem, *, final):
    tm, d = x_ref.shape
    rows = d // LANES
    base = pl.program_id(0) * tm

    step = pl.program_id(0)
    slot = step % 2

    def copies(which, row0, op):
        def body(i, carry):
            for j in range(TOKENS_PER_TRIP):
                r = i * TOKENS_PER_TRIP + j
                for k, y_ref in enumerate((y0_ref, y1_ref)):
                    op(_token_copy(ys_ref, _slot(code_ref, pad_start_ref, 2 * (row0 + r) + k),
                                   y_ref.at[which], r, sem.at[which], rows), k)
            return carry
        lax.fori_loop(0, tm // TOKENS_PER_TRIP, body, 0)

    @pl.when(step == 0)
    def _():
        copies(slot, base, _start_copy)

    @pl.when(step + 1 < pl.num_programs(0))
    def _():
        copies(1 - slot, base + tm, _start_copy)

    copies(slot, base, _wait_copy)
    y0_rows, y1_rows = y0_ref.at[slot], y1_ref.at[slot]
    route = route_ref[...]
    g0, g1 = route[:, 4:5], route[:, 5:6]
    ssq = jnp.zeros((tm, 1), F32)
    for s in range(rows):
        cols = slice(s * LANES, (s + 1) * LANES)
        y = g0 * _token_cols(y0_rows, s, tm, rows) + g1 * _token_cols(y1_rows, s, tm, rows)
        out = x_ref[:, cols] + g_ref[0][:, cols] * y
        o_ref[:, cols] = out
        ssq = ssq + jnp.sum(out * out, axis=-1, keepdims=True)
    if final:
        o_ref[...] = o_ref[...] * lax.rsqrt(ssq * (1.0 / d) + EPS) * gain_ref[...]


def combine(code, pad_start, x, route, gate, gain, ys, rows_per_seg, final):
    t, d = x.shape
    rows = d // LANES
    tm = _tile(rows_per_seg, 256)
    return pl.pallas_call(
        functools.partial(_combine_kernel, final=final),
        out_shape=jax.ShapeDtypeStruct((t, d), F32),
        grid_spec=pltpu.PrefetchScalarGridSpec(
            num_scalar_prefetch=2,
            grid=(t // tm,),
            in_specs=[pl.BlockSpec((tm, d), lambda i, *_: (i, 0)),
                      pl.BlockSpec((tm, ROUTE_W), lambda i, *_: (i, 0)),
                      pl.BlockSpec((1, 1, d), lambda i, *_: ((i * tm) // rows_per_seg, 0, 0)),
                      pl.BlockSpec((1, d), lambda i, *_: (0, 0)),
                      pl.BlockSpec(memory_space=pl.ANY)],
            out_specs=pl.BlockSpec((tm, d), lambda i, *_: (i, 0)),
            scratch_shapes=[pltpu.VMEM((2, tm * rows, LANES), F32),
                            pltpu.VMEM((2, tm * rows, LANES), F32),
                            pltpu.SemaphoreType.DMA((2,))]),
        compiler_params=_params("arbitrary"),
        name="moe_combine",
    )(code, pad_start, x, route, gate, gain, ys)


def _slot_plan(route_list, counts):
    cnt = counts[0, N_GROUPS:N_GROUPS + N_EXPERTS].astype(jnp.int32)
    padded = (cnt + MOE_BLOCK - 1) // MOE_BLOCK * MOE_BLOCK
    pad_end = jnp.cumsum(padded)
    pad_start = pad_end - padded
    n_assign = 2 * sum(r.shape[0] for r in route_list)
    n_blocks = -(-n_assign // MOE_BLOCK) + N_EXPERTS
    block_start = jnp.arange(n_blocks, dtype=jnp.int32) * MOE_BLOCK
    block_exp = jnp.minimum(jnp.sum(block_start[:, None] >= pad_end[None, :], axis=1),
                            N_EXPERTS - 1).astype(jnp.int32)
    nused = (pad_end[-1:] // MOE_BLOCK).astype(jnp.int32)
    assert n_assign < (1 << SLOT_BITS)
    codes = [((r[:, 0:2].astype(jnp.int32) << SLOT_BITS) | r[:, 2:4].astype(jnp.int32)).reshape(-1)
             for r in route_list]
    return codes, pad_start.astype(jnp.int32), pad_end.astype(jnp.int32), block_exp, nused, n_blocks


def hier_moe_layer(streams, layer, wr, br, w1, w3, w2, gain, final):
    counts = jnp.zeros((1, ROUTE_W), F32)
    routes = []
    for x, sh, sc, _, rps in streams:
        route, counts = router(x, sh, sc, wr, br, counts, rps)
        routes.append(route)
    codes, pad_start, pad_end, block_exp, nused, n_blocks = _slot_plan(routes, counts)
    xs = None
    for (x, sh, sc, _, rps), code in zip(streams, codes):
        xs = dispatch(code, pad_start, pad_end, nused, n_blocks, x, sh, sc, xs, rps)
    ys = experts(block_exp, nused, xs, w1, w3, w2, layer)
    return [combine(code, pad_start, x, route, g, gain, ys, rps, final)
            for (x, _, _, g, rps), code, route in zip(streams, codes, routes)]


def kernel(x, c, ctx, c_ctx, mod_w, mod_b, hy_w_in, hy_conv_w, hy_conv_b, hy_f_w1, hy_f_b1,
           hy_f_freq1, hy_f_w2, hy_f_b2, hy_f_freq2, hy_f_w3, hy_bias, hy_w_out,
           at_w_qkv, at_sink, at_w_o, moe_wg, moe_bg, moe_we, moe_be, moe_w1, moe_w3, moe_w2,
           final_gain):
    batch, seq_len, d = x.shape
    ctx_len = ctx.shape[1]
    depth = mod_w.shape[0]
    d_q = at_w_qkv.shape[2] - 2 * N_KV_HEADS * HEAD_DIM
    d_kv = N_KV_HEADS * HEAD_DIM
    n_ctx = batch * ctx_len

    xl = x.reshape(batch * seq_len, d)
    xc = ctx.reshape(n_ctx, d)

    n_cond = batch + 1
    cond = jnp.concatenate([c, c_ctx[None, :], jnp.zeros((-n_cond % 8, d), F32)], axis=0)
    mods = adaln(cond, mod_w, mod_b)

    def mod_slices(i):
        parts = [mods[i, :, j * d:(j + 1) * d] for j in range(6)]
        lat = [p[:batch].reshape(batch, 1, d) for p in parts]
        cx = [p[batch:batch + 1].reshape(1, 1, d) for p in parts]
        return lat, cx

    def use_fft(n):
        return n >= 512 and (2 * n) % (FFT_N1 * 2 * SUB) == 0

    tables = {n: (fft_tables(n) if use_fft(n) else dft_tables(n)) for n in {seq_len, ctx_len}}
    rtabs = rope_tables(seq_len)
    gain = final_gain.reshape(1, d)

    for i in range(depth):
        last = i == depth - 1
        j = i // 2
        (sh1, sc1, g1, sh2, sc2, g2), (sh1c, sc1c, g1c, sh2c, sc2c, g2c) = mod_slices(i)
        if i % 2 == 0:
            w_in = hy_w_in[j].astype(BF16)
            w_out = hy_w_out[j].astype(BF16)
            filt = (hy_f_w1[j], hy_f_b1[j], hy_f_freq1[j], hy_f_w2[j], hy_f_b2[j], hy_f_freq2[j],
                    hy_f_w3[j])
            bias = hy_bias[j].reshape(2, 1, d)

            def mixer(xs_, sh, sc, g, n_seq, rps):
                u3 = modmm_conv(xs_, sh, sc, w_in, hy_conv_w[j], hy_conv_b[j], rps, n_seq)
                if use_fft(n_seq):
                    taps = hyena_filter_taps(n_seq, *filt, d, even_odd=False)
                    spec = fft_filter_spectrum(tables[n_seq], *taps)
                    z = hyena_long_convs_fft(u3, tables[n_seq], spec, bias, n_seq, d)
                else:
                    taps = hyena_filter_taps(n_seq, *filt, d, even_odd=True)
                    spec = filter_spectrum(*tables[n_seq], *taps)
                    z = hyena_long_convs(u3, tables[n_seq], spec, bias, n_seq, d)
                return mm_res(z, w_out, xs_, g, rps)

            xl = mixer(xl, sh1, sc1, g1, seq_len, seq_len)
            if not last:
                xc = mixer(xc, sh1c, sc1c, g1c, ctx_len, n_ctx)
        else:
            w_qkv = at_w_qkv[j].astype(BF16)
            w_o = at_w_o[j].astype(BF16)
            sink = at_sink[j]
            qkv = modmm(xl, sh1, sc1, w_qkv, seq_len, rope=(d_q, d_q + d_kv), rope_tabs=rtabs)
            qkv_c = modmm(xc, sh1c, sc1c, w_qkv, n_ctx)
            o = window_attention(qkv, qkv_c, sink, batch, seq_len, ctx_len, d_q, d_kv)
            xl = mm_res(o, w_o, xl, g1, seq_len)
            if not last:
                o_c = ctx_attention(qkv_c, sink, batch, ctx_len, d_q, d_kv)
                xc = mm_res(o_c, w_o, xc, g1c, n_ctx)

        wr = jnp.concatenate([moe_wg[i], moe_we[i],
                              jnp.zeros((d, ROUTE_W - N_GROUPS - N_EXPERTS), F32)], axis=1)
        br = jnp.concatenate([moe_bg[i], moe_be[i],
                              jnp.zeros((ROUTE_W - N_GROUPS - N_EXPERTS,), F32)]).reshape(1, ROUTE_W)
        streams = [(xl, sh2, sc2, g2, seq_len)]
        if not last:
            streams.append((xc, sh2c, sc2c, g2c, n_ctx))
        outs = hier_moe_layer(streams, i, wr, br, moe_w1, moe_w3, moe_w2, gain, last)
        xl = outs[0]
        if not last:
            xc = outs[1]

    return xl.reshape(batch, seq_len, d)
```

```python
import functools
import math

import jax
import jax.numpy as jnp
from jax import lax
from jax.experimental import pallas as pl
from jax.experimental.pallas import tpu as pltpu

F32 = jnp.float32
BF16 = jnp.bfloat16
HIGHEST = lax.Precision.HIGHEST

EPS = 1e-6
GRID_W = 64
HEAD_DIM = 64
N_KV_HEADS = 4
WINDOW = 128
Q_BLOCK = 128
ROPE_PAIRS = HEAD_DIM // 4
ROPE_BASE = 10000.0
N_GROUPS = 8
EXPERTS_PER_GROUP = 8
N_EXPERTS = N_GROUPS * EXPERTS_PER_GROUP
MOE_BLOCK = 256
HY_EMB = 33
HY_BANDS = (HY_EMB - 1) // 2
HY_FAST_DECAY_PCT = 0.3
HY_SLOW_DECAY_PCT = 1.5
HY_DECAY_TARGET = 1e-2

LANES = 128
ROUTE_W = LANES
VMEM_LIMIT = 56 * 1024 * 1024


def _tile(n, pref):
    if n <= pref:
        return n
    t = pref
    while n % t:
        t -= 1
    return t


def _params(*sem):
    return pltpu.CompilerParams(dimension_semantics=sem, vmem_limit_bytes=VMEM_LIMIT)


def _modulated_norm(x, sh, sc):
    y = x * lax.rsqrt(jnp.mean(x * x, axis=-1, keepdims=True) + EPS)
    return y * (1.0 + sc) + sh


def _adaln_kernel(c_ref, w_ref, b_ref, o_ref):
    c = c_ref[...]
    a = c * jax.nn.sigmoid(c)
    o_ref[0] = jnp.dot(a, w_ref[0], precision=HIGHEST, preferred_element_type=F32) + b_ref[0]


def adaln(cond, mod_w, mod_b):
    depth, d, n = mod_w.shape
    r = cond.shape[0]
    tn = _tile(n, 1024)
    return pl.pallas_call(
        _adaln_kernel,
        out_shape=jax.ShapeDtypeStruct((depth, r, n), F32),
        grid=(depth, n // tn),
        in_specs=[pl.BlockSpec((r, d), lambda l, j: (0, 0)),
                  pl.BlockSpec((1, d, tn), lambda l, j: (l, 0, j)),
                  pl.BlockSpec((1, 1, tn), lambda l, j: (l, 0, j))],
        out_specs=pl.BlockSpec((1, r, tn), lambda l, j: (l, 0, j)),
        compiler_params=_params("arbitrary", "arbitrary"),
        name="adaln",
    )(cond, mod_w, mod_b.reshape(depth, 1, n))


def _modmm_kernel(x_ref, sh_ref, sc_ref, w_ref, *rest, rope):
    if rope is None:
        o_ref, h_ref = rest
    else:
        cos_ref, sa_ref, sb_ref, o_ref, h_ref = rest

    @pl.when(pl.program_id(1) == 0)
    def _():
        h_ref[...] = _modulated_norm(x_ref[...], sh_ref[0], sc_ref[0]).astype(BF16)

    acc = jnp.dot(h_ref[...], w_ref[...], preferred_element_type=F32)
    if rope is None:
        o_ref[...] = acc.astype(o_ref.dtype)
        return
    d_q, d_qk = rope
    tn = acc.shape[1]
    for c in range(tn // LANES):
        col0 = pl.program_id(1) * tn + c * LANES
        x = acc[:, c * LANES:(c + 1) * LANES]

        @pl.when(col0 < d_qk)
        def _(x=x, c=c, col0=col0):
            y = (x * cos_ref[...] + pltpu.roll(x, LANES - ROPE_PAIRS, 1) * sa_ref[...]
                 + pltpu.roll(x, ROPE_PAIRS, 1) * sb_ref[...])
            y = y * jnp.where(col0 < d_q, Q_SCALE, 1.0).astype(F32)
            o_ref[:, c * LANES:(c + 1) * LANES] = y.astype(o_ref.dtype)

        @pl.when(col0 >= d_qk)
        def _(x=x, c=c):
            o_ref[:, c * LANES:(c + 1) * LANES] = x.astype(o_ref.dtype)


def modmm(x, sh, sc, w, rows_per_seg, rope=None, rope_tabs=None):
    t, d = x.shape
    n = w.shape[1]
    tm = _tile(rows_per_seg, 1024)
    tn = _tile(n, 1024)
    seg = lambda i, j: ((i * tm) // rows_per_seg, 0, 0)
    in_specs = [pl.BlockSpec((tm, d), lambda i, j: (i, 0)),
                pl.BlockSpec((1, 1, d), seg),
                pl.BlockSpec((1, 1, d), seg),
                pl.BlockSpec((d, tn), lambda i, j: (0, j))]
    args = [x, sh, sc, w]
    if rope is not None:
        nb = rows_per_seg // tm
        in_specs += [pl.BlockSpec((tm, LANES), lambda i, j: (i % nb, 0))] * 3
        args += list(rope_tabs)
    return pl.pallas_call(
        functools.partial(_modmm_kernel, rope=rope),
        out_shape=jax.ShapeDtypeStruct((t, n), BF16),
        grid=(t // tm, n // tn),
        in_specs=in_specs,
        out_specs=pl.BlockSpec((tm, tn), lambda i, j: (i, j)),
        scratch_shapes=[pltpu.VMEM((tm, d), BF16)],
        compiler_params=_params("arbitrary", "arbitrary"),
        name="modmm",
    )(*args)


HALO = 16


def _convmm_kernel(x_ref, xp_ref, xn_ref, sh_ref, sc_ref, w_ref, cw_ref, cb_ref, o_ref, h_ref, *,
                   seq_len):
    tm = x_ref.shape[0]

    @pl.when(pl.program_id(1) == 0)
    def _():
        sh, sc = sh_ref[0], sc_ref[0]
        h_ref[0:tm] = _modulated_norm(x_ref[...], sh, sc).astype(BF16)
        h_ref[tm:tm + HALO] = _modulated_norm(xp_ref[...], sh, sc).astype(BF16)
        h_ref[tm + HALO:tm + 2 * HALO] = _modulated_norm(xn_ref[...], sh, sc).astype(BF16)

    acc = jnp.dot(h_ref[...], w_ref[...], preferred_element_type=F32)
    u = acc[:tm]
    row = lax.broadcasted_iota(jnp.int32, u.shape, 0)
    pos = (pl.program_id(0) * tm + row) % seq_len
    prev = jnp.where(row == 0, acc[tm + HALO - 1:tm + HALO], pltpu.roll(u, 1, 0))
    prev = jnp.where(pos == 0, 0.0, prev)
    nxt = jnp.where(row == tm - 1, acc[tm + HALO:tm + HALO + 1], pltpu.roll(u, tm - 1, 0))
    nxt = jnp.where(pos == seq_len - 1, 0.0, nxt)
    y = prev * cw_ref[0:1, :] + u * cw_ref[1:2, :] + nxt * cw_ref[2:3, :] + cb_ref[...]
    o_ref[...] = y.astype(o_ref.dtype)


def modmm_conv(x, sh, sc, w, conv_w, conv_b, rows_per_seg, seq_len):
    t, d = x.shape
    n = w.shape[1]
    tm = _tile(rows_per_seg, 1024)
    tn = _tile(n, 1024)
    per = tm // HALO
    seg = lambda i, j: ((i * tm) // rows_per_seg, 0, 0)
    return pl.pallas_call(
        functools.partial(_convmm_kernel, seq_len=seq_len),
        out_shape=jax.ShapeDtypeStruct((t, n), BF16),
        grid=(t // tm, n // tn),
        in_specs=[pl.BlockSpec((tm, d), lambda i, j: (i, 0)),
                  pl.BlockSpec((HALO, d), lambda i, j: (jnp.maximum(i * per - 1, 0), 0)),
                  pl.BlockSpec((HALO, d), lambda i, j: (jnp.minimum((i + 1) * per, t // HALO - 1), 0)),
                  pl.BlockSpec((1, 1, d), seg),
                  pl.BlockSpec((1, 1, d), seg),
                  pl.BlockSpec((d, tn), lambda i, j: (0, j)),
                  pl.BlockSpec((3, tn), lambda i, j: (0, j)),
                  pl.BlockSpec((1, tn), lambda i, j: (0, j))],
        out_specs=pl.BlockSpec((tm, tn), lambda i, j: (i, j)),
        scratch_shapes=[pltpu.VMEM((tm + 2 * HALO, d), BF16)],
        compiler_params=_params("arbitrary", "arbitrary"),
        name="modmm_conv",
    )(x, x, x, sh, sc, w, conv_w, conv_b.reshape(1, n))


def _mmres_kernel(a_ref, w_ref, r_ref, g_ref, o_ref):
    acc = jnp.dot(a_ref[...], w_ref[...], preferred_element_type=F32)
    o_ref[...] = r_ref[...] + g_ref[0] * acc


def mm_res(a, w, res, gate, rows_per_seg):
    t, k = a.shape
    n = w.shape[1]
    tm = _tile(rows_per_seg, 1024)
    tn = _tile(n, 1024)
    return pl.pallas_call(
        _mmres_kernel,
        out_shape=jax.ShapeDtypeStruct((t, n), F32),
        grid=(t // tm, n // tn),
        in_specs=[pl.BlockSpec((tm, k), lambda i, j: (i, 0)),
                  pl.BlockSpec((k, tn), lambda i, j: (0, j)),
                  pl.BlockSpec((tm, tn), lambda i, j: (i, j)),
                  pl.BlockSpec((1, 1, tn), lambda i, j: ((i * tm) // rows_per_seg, 0, j))],
        out_specs=pl.BlockSpec((tm, tn), lambda i, j: (i, j)),
        compiler_params=_params("arbitrary", "arbitrary"),
        name="mm_res",
    )(a, w, res, gate)


def _filter_mlp_kernel(feat_ref, w1_ref, b1_ref, f1_ref, w2_ref, b2_ref, f2_ref, h_ref):
    h = jnp.sin(f1_ref[...] * (jnp.dot(feat_ref[...], w1_ref[...], precision=HIGHEST,
                                       preferred_element_type=F32) + b1_ref[...]))
    h_ref[...] = jnp.sin(f2_ref[...] * (jnp.dot(h, w2_ref[...], precision=HIGHEST,
                                                preferred_element_type=F32) + b2_ref[...]))


def _filter_kernel(hf_ref, hb_ref, tf_ref, tb_ref, w3f_ref, w3b_ref, dl_ref, p_ref, q_ref, *,
                   even_odd):
    dl = jnp.abs(dl_ref[...])
    hf = jnp.dot(hf_ref[...], w3f_ref[...], precision=HIGHEST,
                 preferred_element_type=F32) * jnp.exp(-tf_ref[...] * dl)
    hb = jnp.dot(hb_ref[...], w3b_ref[...], precision=HIGHEST,
                 preferred_element_type=F32) * jnp.exp(-tb_ref[...] * dl)
    row = lax.broadcasted_iota(jnp.int32, hf.shape, 0)
    hb = jnp.where(row == 0, 0.0, hb)
    s = lax.rsqrt(jnp.sum(hf * hf + hb * hb, axis=0, keepdims=True) + EPS)
    p, q = ((hf + hb) * s, (hf - hb) * s) if even_odd else (hf * s, hb * s)
    p_ref[0] = p.astype(p_ref.dtype).reshape(p_ref.shape[1:])
    q_ref[0] = q.astype(q_ref.dtype).reshape(q_ref.shape[1:])


def hyena_filter_taps(seq_len, w1, b1, f1, w2, b2, f2, w3, d, even_odd):
    t = jnp.linspace(0.0, 1.0, seq_len, dtype=F32)[:, None]
    omega = (2.0 * math.pi / seq_len) * jnp.arange(seq_len, dtype=F32)[:, None]
    bands = jnp.linspace(1e-4, HY_BANDS - 1, HY_BANDS, dtype=F32)[None, :]
    feats = jnp.concatenate([t, jnp.cos(bands * omega), -jnp.sin(bands * omega),
                             jnp.zeros((seq_len, LANES - HY_EMB), F32)], axis=-1)
    if not even_odd:
        feats = jnp.concatenate([feats, feats[:1], feats[:0:-1]], axis=0)
    w1p = jnp.concatenate([w1, jnp.zeros((LANES - HY_EMB, w1.shape[1]), F32)], axis=0)
    fw = w1.shape[1]
    rows = feats.shape[0]
    vec = lambda a: a.reshape(1, fw)
    h = pl.pallas_call(
        _filter_mlp_kernel,
        out_shape=jax.ShapeDtypeStruct((rows, fw), F32),
        compiler_params=_params(),
        name="hyena_filter_mlp",
    )(feats, w1p, vec(b1), vec(f1), w2, vec(b2), vec(f2))
    tcol = feats[:, 0:1]
    max_decay = math.log(HY_DECAY_TARGET) / HY_FAST_DECAY_PCT
    min_decay = math.log(HY_DECAY_TARGET) / HY_SLOW_DECAY_PCT
    deltas = jnp.linspace(min_decay, max_decay, d, dtype=F32).reshape(1, d)
    tn = _tile(d, 256)
    nb = d // tn
    bsel = 0 if even_odd else 1
    if even_odd:
        oshape, oblock = (2, seq_len, d), (1, seq_len, tn)
        omap = lambda o, j: (o, 0, j)
    else:
        n2 = 2 * seq_len // FFT_N1
        oshape, oblock = (2, seq_len // n2, n2, d), (1, seq_len // n2, n2, tn)
        omap = lambda o, j: (o, 0, 0, j)
    out = jax.ShapeDtypeStruct(oshape, BF16)
    return pl.pallas_call(
        functools.partial(_filter_kernel, even_odd=even_odd),
        out_shape=(out, out),
        grid=(2, nb),
        in_specs=[pl.BlockSpec((seq_len, fw), lambda o, j: (0, 0)),
                  pl.BlockSpec((seq_len, fw), lambda o, j: (bsel, 0)),
                  pl.BlockSpec((seq_len, 1), lambda o, j: (0, 0)),
                  pl.BlockSpec((seq_len, 1), lambda o, j: (bsel, 0)),
                  pl.BlockSpec((fw, tn), lambda o, j: (0, (2 * o) * nb + j)),
                  pl.BlockSpec((fw, tn), lambda o, j: (0, (2 * o + 1) * nb + j)),
                  pl.BlockSpec((1, tn), lambda o, j: (0, j))],
        out_specs=(pl.BlockSpec(oblock, omap), pl.BlockSpec(oblock, omap)),
        compiler_params=_params("arbitrary", "arbitrary"),
        name="hyena_filter",
    )(h, h, tcol, tcol, w3, w3, deltas)


def _alt_sum(x):
    row = lax.broadcasted_iota(jnp.int32, x.shape, 0)
    return jnp.sum(jnp.where((row & 1) == 1, -x, x), axis=0, keepdims=True)


def _spectrum_kernel(c_ref, s_ref, e_ref, o_ref, kr_ref, km_ref):
    kr_ref[0] = jnp.dot(c_ref[...], e_ref[0], preferred_element_type=F32)
    km = jnp.dot(s_ref[...], o_ref[0], preferred_element_type=F32)

    @pl.when(pl.program_id(2) != 0)
    def _():
        km_ref[0] = km

    @pl.when(pl.program_id(2) == 0)
    def _():
        row = lax.broadcasted_iota(jnp.int32, km.shape, 0)
        km_ref[0] = jnp.where(row == 0, _alt_sum(e_ref[0].astype(F32)), km)


def filter_spectrum(cmat, smat, e_taps, o_taps):
    _, n, d = e_taps.shape
    tk = _tile(n, 512)
    tn = _tile(d, 512)
    out = jax.ShapeDtypeStruct((2, n, d), F32)
    return pl.pallas_call(
        _spectrum_kernel,
        out_shape=(out, out),
        grid=(2, d // tn, n // tk),
        in_specs=[pl.BlockSpec((tk, n), lambda o, j, k: (k, 0)),
                  pl.BlockSpec((tk, n), lambda o, j, k: (k, 0)),
                  pl.BlockSpec((1, n, tn), lambda o, j, k: (o, 0, j)),
                  pl.BlockSpec((1, n, tn), lambda o, j, k: (o, 0, j))],
        out_specs=(pl.BlockSpec((1, tk, tn), lambda o, j, k: (o, k, j)),
                   pl.BlockSpec((1, tk, tn), lambda o, j, k: (o, k, j))),
        compiler_params=_params("arbitrary", "arbitrary", "arbitrary"),
        name="filter_spectrum",
    )(cmat, smat, e_taps, o_taps)


def _dft_fwd_kernel(c_ref, s_ref, u_ref, kr_ref, km_ref, yr_ref, q_ref):
    u = u_ref[...]
    a = jnp.dot(c_ref[...], u, preferred_element_type=F32)
    b = jnp.dot(s_ref[...], u, preferred_element_type=F32)
    kr = kr_ref[0]
    km = km_ref[0]
    yr = a * kr - b * km
    q = b * kr + a * km

    @pl.when(pl.program_id(2) != 0)
    def _():
        yr_ref[...] = yr.astype(yr_ref.dtype)
        q_ref[...] = q.astype(q_ref.dtype)

    @pl.when(pl.program_id(2) == 0)
    def _():
        row = lax.broadcasted_iota(jnp.int32, yr.shape, 0)
        ynyq = _alt_sum(u.astype(F32)) * km[0:1, :]
        yr_ref[...] = jnp.where(row == 0, 0.5 * yr, yr).astype(yr_ref.dtype)
        q_ref[...] = jnp.where(row == 0, 0.5 * ynyq, q).astype(q_ref.dtype)


def dft_fwd(cmat, smat, u, col0, kr, km, order, seq_len, d):
    t = u.shape[0]
    tk = _tile(seq_len, 512)
    tn = _tile(d, 512)
    cb = col0 // tn
    out = jax.ShapeDtypeStruct((t, d), BF16)
    nk = seq_len // tk
    return pl.pallas_call(
        _dft_fwd_kernel,
        out_shape=(out, out),
        grid=(t // seq_len, d // tn, nk),
        in_specs=[pl.BlockSpec((tk, seq_len), lambda b, j, k: (k, 0)),
                  pl.BlockSpec((tk, seq_len), lambda b, j, k: (k, 0)),
                  pl.BlockSpec((seq_len, tn), lambda b, j, k: (b, cb + j)),
                  pl.BlockSpec((1, tk, tn), lambda b, j, k: (order, k, j)),
                  pl.BlockSpec((1, tk, tn), lambda b, j, k: (order, k, j))],
        out_specs=(pl.BlockSpec((tk, tn), lambda b, j, k: (b * nk + k, j)),
                   pl.BlockSpec((tk, tn), lambda b, j, k: (b * nk + k, j))),
        compiler_params=_params("arbitrary", "arbitrary", "arbitrary"),
        name="dft_fwd",
    )(cmat, smat, u, kr, km)


def _dft_inv_kernel(c_ref, s_ref, yr_ref, q_ref, u_ref, g_ref, bias_ref, o_ref, *, seq_len):
    acc = jnp.dot(c_ref[...], yr_ref[...], preferred_element_type=F32)
    acc += jnp.dot(s_ref[...], q_ref[...], preferred_element_type=F32)
    tt = acc.shape[0]
    t_idx = pl.program_id(2) * tt + lax.broadcasted_iota(jnp.int32, acc.shape, 0)
    half_nyq = q_ref[0:1, :].astype(F32)
    acc += jnp.where((t_idx & 1) == 1, -half_nyq, half_nyq)
    u = u_ref[...].astype(F32)
    y = acc * (1.0 / seq_len) + u * bias_ref[0]
    o_ref[...] = (g_ref[...].astype(F32) * y).astype(o_ref.dtype)


def dft_inv(cmat, smat, yr, q, u, ucol0, gsrc, gcol0, bias, order, seq_len, d):
    t = yr.shape[0]
    tt = _tile(seq_len, 512)
    tn = _tile(d, 512)
    ub, gb = ucol0 // tn, gcol0 // tn
    nt = seq_len // tt
    return pl.pallas_call(
        functools.partial(_dft_inv_kernel, seq_len=seq_len),
        out_shape=jax.ShapeDtypeStruct((t, d), BF16),
        grid=(t // seq_len, d // tn, nt),
        in_specs=[pl.BlockSpec((tt, seq_len), lambda b, j, k: (k, 0)),
                  pl.BlockSpec((tt, seq_len), lambda b, j, k: (k, 0)),
                  pl.BlockSpec((seq_len, tn), lambda b, j, k: (b, j)),
                  pl.BlockSpec((seq_len, tn), lambda b, j, k: (b, j)),
                  pl.BlockSpec((tt, tn), lambda b, j, k: (b * nt + k, ub + j)),
                  pl.BlockSpec((tt, tn), lambda b, j, k: (b * nt + k, gb + j)),
                  pl.BlockSpec((1, 1, tn), lambda b, j, k: (order, 0, j))],
        out_specs=pl.BlockSpec((tt, tn), lambda b, j, k: (b * nt + k, j)),
        compiler_params=_params("arbitrary", "arbitrary", "arbitrary"),
        name="dft_inv",
    )(cmat, smat, yr, q, u, gsrc, bias)


def dft_tables(seq_len):
    n = jnp.arange(seq_len, dtype=jnp.int32)
    ang = ((n[:, None] * n[None, :]) % (2 * seq_len)).astype(F32) * (math.pi / seq_len)
    return jnp.cos(ang).astype(BF16), jnp.sin(ang).astype(BF16)


def hyena_long_convs(u3, tables, spectrum, bias, seq_len, d):
    cmat, smat = tables
    kr, km = spectrum
    yr, q = dft_fwd(cmat, smat, u3, 0, kr, km, 0, seq_len, d)
    z = dft_inv(cmat, smat, yr, q, u3, 0, u3, d, bias, 0, seq_len, d)
    yr, q = dft_fwd(cmat, smat, z, 0, kr, km, 1, seq_len, d)
    return dft_inv(cmat, smat, yr, q, z, 0, u3, 2 * d, bias, 1, seq_len, d)


FFT_N1 = 64
SUB = 8


def fft_tables(seq_len):
    n = 2 * seq_len
    n1, n2 = FFT_N1, n // FFT_N1
    two_pi = 2.0 * math.pi
    eye = jnp.eye(SUB, dtype=F32)
    k1 = jnp.arange(n1, dtype=jnp.int32)
    h1 = jnp.arange(n1 // 2, dtype=jnp.int32)
    ang = ((k1[:, None] * h1[None, :]) % n1).astype(F32) * (two_pi / n1)
    ca = jnp.stack([jnp.cos(ang), -jnp.sin(ang)], axis=1)
    ka = jnp.einsum('krn,jm->krjnm', ca, eye).reshape(n1 * 2 * SUB, (n1 // 2) * SUB)
    c2 = jnp.stack([jnp.cos(ang.T), -jnp.sin(ang.T)], axis=2)
    ka2 = jnp.einsum('tkr,jm->tjkrm', c2, eye).reshape((n1 // 2) * SUB, n1 * 2 * SUB)
    k2 = jnp.arange(n2 // 2, dtype=jnp.int32)
    m2 = jnp.arange(n2, dtype=jnp.int32)
    k = k1[:, None, None] + n1 * k2[None, :, None]
    th = ((k * m2[None, None, :]) % n).astype(F32) * (two_pi / n)
    c, s = jnp.cos(th), jnp.sin(th)
    fb = jnp.concatenate([jnp.concatenate([c, s], axis=2),
                          jnp.concatenate([-s, c], axis=2)], axis=1)
    ct, st = jnp.swapaxes(c, 1, 2), jnp.swapaxes(s, 1, 2)
    gb = jnp.concatenate([jnp.concatenate([ct, -st], axis=2),
                          jnp.concatenate([st, ct], axis=2)], axis=1)
    return tuple(a.astype(BF16) for a in (ka, fb, gb, ka2))


def _fft_stage_a(src_ref, ka_ref, a_ref):
    n1h, n2, tn = src_ref.shape
    for g in range(n2 // SUB):
        rhs = src_ref[:, g * SUB:(g + 1) * SUB, :].reshape(n1h * SUB, tn).astype(BF16)
        out = jnp.dot(ka_ref[...], rhs, preferred_element_type=F32)
        a_ref[:, :, g * SUB:(g + 1) * SUB, :] = out.reshape(FFT_N1, 2, SUB, tn)


def _fft_stage_b(fb_ref, a_ref, k1):
    _, _, n2, tn = a_ref.shape
    slab = a_ref[k1].reshape(2 * n2, tn).astype(BF16)
    return jnp.dot(fb_ref[k1], slab, preferred_element_type=F32)


def _block_alt_sum(src_ref):
    return _alt_sum(jnp.sum(src_ref[...], axis=0))


def _fftspec_kernel(tf_ref, tb_ref, ka_ref, fb_ref, kf_ref, knyq_ref, src_ref, a_ref):
    h = a_ref.shape[2] // 2
    tn = a_ref.shape[3]
    nyq = None
    for which, t_ref in enumerate((tf_ref, tb_ref)):
        src_ref[...] = t_ref[0].astype(F32)
        part = _block_alt_sum(src_ref)
        nyq = part if nyq is None else nyq + part
        _fft_stage_a(src_ref, ka_ref, a_ref)

        def body(k1, carry, which=which):
            x = _fft_stage_b(fb_ref, a_ref, k1).reshape(2, h, tn)
            if which == 0:
                kf_ref[0, k1] = x
            else:
                sign = 1.0 - 2.0 * jnp.asarray(k1 & 1, F32)
                kf_ref[0, k1] = kf_ref[0, k1] + sign * x
            return carry

        lax.fori_loop(0, FFT_N1, body, 0, unroll=4)
    knyq_ref[0] = nyq
    dc = kf_ref[0, 0]
    row = lax.broadcasted_iota(jnp.int32, dc.shape, 1)
    kf_ref[0, 0] = jnp.where(row == 0, 0.5 * dc, dc)


def _const_spec(shape):
    return pl.BlockSpec(shape, lambda *_: (0,) * len(shape), pipeline_mode=pl.Buffered(1))


def fft_filter_spectrum(tabs, taps_f, taps_b):
    ka, fb, _, _ = tabs
    _, n1h, n2, d = taps_f.shape
    tn = _tile(d, 256)
    tap = pl.BlockSpec((1, n1h, n2, tn), lambda o, j: (o, 0, 0, j))
    return pl.pallas_call(
        _fftspec_kernel,
        out_shape=(jax.ShapeDtypeStruct((2, FFT_N1, 2, n2 // 2, d), F32),
                   jax.ShapeDtypeStruct((2, 1, d), F32)),
        grid=(2, d // tn),
        in_specs=[tap, tap, _const_spec(ka.shape), _const_spec(fb.shape)],
        out_specs=(pl.BlockSpec((1, FFT_N1, 2, n2 // 2, tn), lambda o, j: (o, 0, 0, 0, j)),
                   pl.BlockSpec((1, 1, tn), lambda o, j: (o, 0, j))),
        scratch_shapes=[pltpu.VMEM((n1h, n2, tn), F32), pltpu.VMEM((FFT_N1, 2, n2, tn), F32)],
        compiler_params=_params("arbitrary", "arbitrary"),
        name="fft_filter_spectrum",
    )(taps_f, taps_b, ka, fb)


def _fftconv_kernel(u_ref, g_ref, bias_ref, kf_ref, knyq_ref, ka_ref, fb_ref, gb_ref, ka2_ref,
                    o_ref, src_ref, a_ref):
    n1h, n2, tn = src_ref.shape
    h = n2 // 2
    seq_len = n1h * n2
    src_ref[...] = u_ref[...].astype(F32)
    ynyq = _block_alt_sum(src_ref) * knyq_ref[0] * (0.5 / seq_len)
    _fft_stage_a(src_ref, ka_ref, a_ref)

    def body(k1, carry):
        x = _fft_stage_b(fb_ref, a_ref, k1)
        xr, xi = x[:h], x[h:]
        kr, ki = kf_ref[0, k1, 0], kf_ref[0, k1, 1]
        y = jnp.concatenate([xr * kr - xi * ki, xr * ki + xi * kr], axis=0).astype(BF16)
        z = jnp.dot(gb_ref[k1], y, preferred_element_type=F32)
        a_ref[k1] = z.reshape(2, n2, tn)
        return carry

    lax.fori_loop(0, FFT_N1, body, 0, unroll=True)
    bias = bias_ref[0]
    pair = 2 * SUB
    row = lax.broadcasted_iota(jnp.int32, (n1h, pair, tn), 1)
    nyq = jnp.where((row & 1) == 1, -ynyq, ynyq)
    for gg in range(n2 // pair):
        parts = []
        for g in (2 * gg, 2 * gg + 1):
            zg = a_ref[:, :, g * SUB:(g + 1) * SUB, :].reshape(FFT_N1 * 2 * SUB, tn).astype(BF16)
            acc = jnp.dot(ka2_ref[...], zg, preferred_element_type=F32)
            parts.append(acc.reshape(n1h, SUB, tn))
        acc = jnp.concatenate(parts, axis=1)
        rows = slice(gg * pair, (gg + 1) * pair)
        y = acc * (1.0 / seq_len) + nyq + src_ref[:, rows, :] * bias
        o_ref[:, rows, :] = (g_ref[:, rows, :].astype(F32) * y).astype(o_ref.dtype)


def fft_long_conv(tabs, spec, u, ucol0, gsrc, gcol0, bias, order, seq_len, d):
    ka, fb, gb, ka2 = tabs
    kf, knyq = spec
    nb, n2, _ = u.shape
    n1h = seq_len // n2
    tn = _tile(d, 256)
    ub, gb_ = ucol0 // tn, gcol0 // tn
    return pl.pallas_call(
        _fftconv_kernel,
        out_shape=jax.ShapeDtypeStruct((nb, n2, d), BF16),
        grid=(d // tn, nb // n1h),
        in_specs=[pl.BlockSpec((n1h, n2, tn), lambda j, b: (b, 0, ub + j)),
                  pl.BlockSpec((n1h, n2, tn), lambda j, b: (b, 0, gb_ + j)),
                  pl.BlockSpec((1, 1, tn), lambda j, b: (order, 0, j)),
                  pl.BlockSpec((1, FFT_N1, 2, n2 // 2, tn), lambda j, b: (order, 0, 0, 0, j),
                               pipeline_mode=pl.Buffered(1)),
                  pl.BlockSpec((1, 1, tn), lambda j, b: (order, 0, j)),
                  _const_spec(ka.shape), _const_spec(fb.shape), _const_spec(gb.shape),
                  _const_spec(ka2.shape)],
        out_specs=pl.BlockSpec((n1h, n2, tn), lambda j, b: (b, 0, j)),
        scratch_shapes=[pltpu.VMEM((n1h, n2, tn), F32), pltpu.VMEM((FFT_N1, 2, n2, tn), F32)],
        compiler_params=_params("arbitrary", "arbitrary"),
        name="fft_long_conv",
    )(u, gsrc, bias, kf, knyq, ka, fb, gb, ka2)


def hyena_long_convs_fft(u3, tabs, spec, bias, seq_len, d):
    t = u3.shape[0]
    n2 = 2 * seq_len // FFT_N1
    u3b = u3.reshape(t // n2, n2, 3 * d)
    z = fft_long_conv(tabs, spec, u3b, 0, u3b, d, bias, 0, seq_len, d)
    z = fft_long_conv(tabs, spec, z, 0, u3b, 2 * d, bias, 1, seq_len, d)
    return z.reshape(t, d)


def rope_tables(seq_len):
    pos = jnp.arange(seq_len, dtype=jnp.int32)
    r = (pos // GRID_W).astype(F32)[:, None]
    col = (pos % GRID_W).astype(F32)[:, None]
    inv = ROPE_BASE ** (-(2.0 * jnp.arange(ROPE_PAIRS, dtype=F32)) / (2 * ROPE_PAIRS))
    ar, ac = r * inv, col * inv
    zeros = jnp.zeros_like(ar)
    cos = jnp.concatenate([jnp.cos(ar), jnp.cos(ar), jnp.cos(ac), jnp.cos(ac)], axis=1)
    sa = jnp.concatenate([-jnp.sin(ar), zeros, -jnp.sin(ac), zeros], axis=1)
    sb = jnp.concatenate([zeros, jnp.sin(ar), zeros, jnp.sin(ac)], axis=1)
    rep = LANES // HEAD_DIM
    return tuple(jnp.tile(a, (1, rep)) for a in (cos, sa, sb))


LOG2E = 1.4426950408889634
Q_SCALE = HEAD_DIM ** -0.5 * LOG2E


def _attend(q_ref, sink_ref, keys, vals, bias, o_ref, n_heads, q_per_kv, q_scale=None):
    nq = q_ref.shape[0]
    s_len = keys.shape[0]
    lane = lax.broadcasted_iota(jnp.int32, (s_len, HEAD_DIM), 1)
    ones_col = jnp.where(lane == 0, 1.0, 0.0).astype(BF16)
    hidx = lax.broadcasted_iota(jnp.int32, (q_per_kv, 1, 1), 0)
    pending = []
    for g in range(n_heads // q_per_kv):
        qs = []
        sink = jnp.zeros((q_per_kv, 1, 1), F32)
        for h in range(q_per_kv):
            col = (g * q_per_kv + h) * HEAD_DIM
            q = q_ref[:, col:col + HEAD_DIM]
            if q_scale is not None:
                q = (q.astype(F32) * q_scale).astype(BF16)
            qs.append(q)
            sink = jnp.where(hidx == h, sink_ref[g * q_per_kv + h] * LOG2E, sink)
        qg = jnp.concatenate(qs, axis=0)
        kg = keys[:, g * HEAD_DIM:(g + 1) * HEAD_DIM]
        vg = jnp.concatenate([vals[:, g * HEAD_DIM:(g + 1) * HEAD_DIM], ones_col], axis=1)
        s = lax.dot_general(qg, kg, (((1,), (1,)), ((), ())), preferred_element_type=F32)
        s = s.reshape(q_per_kv, nq, s_len)
        if bias is not None:
            s = s + bias[None]
        m = jnp.maximum(jnp.max(s, axis=-1, keepdims=True), sink)
        p = jnp.exp2(s - m).astype(BF16).reshape(q_per_kv * nq, s_len)
        acc = jnp.dot(p, vg, preferred_element_type=F32).reshape(q_per_kv, nq, 2 * HEAD_DIM)
        denom = acc[:, :, HEAD_DIM:HEAD_DIM + 1] + jnp.exp2(sink - m)
        out = acc[:, :, :HEAD_DIM] / denom
        for h in range(q_per_kv):
            pending.append(out[h])
            if len(pending) == LANES // HEAD_DIM:
                lo = (g * q_per_kv + h + 1) * HEAD_DIM - LANES
                o_ref[:, lo:lo + LANES] = jnp.concatenate(pending, axis=1).astype(o_ref.dtype)
                pending = []


def _attn_kernel(sink_ref, q_ref, k_ref, v_ref, kc_ref, vc_ref, o_ref, *, seq_len, n_heads, q_per_kv):
    span = Q_BLOCK + 2 * WINDOW
    ctx_len = kc_ref.shape[0]
    start = pl.program_id(1) * Q_BLOCK
    kstart = pl.multiple_of(jnp.clip(start - WINDOW, 0, seq_len - span), Q_BLOCK)
    qpos = start + lax.broadcasted_iota(jnp.int32, (Q_BLOCK, span + ctx_len), 0)
    col = lax.broadcasted_iota(jnp.int32, (Q_BLOCK, span + ctx_len), 1)
    visible = (jnp.abs(qpos - (kstart + col)) <= WINDOW) | (col >= span)
    bias = jnp.where(visible, 0.0, -1e30).astype(F32)
    keys = jnp.concatenate([k_ref[pl.ds(kstart, span), :], kc_ref[...]], axis=0)
    vals = jnp.concatenate([v_ref[pl.ds(kstart, span), :], vc_ref[...]], axis=0)
    _attend(q_ref, sink_ref, keys, vals, bias, o_ref, n_heads, q_per_kv)


def window_attention(qkv, qkv_ctx, sink, batch, seq_len, ctx_len, d_q, d_kv):
    n_heads = d_q // HEAD_DIM
    nqb = seq_len // Q_BLOCK
    kcol, vcol = d_q // d_kv, d_q // d_kv + 1
    return pl.pallas_call(
        functools.partial(_attn_kernel, seq_len=seq_len, n_heads=n_heads,
                          q_per_kv=n_heads // N_KV_HEADS),
        out_shape=jax.ShapeDtypeStruct((batch * seq_len, d_q), BF16),
        grid=(batch, nqb),
        in_specs=[pl.BlockSpec(memory_space=pltpu.SMEM),
                  pl.BlockSpec((Q_BLOCK, d_q), lambda b, i: (b * nqb + i, 0)),
                  pl.BlockSpec((seq_len, d_kv), lambda b, i: (b, kcol)),
                  pl.BlockSpec((seq_len, d_kv), lambda b, i: (b, vcol)),
                  pl.BlockSpec((ctx_len, d_kv), lambda b, i: (b, kcol)),
                  pl.BlockSpec((ctx_len, d_kv), lambda b, i: (b, vcol))],
        out_specs=pl.BlockSpec((Q_BLOCK, d_q), lambda b, i: (b * nqb + i, 0)),
        compiler_params=_params("arbitrary", "arbitrary"),
        name="window_attention",
    )(sink, qkv, qkv, qkv, qkv_ctx, qkv_ctx)


def _ctx_attn_kernel(sink_ref, q_ref, kc_ref, vc_ref, o_ref, *, n_heads, q_per_kv, scale):
    _attend(q_ref, sink_ref, kc_ref[...], vc_ref[...], None, o_ref, n_heads, q_per_kv,
            q_scale=scale)


def ctx_attention(qkv_ctx, sink, batch, ctx_len, d_q, d_kv):
    n_heads = d_q // HEAD_DIM
    kcol, vcol = d_q // d_kv, d_q // d_kv + 1
    return pl.pallas_call(
        functools.partial(_ctx_attn_kernel, n_heads=n_heads, q_per_kv=n_heads // N_KV_HEADS,
                          scale=Q_SCALE),
        out_shape=jax.ShapeDtypeStruct((batch * ctx_len, d_q), BF16),
        grid=(batch,),
        in_specs=[pl.BlockSpec(memory_space=pltpu.SMEM),
                  pl.BlockSpec((ctx_len, d_q), lambda b: (b, 0)),
                  pl.BlockSpec((ctx_len, d_kv), lambda b: (b, kcol)),
                  pl.BlockSpec((ctx_len, d_kv), lambda b: (b, vcol))],
        out_specs=pl.BlockSpec((ctx_len, d_q), lambda b: (b, 0)),
        compiler_params=_params("arbitrary"),
        name="ctx_attention",
    )(sink, qkv_ctx, qkv_ctx, qkv_ctx)


def _router_kernel(x_ref, sh_ref, sc_ref, wr_ref, br_ref, cnt0_ref, route_ref, cnt_ref, carry_ref):
    @pl.when(pl.program_id(0) == 0)
    def _():
        carry_ref[...] = cnt0_ref[...]

    h = _modulated_norm(x_ref[...], sh_ref[0], sc_ref[0])
    h_hi = h.astype(BF16)
    h_lo = (h - h_hi.astype(F32)).astype(BF16)
    logits = (jnp.dot(h_hi, wr_ref[0], preferred_element_type=F32)
              + (jnp.dot(h_hi, wr_ref[1], preferred_element_type=F32)
                 + jnp.dot(h_lo, wr_ref[0], preferred_element_type=F32))) + br_ref[...]
    tm = logits.shape[0]
    lane_i = lax.broadcasted_iota(jnp.int32, logits.shape, 1)
    lane = lane_i.astype(F32)
    neg = -jnp.inf
    lg = jnp.where(lane_i < N_GROUPS, logits, neg)
    mg = jnp.max(lg, axis=-1, keepdims=True)
    p_g = 1.0 / jnp.sum(jnp.exp(lg - mg), axis=-1, keepdims=True)
    g_sel = jnp.min(jnp.where(lg == mg, lane, float(ROUTE_W)), axis=-1, keepdims=True)
    lo = N_GROUPS + g_sel * EXPERTS_PER_GROUP
    lf = jnp.where((lane >= lo) & (lane < lo + EXPERTS_PER_GROUP), logits, neg)
    m1 = jnp.max(lf, axis=-1, keepdims=True)
    i1 = jnp.min(jnp.where(lf == m1, lane, float(ROUTE_W)), axis=-1, keepdims=True)
    lf2 = jnp.where(lane == i1, neg, lf)
    m2 = jnp.max(lf2, axis=-1, keepdims=True)
    i2 = jnp.min(jnp.where(lf2 == m2, lane, float(ROUTE_W)), axis=-1, keepdims=True)
    a2 = jnp.exp(m2 - m1)
    gate1 = p_g / (1.0 + a2)
    gate2 = gate1 * a2
    sel1 = lane == i1
    sel2 = lane == i2
    onehot = jnp.where(sel1 | sel2, 1.0, 0.0)
    r_i = lax.broadcasted_iota(jnp.int32, (tm, tm), 0)
    c_i = lax.broadcasted_iota(jnp.int32, (tm, tm), 1)
    tri = jnp.where(r_i > c_i, 1.0, 0.0).astype(BF16)
    before = jnp.dot(tri, onehot.astype(BF16), preferred_element_type=F32) + carry_ref[...]
    rank1 = jnp.sum(jnp.where(sel1, before, 0.0), axis=-1, keepdims=True)
    rank2 = jnp.sum(jnp.where(sel2, before, 0.0), axis=-1, keepdims=True)
    carry_ref[...] += jnp.sum(onehot, axis=0, keepdims=True)
    cnt_ref[...] = carry_ref[...]
    vals = (i1 - N_GROUPS, i2 - N_GROUPS, rank1, rank2, gate1, gate2)
    route = jnp.zeros_like(logits)
    for idx, v in enumerate(vals):
        route = jnp.where(lane_i == idx, v, route)
    route_ref[...] = route


def router(x, sh, sc, wr, br, cnt0, rows_per_seg):
    t, d = x.shape
    tm = _tile(rows_per_seg, 512)
    seg = lambda i: ((i * tm) // rows_per_seg, 0, 0)
    wr_hi = wr.astype(BF16)
    wr = jnp.stack([wr_hi, (wr - wr_hi.astype(F32)).astype(BF16)])
    return pl.pallas_call(
        _router_kernel,
        out_shape=(jax.ShapeDtypeStruct((t, ROUTE_W), F32), jax.ShapeDtypeStruct((1, ROUTE_W), F32)),
        grid=(t // tm,),
        in_specs=[pl.BlockSpec((tm, d), lambda i: (i, 0)),
                  pl.BlockSpec((1, 1, d), seg), pl.BlockSpec((1, 1, d), seg),
                  pl.BlockSpec((2, d, ROUTE_W), lambda i: (0, 0, 0)),
                  pl.BlockSpec((1, ROUTE_W), lambda i: (0, 0)),
                  pl.BlockSpec((1, ROUTE_W), lambda i: (0, 0))],
        out_specs=(pl.BlockSpec((tm, ROUTE_W), lambda i: (i, 0)),
                   pl.BlockSpec((1, ROUTE_W), lambda i: (0, 0))),
        scratch_shapes=[pltpu.VMEM((1, ROUTE_W), F32)],
        compiler_params=_params("arbitrary"),
        name="router",
    )(x, sh, sc, wr, br, cnt0)


def _slot_rows(d):
    del d
    return 1


def _token_copy(src, src_tok, dst, dst_tok, sem, rows):
    return pltpu.make_async_copy(src.at[pl.ds(src_tok * rows, rows)],
                                 dst.at[pl.ds(dst_tok * rows, rows)], sem)


TOKENS_PER_TRIP = 2


def _start_copy(copy, k):
    copy.start(priority=k)


def _wait_copy(copy, k):
    del k
    copy.wait()


def _to_token_rows(mat, rows_ref):
    tm, d = mat.shape
    rows = _slot_rows(d)
    width = d // rows
    for s in range(rows):
        rows_ref[pl.ds(s, tm, stride=rows), :] = mat[:, s * width:(s + 1) * width]


def _token_cols(rows_ref, s, tm, rows):
    return rows_ref[pl.ds(s, tm, stride=rows), :]


SLOT_BITS = 20


def _slot(code_ref, pad_start_ref, idx):
    code = code_ref[idx]
    return pad_start_ref[code >> SLOT_BITS] + (code & ((1 << SLOT_BITS) - 1))


def _dispatch_kernel(code_ref, pad_start_ref, pad_end_ref, nused_ref, x_ref, sh_ref, sc_ref, *rest,
                     first, n_blocks):
    if first:
        xs_ref, h_ref, zero_ref, sem, zsem = rest
    else:
        _, xs_ref, h_ref, zero_ref, sem, zsem = rest
    tm, d = x_ref.shape
    rows = _slot_rows(d)
    width = d // rows
    base = pl.program_id(0) * tm

    if first:
        @pl.when(pl.program_id(0) == 0)
        def _():
            zero_ref[...] = jnp.zeros_like(zero_ref)
            blk_rows = MOE_BLOCK * rows

            def zero_block(blk):
                return pltpu.make_async_copy(zero_ref, xs_ref.at[pl.ds(blk * blk_rows, blk_rows)], zsem)

            def last_blocks(op):
                def body(e, carry):
                    hi = pad_end_ref[e]
                    lo = pad_start_ref[e]

                    @pl.when(hi > lo)
                    def _():
                        op(zero_block(hi // MOE_BLOCK - 1))
                    return carry
                lax.fori_loop(0, N_EXPERTS, body, 0)

            def tail_blocks(op):
                def body(blk, carry):
                    op(zero_block(blk))
                    return carry
                lax.fori_loop(nused_ref[0], n_blocks, body, 0)

            last_blocks(lambda c: c.start())
            tail_blocks(lambda c: c.start())
            last_blocks(lambda c: c.wait())
            tail_blocks(lambda c: c.wait())

    step = pl.program_id(0)
    slot = step % 2
    _to_token_rows(_modulated_norm(x_ref[...], sh_ref[0], sc_ref[0]), h_ref.at[slot])

    def copies(which, row0, op):
        def body(i, carry):
            for j in range(TOKENS_PER_TRIP):
                r = i * TOKENS_PER_TRIP + j
                for k in range(2):
                    op(_token_copy(h_ref.at[which], r, xs_ref,
                                   _slot(code_ref, pad_start_ref, 2 * (row0 + r) + k), sem.at[which],
                                   rows), k)
            return carry
        lax.fori_loop(0, tm // TOKENS_PER_TRIP, body, 0)

    copies(slot, base, _start_copy)

    @pl.when(step > 0)
    def _():
        copies(1 - slot, base - tm, _wait_copy)

    @pl.when(step == pl.num_programs(0) - 1)
    def _():
        copies(slot, base, _wait_copy)


def dispatch(code, pad_start, pad_end, nused, n_blocks, x, sh, sc, xs, rows_per_seg):
    t, d = x.shape
    rows = _slot_rows(d)
    width = d // rows
    tm = _tile(rows_per_seg, 256)
    first = xs is None
    seg = lambda i, *_: ((i * tm) // rows_per_seg, 0, 0)
    in_specs = [pl.BlockSpec((tm, d), lambda i, *_: (i, 0)),
                pl.BlockSpec((1, 1, d), seg), pl.BlockSpec((1, 1, d), seg)]
    args = [code, pad_start, pad_end, nused, x, sh, sc]
    if not first:
        in_specs.append(pl.BlockSpec(memory_space=pl.ANY))
        args.append(xs)
    return pl.pallas_call(
        functools.partial(_dispatch_kernel, first=first, n_blocks=n_blocks),
        out_shape=jax.ShapeDtypeStruct((n_blocks * MOE_BLOCK * rows, width), F32),
        grid_spec=pltpu.PrefetchScalarGridSpec(
            num_scalar_prefetch=4,
            grid=(t // tm,),
            in_specs=in_specs,
            out_specs=pl.BlockSpec(memory_space=pl.ANY),
            scratch_shapes=[pltpu.VMEM((2, tm * rows, width), F32),
                            pltpu.VMEM((MOE_BLOCK * rows, width), F32),
                            pltpu.SemaphoreType.DMA((2,)), pltpu.SemaphoreType.DMA]),
        input_output_aliases={} if first else {7: 0},
        compiler_params=_params("arbitrary"),
        name="moe_dispatch",
    )(*args)


def _expert_kernel(bexp_ref, nused_ref, xs_ref, w1_ref, w3_ref, w2_ref, ys_ref, xb, w13b, w2b):
    i = pl.program_id(0)
    e = bexp_ref[i]
    prev = bexp_ref[jnp.maximum(i - 1, 0)]
    tm, d = xb.shape
    rows = _slot_rows(d)
    width = d // rows
    de = w2b.shape[0]

    @pl.when((i == 0) | (e != prev))
    def _():
        w13b[:, :de] = w1_ref[0, 0].astype(BF16)
        w13b[:, de:] = w3_ref[0, 0].astype(BF16)
        w2b[...] = w2_ref[0, 0].astype(BF16)

    @pl.when(i < nused_ref[0])
    def _():
        for s in range(rows):
            xb[:, s * width:(s + 1) * width] = _token_cols(xs_ref, s, tm, rows).astype(BF16)
        h13 = jnp.dot(xb[...], w13b[...], preferred_element_type=F32)
        h1, h3 = h13[:, :de], h13[:, de:]
        a = (h1 * jax.nn.sigmoid(h1) * h3).astype(BF16)
        _to_token_rows(jnp.dot(a, w2b[...], preferred_element_type=F32), ys_ref)

    @pl.when(i >= nused_ref[0])
    def _():
        ys_ref[...] = jnp.zeros_like(ys_ref)


def experts(block_exp, nused, xs, w1, w3, w2, layer):
    d, de = w1.shape[-2:]
    width = d // _slot_rows(d)
    blk_rows = MOE_BLOCK * _slot_rows(d)
    blk = lambda i, bexp, nu: (jnp.minimum(i, nu[0] - 1), 0)
    wspec = lambda shape: pl.BlockSpec(shape, lambda i, bexp, nu: (layer, bexp[i], 0, 0))
    return pl.pallas_call(
        _expert_kernel,
        out_shape=jax.ShapeDtypeStruct(xs.shape, F32),
        grid_spec=pltpu.PrefetchScalarGridSpec(
            num_scalar_prefetch=2,
            grid=(xs.shape[0] // blk_rows,),
            in_specs=[pl.BlockSpec((blk_rows, width), blk),
                      wspec((1, 1, d, de)), wspec((1, 1, d, de)), wspec((1, 1, de, d))],
            out_specs=pl.BlockSpec((blk_rows, width), lambda i, bexp, nu: (i, 0)),
            scratch_shapes=[pltpu.VMEM((MOE_BLOCK, d), BF16), pltpu.VMEM((d, 2 * de), BF16),
                            pltpu.VMEM((de, d), BF16)]),
        compiler_params=_params("arbitrary"),
        name="moe_experts",
    )(block_exp, nused, xs, w1, w3, w2)


def _combine_kernel(code_ref, pad_start_ref, x_ref, route_ref, g_ref, gain_ref, ys_ref, o_ref, y0_ref,
                    y1_ref, sem, *, final):
    tm, d = x_ref.shape
    rows = _slot_rows(d)
    width = d // rows
    base = pl.program_id(0) * tm

    step = pl.program_id(0)
    slot = step % 2

    def copies(which, row0, op):
        def body(i, carry):
            for j in range(TOKENS_PER_TRIP):
                r = i * TOKENS_PER_TRIP + j
                for k, y_ref in enumerate((y0_ref, y1_ref)):
                    op(_token_copy(ys_ref, _slot(code_ref, pad_start_ref, 2 * (row0 + r) + k),
                                   y_ref.at[which], r, sem.at[which], rows), k)
            return carry
        lax.fori_loop(0, tm // TOKENS_PER_TRIP, body, 0)

    @pl.when(step == 0)
    def _():
        copies(slot, base, _start_copy)

    @pl.when(step + 1 < pl.num_programs(0))
    def _():
        copies(1 - slot, base + tm, _start_copy)

    copies(slot, base, _wait_copy)
    y0_rows, y1_rows = y0_ref.at[slot], y1_ref.at[slot]
    route = route_ref[...]
    g0, g1 = route[:, 4:5], route[:, 5:6]
    ssq = jnp.zeros((tm, 1), F32)
    for s in range(rows):
        cols = slice(s * width, (s + 1) * width)
        y = g0 * _token_cols(y0_rows, s, tm, rows) + g1 * _token_cols(y1_rows, s, tm, rows)
        out = x_ref[:, cols] + g_ref[0][:, cols] * y
        o_ref[:, cols] = out
        ssq = ssq + jnp.sum(out * out, axis=-1, keepdims=True)
    if final:
        o_ref[...] = o_ref[...] * lax.rsqrt(ssq * (1.0 / d) + EPS) * gain_ref[...]


def combine(code, pad_start, x, route, gate, gain, ys, rows_per_seg, final):
    t, d = x.shape
    rows = _slot_rows(d)
    width = d // rows
    tm = _tile(rows_per_seg, 256)
    return pl.pallas_call(
        functools.partial(_combine_kernel, final=final),
        out_shape=jax.ShapeDtypeStruct((t, d), F32),
        grid_spec=pltpu.PrefetchScalarGridSpec(
            num_scalar_prefetch=2,
            grid=(t // tm,),
            in_specs=[pl.BlockSpec((tm, d), lambda i, *_: (i, 0)),
                      pl.BlockSpec((tm, ROUTE_W), lambda i, *_: (i, 0)),
                      pl.BlockSpec((1, 1, d), lambda i, *_: ((i * tm) // rows_per_seg, 0, 0)),
                      pl.BlockSpec((1, d), lambda i, *_: (0, 0)),
                      pl.BlockSpec(memory_space=pl.ANY)],
            out_specs=pl.BlockSpec((tm, d), lambda i, *_: (i, 0)),
            scratch_shapes=[pltpu.VMEM((2, tm * rows, width), F32),
                            pltpu.VMEM((2, tm * rows, width), F32),
                            pltpu.SemaphoreType.DMA((2,))]),
        compiler_params=_params("arbitrary"),
        name="moe_combine",
    )(code, pad_start, x, route, gate, gain, ys)


def _slot_plan(route_list, counts):
    cnt = counts[0, N_GROUPS:N_GROUPS + N_EXPERTS].astype(jnp.int32)
    padded = (cnt + MOE_BLOCK - 1) // MOE_BLOCK * MOE_BLOCK
    pad_end = jnp.cumsum(padded)
    pad_start = pad_end - padded
    n_assign = 2 * sum(r.shape[0] for r in route_list)
    n_blocks = -(-n_assign // MOE_BLOCK) + N_EXPERTS
    block_start = jnp.arange(n_blocks, dtype=jnp.int32) * MOE_BLOCK
    block_exp = jnp.minimum(jnp.sum(block_start[:, None] >= pad_end[None, :], axis=1),
                            N_EXPERTS - 1).astype(jnp.int32)
    nused = (pad_end[-1:] // MOE_BLOCK).astype(jnp.int32)
    assert n_assign < (1 << SLOT_BITS)
    codes = [((r[:, 0:2].astype(jnp.int32) << SLOT_BITS) | r[:, 2:4].astype(jnp.int32)).reshape(-1)
             for r in route_list]
    return codes, pad_start.astype(jnp.int32), pad_end.astype(jnp.int32), block_exp, nused, n_blocks


def hier_moe_layer(streams, layer, wr, br, w1, w3, w2, gain, final):
    counts = jnp.zeros((1, ROUTE_W), F32)
    routes = []
    for x, sh, sc, _, rps in streams:
        route, counts = router(x, sh, sc, wr, br, counts, rps)
        routes.append(route)
    codes, pad_start, pad_end, block_exp, nused, n_blocks = _slot_plan(routes, counts)
    xs = None
    for (x, sh, sc, _, rps), code in zip(streams, codes):
        xs = dispatch(code, pad_start, pad_end, nused, n_blocks, x, sh, sc, xs, rps)
    ys = experts(block_exp, nused, xs, w1, w3, w2, layer)
    return [combine(code, pad_start, x, route, g, gain, ys, rps, final)
            for (x, _, _, g, rps), code, route in zip(streams, codes, routes)]


def kernel(x, c, ctx, c_ctx, mod_w, mod_b, hy_w_in, hy_conv_w, hy_conv_b, hy_f_w1, hy_f_b1,
           hy_f_freq1, hy_f_w2, hy_f_b2, hy_f_freq2, hy_f_w3, hy_bias, hy_w_out,
           at_w_qkv, at_sink, at_w_o, moe_wg, moe_bg, moe_we, moe_be, moe_w1, moe_w3, moe_w2,
           final_gain):
    batch, seq_len, d = x.shape
    ctx_len = ctx.shape[1]
    depth = mod_w.shape[0]
    d_q = at_w_qkv.shape[2] - 2 * N_KV_HEADS * HEAD_DIM
    d_kv = N_KV_HEADS * HEAD_DIM
    n_ctx = batch * ctx_len

    xl = x.reshape(batch * seq_len, d)
    xc = ctx.reshape(n_ctx, d)

    n_cond = batch + 1
    cond = jnp.concatenate([c, c_ctx[None, :], jnp.zeros((-n_cond % 8, d), F32)], axis=0)
    mods = adaln(cond, mod_w, mod_b)

    def mod_slices(i):
        parts = [mods[i, :, j * d:(j + 1) * d] for j in range(6)]
        lat = [p[:batch].reshape(batch, 1, d) for p in parts]
        cx = [p[batch:batch + 1].reshape(1, 1, d) for p in parts]
        return lat, cx

    def use_fft(n):
        return n >= 512 and (2 * n) % (FFT_N1 * 2 * SUB) == 0

    tables = {n: (fft_tables(n) if use_fft(n) else dft_tables(n)) for n in {seq_len, ctx_len}}
    rtabs = rope_tables(seq_len)
    gain = final_gain.reshape(1, d)

    for i in range(depth):
        last = i == depth - 1
        j = i // 2
        (sh1, sc1, g1, sh2, sc2, g2), (sh1c, sc1c, g1c, sh2c, sc2c, g2c) = mod_slices(i)
        if i % 2 == 0:
            w_in = hy_w_in[j].astype(BF16)
            w_out = hy_w_out[j].astype(BF16)
            filt = (hy_f_w1[j], hy_f_b1[j], hy_f_freq1[j], hy_f_w2[j], hy_f_b2[j], hy_f_freq2[j],
                    hy_f_w3[j])
            bias = hy_bias[j].reshape(2, 1, d)

            def mixer(xs_, sh, sc, g, n_seq, rps):
                u3 = modmm_conv(xs_, sh, sc, w_in, hy_conv_w[j], hy_conv_b[j], rps, n_seq)
                if use_fft(n_seq):
                    taps = hyena_filter_taps(n_seq, *filt, d, even_odd=False)
                    spec = fft_filter_spectrum(tables[n_seq], *taps)
                    z = hyena_long_convs_fft(u3, tables[n_seq], spec, bias, n_seq, d)
                else:
                    taps = hyena_filter_taps(n_seq, *filt, d, even_odd=True)
                    spec = filter_spectrum(*tables[n_seq], *taps)
                    z = hyena_long_convs(u3, tables[n_seq], spec, bias, n_seq, d)
                return mm_res(z, w_out, xs_, g, rps)

            xl = mixer(xl, sh1, sc1, g1, seq_len, seq_len)
            if not last:
                xc = mixer(xc, sh1c, sc1c, g1c, ctx_len, n_ctx)
        else:
            w_qkv = at_w_qkv[j].astype(BF16)
            w_o = at_w_o[j].astype(BF16)
            sink = at_sink[j]
            qkv = modmm(xl, sh1, sc1, w_qkv, seq_len, rope=(d_q, d_q + d_kv), rope_tabs=rtabs)
            qkv_c = modmm(xc, sh1c, sc1c, w_qkv, n_ctx)
            o = window_attention(qkv, qkv_c, sink, batch, seq_len, ctx_len, d_q, d_kv)
            xl = mm_res(o, w_o, xl, g1, seq_len)
            if not last:
                o_c = ctx_attention(qkv_c, sink, batch, ctx_len, d_q, d_kv)
                xc = mm_res(o_c, w_o, xc, g1c, n_ctx)

        wr = jnp.concatenate([moe_wg[i], moe_we[i],
                              jnp.zeros((d, ROUTE_W - N_GROUPS - N_EXPERTS), F32)], axis=1)
        br = jnp.concatenate([moe_bg[i], moe_be[i],
                              jnp.zeros((ROUTE_W - N_GROUPS - N_EXPERTS,), F32)]).reshape(1, ROUTE_W)
        streams = [(xl, sh2, sc2, g2, seq_len)]
        if not last:
            streams.append((xc, sh2c, sc2c, g2c, n_ctx))
        outs = hier_moe_layer(streams, i, wr, br, moe_w1, moe_w3, moe_w2, gain, last)
        xl = outs[0]
        if not last:
            xc = outs[1]

    return xl.reshape(batch, seq_len, d)
```
